```python
import math
import jax, jax.numpy as jnp
from jax import lax
import numpy as np

D_MODEL = 2048
BATCH = 4
SEQ = 2048
DEPTH = 2

D_RWKV = D_MODEL // 2
RWKV_HEAD = 64
RWKV_HEADS = D_RWKV // RWKV_HEAD
LORA_DECAY = max(32, int(round(1.8 * math.sqrt(D_MODEL) / 32)) * 32)
LORA_A = max(32, int(round(1.8 * math.sqrt(D_MODEL) / 32)) * 32)
LORA_V = max(32, int(round(1.3 * math.sqrt(D_MODEL) / 32)) * 32)
LORA_G = max(32, int(round(0.6 * D_MODEL ** 0.8 / 32)) * 32)
GN_EPS = 64e-5
D_POOL = D_MODEL // 2
POOL_WINDOWS = (2, 4, 8, 16)
N_POOL_GROUPS = len(POOL_WINDOWS)
POOL_GROUP = D_POOL // N_POOL_GROUPS
OFF_R = 0
OFF_K = OFF_R + D_RWKV
OFF_V = OFF_K + D_RWKV
OFF_WD = OFF_V + D_RWKV
OFF_AD = OFF_WD + LORA_DECAY
OFF_GD = OFF_AD + LORA_A
N_SHIFT = OFF_GD + LORA_G
OFF_POOL = N_SHIFT
OFF_GATE_R = OFF_POOL + D_POOL
OFF_GATE_P = OFF_GATE_R + D_MODEL
N_IN = OFF_GATE_P + D_MODEL
N_EXPERTS = 64
TOP_K = 8
N_GROUPS = 8
TOPK_GROUPS = 4
D_EXPERT = 512
D_SHARED = 512
ROUTED_SCALE = 2.5
EXPERT_BLOCK = 128
RMS_EPS = 1e-6

kernel_name = "hybrid_rwkv7_pool_moe_adaln"


def rms_norm(x, g):
    x32 = x.astype(jnp.float32)
    y = x32 * lax.rsqrt(jnp.mean(x32 * x32, axis=-1, keepdims=True) + RMS_EPS)
    return (y * g.astype(jnp.float32)).astype(x.dtype)


def token_shift(p):
    return jnp.pad(p, ((0, 0), (1, 0), (0, 0)))[:, :-1]


def causal_lerp(p, mu):
    return p + (token_shift(p) - p) * mu


def heads(t):
    return t.reshape(t.shape[0], t.shape[1], RWKV_HEADS, RWKV_HEAD)


def rwkv7_recurrence(r, decay, k, v, a_vec, b_vec):
    B, S, H, N = r.shape

    def step(state, inp):
        r_t, w_t, k_t, v_t, a_t, b_t = inp
        sa = jnp.einsum('bhvk,bhk->bhv', state, a_t)
        state = (state * w_t[:, :, None, :]
                 + sa[..., None] * b_t[:, :, None, :]
                 + v_t[..., None] * k_t[:, :, None, :])
        y_t = jnp.einsum('bhvk,bhk->bhv', state, r_t)
        return state, y_t

    state0 = jnp.zeros((B, H, N, N), jnp.float32)
    xs = tuple(jnp.moveaxis(t, 1, 0) for t in (r, decay, k, v, a_vec, b_vec))
    _, ys = lax.scan(step, state0, xs)
    return jnp.moveaxis(ys, 0, 1)


def multiscale_pool(p, pool_w, pool_scale):
    B, S, _ = p.shape
    p32 = p.astype(jnp.float32).reshape(B, S, N_POOL_GROUPS, POOL_GROUP)
    cs = jnp.cumsum(p32, axis=1)
    pos = jnp.arange(1, S + 1, dtype=jnp.float32)
    outs = []
    for gi, win in enumerate(POOL_WINDOWS):
        c_g = cs[:, :, gi]
        lag = jnp.pad(c_g, ((0, 0), (win, 0), (0, 0)))[:, :S]
        mean = (c_g - lag) / jnp.minimum(pos, float(win))[None, :, None]
        outs.append(mean - p32[:, :, gi])
    u = jnp.stack(outs, axis=2).astype(p.dtype)
    z = jnp.einsum('bsgc,gcd->bsgd', u, pool_w)
    return z.reshape(B, S, D_POOL) * pool_scale


def mixer_sublayer(h, w_in, mu_shift, w_decay_up, decay_base, w_a_up, a_base, w_g_up,
                   k_k, k_a, r_k, lnx_g, lnx_b, pool_w, pool_scale,
                   w_br_rwkv, w_br_pool, w_out, v_first, v_res):
    f32 = jnp.float32
    B, S, _ = h.shape
    proj = h @ w_in
    sp = causal_lerp(proj[..., :N_SHIFT], mu_shift)
    r = sp[..., OFF_R:OFF_K]
    k = sp[..., OFF_K:OFF_V]
    v = sp[..., OFF_V:OFF_WD]
    wd = sp[..., OFF_WD:OFF_AD]
    ad = sp[..., OFF_AD:OFF_GD]
    gd = sp[..., OFF_GD:N_SHIFT]
    pool_in = proj[..., OFF_POOL:OFF_GATE_R]
    gate_r = proj[..., OFF_GATE_R:OFF_GATE_P]
    gate_p = proj[..., OFF_GATE_P:]

    w_log = -jax.nn.softplus(-(decay_base + jnp.tanh(wd) @ w_decay_up).astype(f32)) - 0.5
    decay = jnp.exp(-jnp.exp(w_log))
    a = jax.nn.sigmoid((a_base + ad @ w_a_up).astype(f32))
    g = jax.nn.sigmoid(gd) @ w_g_up
    if v_res is None:
        v_first = v
    else:
        w_v_down, mu_v, w_v_up, v_base = v_res
        pv = causal_lerp(h @ w_v_down, mu_v)
        v = v + (v_first - v) * jax.nn.sigmoid(v_base + pv @ w_v_up)
    r32, k32, v32 = r.astype(f32), k.astype(f32), v.astype(f32)
    kk = heads(k32 * k_k.astype(f32))
    kk = kk / jnp.maximum(jnp.sqrt(jnp.sum(kk * kk, axis=-1, keepdims=True)), 1e-12)
    k32 = k32 * (1.0 + (a - 1.0) * k_a.astype(f32))
    rh, kh, vh, ah = heads(r32), heads(k32), heads(v32), heads(a)
    y = rwkv7_recurrence(rh, heads(decay), kh, vh, -kk, kk * ah)
    mu = jnp.mean(y, axis=-1, keepdims=True)
    var = jnp.mean(jnp.square(y - mu), axis=-1, keepdims=True)
    y = ((y - mu) * lax.rsqrt(var + GN_EPS)).reshape(B, S, D_RWKV)
    y = y * lnx_g.astype(f32) + lnx_b.astype(f32)
    bonus = jnp.sum(rh * kh * r_k.astype(f32), axis=-1, keepdims=True) * vh
    y_rwkv = ((y + bonus.reshape(B, S, D_RWKV)) * g.astype(f32)).astype(h.dtype)

    y_pool = multiscale_pool(pool_in, pool_w, pool_scale)

    merged = (jax.nn.sigmoid(gate_r) * (y_rwkv @ w_br_rwkv)
              + jax.nn.sigmoid(gate_p) * (y_pool @ w_br_pool))
    return merged @ w_out, v_first


def moe_sublayer(h, w_router, router_bias, w_exp_gate, w_exp_up, w_exp_down,
                 w_sh_gate, w_sh_up, w_sh_down):
    f32 = jnp.float32
    B, S, D = h.shape
    T = B * S
    hf = h.reshape(T, D)
    scores = jax.nn.sigmoid(hf.astype(f32) @ w_router.astype(f32))
    choice = scores + router_bias.astype(f32)
    per_group = N_EXPERTS // N_GROUPS
    grp_score = lax.top_k(choice.reshape(T, N_GROUPS, per_group), 2)[0].sum(-1)
    _, grp_idx = lax.top_k(grp_score, TOPK_GROUPS)
    grp_mask = jnp.any(grp_idx[..., None] == jnp.arange(N_GROUPS)[None, None, :], axis=1)
    masked = jnp.where(jnp.repeat(grp_mask, per_group, axis=-1), choice, -jnp.inf)
    _, idx = lax.top_k(masked, TOP_K)
    w_sel = jnp.take_along_axis(scores, idx, axis=-1)
    w_sel = w_sel / jnp.sum(w_sel, axis=-1, keepdims=True) * ROUTED_SCALE

    A = T * TOP_K
    flat_e = idx.reshape(A).astype(jnp.int32)
    flat_tok = jnp.arange(A, dtype=jnp.int32) // TOP_K
    flat_w = w_sel.reshape(A)
    order = jnp.argsort(flat_e)
    se = flat_e[order]
    counts = jnp.bincount(flat_e, length=N_EXPERTS).astype(jnp.int32)
    padded = ((counts + EXPERT_BLOCK - 1) // EXPERT_BLOCK) * EXPERT_BLOCK
    start = jnp.cumsum(counts) - counts
    pend = jnp.cumsum(padded)
    pstart = pend - padded
    dest = pstart[se] + (jnp.arange(A, dtype=jnp.int32) - start[se])
    n_rows = -(-A // EXPERT_BLOCK) * EXPERT_BLOCK + N_EXPERTS * EXPERT_BLOCK
    n_blocks = n_rows // EXPERT_BLOCK
    row_tok = jnp.zeros((n_rows,), jnp.int32).at[dest].set(flat_tok[order])
    row_w = jnp.zeros((n_rows,), h.dtype).at[dest].set(flat_w[order].astype(h.dtype))
    block_start = jnp.arange(n_blocks, dtype=jnp.int32) * EXPERT_BLOCK
    block_e = jnp.minimum(jnp.searchsorted(pend, block_start, side='right'), N_EXPERTS - 1)

    def block_fn(args):
        tok, wt, e = args
        xb = hf[tok]
        hid = jax.nn.silu(xb @ w_exp_gate[e]) * (xb @ w_exp_up[e])
        return (hid @ w_exp_down[e]) * wt[:, None]

    y_rows = lax.map(block_fn, (row_tok.reshape(n_blocks, EXPERT_BLOCK),
                                row_w.reshape(n_blocks, EXPERT_BLOCK), block_e))
    routed = jax.ops.segment_sum(y_rows.reshape(n_rows, D), row_tok, num_segments=T)
    shared = (jax.nn.silu(hf @ w_sh_gate) * (hf @ w_sh_up)) @ w_sh_down
    return (routed + shared).reshape(B, S, D)


def setup_inputs(seed: int = 0) -> dict:
    key = jax.random.key(seed)
    ks = iter(jax.random.split(key, 48))
    nrm = lambda shape, s: jax.random.normal(next(ks), shape, jnp.float32) * s
    uni = lambda shape, lo, hi: jax.random.uniform(next(ks), shape, jnp.float32, lo, hi)
    L, Lv, D = DEPTH, DEPTH - 1, D_MODEL
    return {
        "x": nrm((BATCH, SEQ, D), 1.0),
        "c": nrm((BATCH, D), 1.0),
        "ada_w": nrm((L, D, 6 * D), 0.5 * D ** -0.5),
        "ada_b": nrm((L, 6 * D), 0.02),
        "norm1_g": 1.0 + nrm((L, D), 0.02),
        "norm2_g": 1.0 + nrm((L, D), 0.02),
        "w_in": nrm((L, D, N_IN), D ** -0.5),
        "mu_shift": uni((L, N_SHIFT), 0.0, 1.0),
        "w_decay_up": nrm((L, LORA_DECAY, D_RWKV), 0.5 * LORA_DECAY ** -0.5),
        "decay_base": uni((L, D_RWKV), -6.0, 1.0),
        "w_a_up": nrm((L, LORA_A, D_RWKV), LORA_A ** -0.5),
        "a_base": nrm((L, D_RWKV), 0.5),
        "w_g_up": nrm((L, LORA_G, D_RWKV), LORA_G ** -0.5),
        "k_k": 0.85 + nrm((L, D_RWKV), 0.05),
        "k_a": 1.0 + nrm((L, D_RWKV), 0.05),
        "r_k": nrm((L, RWKV_HEADS, RWKV_HEAD), 0.1),
        "lnx_g": 1.0 + nrm((L, D_RWKV), 0.02),
        "lnx_b": nrm((L, D_RWKV), 0.02),
        "w_v_down": nrm((Lv, D, LORA_V), D ** -0.5),
        "mu_v": uni((Lv, LORA_V), 0.0, 1.0),
        "w_v_up": nrm((Lv, LORA_V, D_RWKV), LORA_V ** -0.5),
        "v_base": nrm((Lv, D_RWKV), 0.5),
        "pool_w": nrm((L, N_POOL_GROUPS, POOL_GROUP, POOL_GROUP), POOL_GROUP ** -0.5),
        "pool_scale": 1.0 + nrm((L, D_POOL), 0.1),
        "w_br_rwkv": nrm((L, D_RWKV, D), D_RWKV ** -0.5),
        "w_br_pool": nrm((L, D_POOL, D), D_POOL ** -0.5),
        "w_out": nrm((L, D, D), D ** -0.5),
        "w_router": nrm((L, D, N_EXPERTS), D ** -0.5),
        "router_bias": nrm((L, N_EXPERTS), 0.01),
        "w_exp_gate": nrm((L, N_EXPERTS, D, D_EXPERT), D ** -0.5),
        "w_exp_up": nrm((L, N_EXPERTS, D, D_EXPERT), D ** -0.5),
        "w_exp_down": nrm((L, N_EXPERTS, D_EXPERT, D), D_EXPERT ** -0.5),
        "w_sh_gate": nrm((L, D, D_SHARED), D ** -0.5),
        "w_sh_up": nrm((L, D, D_SHARED), D ** -0.5),
        "w_sh_down": nrm((L, D_SHARED, D), D_SHARED ** -0.5),
        "final_g": 1.0 + nrm((D,), 0.02),
    }


def reference(x, c, ada_w, ada_b, norm1_g, norm2_g, w_in, mu_shift, w_decay_up, decay_base,
              w_a_up, a_base, w_g_up, k_k, k_a, r_k, lnx_g, lnx_b, w_v_down, mu_v, w_v_up,
              v_base, pool_w, pool_scale, w_br_rwkv, w_br_pool, w_out, w_router, router_bias,
              w_exp_gate, w_exp_up, w_exp_down, w_sh_gate, w_sh_up, w_sh_down, final_g):
    cond = jax.nn.silu(c)
    v_first = None
    for l in range(DEPTH):
        mod = cond @ ada_w[l] + ada_b[l]
        sh1, sc1, g1, sh2, sc2, g2 = jnp.split(mod, 6, axis=-1)
        h = rms_norm(x, norm1_g[l]) * (1.0 + sc1[:, None]) + sh1[:, None]
        v_res = None if l == 0 else (w_v_down[l - 1], mu_v[l - 1], w_v_up[l - 1], v_base[l - 1])
        out, v_first = mixer_sublayer(h, w_in[l], mu_shift[l], w_decay_up[l], decay_base[l],
                                      w_a_up[l], a_base[l], w_g_up[l], k_k[l], k_a[l], r_k[l],
                                      lnx_g[l], lnx_b[l], pool_w[l], pool_scale[l],
                                      w_br_rwkv[l], w_br_pool[l], w_out[l], v_first, v_res)
        x = x + g1[:, None] * out
        h = rms_norm(x, norm2_g[l]) * (1.0 + sc2[:, None]) + sh2[:, None]
        x = x + g2[:, None] * moe_sublayer(h, w_router[l], router_bias[l], w_exp_gate[l],
                                           w_exp_up[l], w_exp_down[l], w_sh_gate[l],
                                           w_sh_up[l], w_sh_down[l])
    return rms_norm(x, final_g)
```

```python
import functools
import math

import jax
import jax.numpy as jnp
from jax import lax
from jax.experimental import pallas as pl
from jax.experimental.pallas import tpu as pltpu

F32 = jnp.float32
BF16 = jnp.bfloat16

HEAD = 64
PAIR = 2 * HEAD
CHUNK = 64
POOL_WINDOWS = (2, 4, 8, 16)
POOL_HIST = 16
N_EXPERTS = 64
TOP_K = 8
N_GROUPS = 8
TOPK_GROUPS = 4
ROUTED_SCALE = 2.5
GN_EPS = 64e-5
RMS_EPS = 1e-6
NEG_INF = float("-inf")

PROJ_TN = 1024
N_SHIFT_TILES = 4
EXPERT_TILE = 256
VMEM_LIMIT = 56 * 1024 * 1024

_NN = (((1,), (0,)), ((), ()))
_NT = (((1,), (1,)), ((), ()))


def _dot(a, b, dims=_NN):
    return lax.dot_general(a, b, dims, preferred_element_type=F32)


def _bdot(a, b, dims=_NN):
    return _dot(a.astype(BF16), b.astype(BF16), dims)


def _split(a):
    hi = a.astype(BF16)
    lo = (a - hi.astype(F32)).astype(BF16)
    return hi, lo


def _dot3s(ap, bp, dims=_NN):
    (ah, al), (bh, bl) = ap, bp
    return _dot(ah, bh, dims) + (_dot(ah, bl, dims) + _dot(al, bh, dims))


def _dot3(a, b, dims=_NN):
    return _dot3s(_split(a), _split(b), dims)


def _dot2c(a, c_bf16, dims=_NN):
    ah, al = _split(a)
    return _dot(ah, c_bf16, dims) + _dot(al, c_bf16, dims)


def _sigmoid(x):
    return 1.0 / (1.0 + jnp.exp(-x))


def _silu(x):
    return x * _sigmoid(x)


def _rms_mod(x, g, sc, sh):
    ms = jnp.mean(x * x, axis=-1, keepdims=True)
    return (x * lax.rsqrt(ms + RMS_EPS) * g) * (1.0 + sc) + sh


def _params(sem):
    return pltpu.CompilerParams(dimension_semantics=sem, vmem_limit_bytes=VMEM_LIMIT)


def _ada_kernel(c_ref, w_ref, b_ref, o_ref):
    cond = _silu(c_ref[...])
    o_ref[0] = _dot3(cond, w_ref[0]) + b_ref[0]


def _ada_call(c_pad, ada_w, ada_b):
    L, D, N6 = ada_w.shape
    tn = 1024
    return pl.pallas_call(
        _ada_kernel,
        grid=(L, N6 // tn),
        in_specs=[pl.BlockSpec(c_pad.shape, lambda l, j: (0, 0)),
                  pl.BlockSpec((1, D, tn), lambda l, j: (l, 0, j)),
                  pl.BlockSpec((1, 1, tn), lambda l, j: (l, 0, j))],
        out_specs=pl.BlockSpec((1, c_pad.shape[0], tn), lambda l, j: (l, 0, j)),
        out_shape=jax.ShapeDtypeStruct((L, c_pad.shape[0], N6), F32),
        compiler_params=_params(("arbitrary", "arbitrary")),
        name="adaln",
    )(c_pad, ada_w, ada_b.reshape(L, 1, N6))


def _proj_kernel(x_ref, g_ref, sc_ref, sh_ref, w_ref, mu_ref, o_ref, h_scr, prev_scr, *, tm, seq):
    i = pl.program_id(0)
    j = pl.program_id(1)

    @pl.when(j == 0)
    def _():
        h_scr[...] = _rms_mod(x_ref[...], g_ref[...], sc_ref[0], sh_ref[0]).astype(BF16)

    p = _dot(h_scr[...], w_ref[...])

    @pl.when(j < N_SHIFT_TILES)
    def _():
        seq_start = (i * tm) % seq == 0
        carry = jnp.where(seq_start, 0.0, prev_scr[j])
        row = lax.broadcasted_iota(jnp.int32, p.shape, 0)
        prev = jnp.where(row == 0, carry, pltpu.roll(p, 1, 0))
        o_ref[...] = p + (prev - p) * mu_ref[...]
        prev_scr[j] = p[tm - 1:tm, :]

    @pl.when(j >= N_SHIFT_TILES)
    def _():
        o_ref[...] = p


def _proj_call(x2, g, sc, sh, w_packed, mu_packed, seq, tm=512):
    T, D = x2.shape
    NP = w_packed.shape[1]
    nj = NP // PROJ_TN
    bmap = lambda i, j: ((i * tm) // seq, 0, 0)
    return pl.pallas_call(
        functools.partial(_proj_kernel, tm=tm, seq=seq),
        grid=(T // tm, nj),
        in_specs=[pl.BlockSpec((tm, D), lambda i, j: (i, 0)),
                  pl.BlockSpec((1, D), lambda i, j: (0, 0)),
                  pl.BlockSpec((1, 1, D), bmap),
                  pl.BlockSpec((1, 1, D), bmap),
                  pl.BlockSpec((D, PROJ_TN), lambda i, j: (0, j)),
                  pl.BlockSpec((1, PROJ_TN), lambda i, j: (0, j))],
        out_specs=pl.BlockSpec((tm, PROJ_TN), lambda i, j: (i, j)),
        out_shape=jax.ShapeDtypeStruct((T, NP), F32),
        scratch_shapes=[pltpu.VMEM((tm, D), BF16),
                        pltpu.VMEM((N_SHIFT_TILES, 1, PROJ_TN), F32)],
        compiler_params=_params(("arbitrary", "arbitrary")),
        name="proj",
    )(x2, g, sc, sh, w_packed, mu_packed)


def _stack(z, lane_lo):
    return jnp.concatenate([jnp.where(lane_lo, z, 0.0), jnp.where(lane_lo, 0.0, z)], axis=0)


def _rwkv_kernel(*refs, has_vres, seq):
    if has_vres:
        (r_ref, k_ref, v_ref, wd_ref, ad_ref, gd_ref, vd_ref, vf_ref,
         wdu_ref, wau_ref, wgu_ref, wvu_ref, vec_ref, y_ref, st_ref) = refs
    else:
        (r_ref, k_ref, v_ref, wd_ref, ad_ref, gd_ref,
         wdu_ref, wau_ref, wgu_ref, vec_ref, y_ref, st_ref) = refs
    C = CHUNK
    C2 = 2 * C
    lane = lax.broadcasted_iota(jnp.int32, (C, PAIR), 1)
    lane_lo = lane < HEAD
    ri = lax.broadcasted_iota(jnp.int32, (C2, C2), 0)
    ci = lax.broadcasted_iota(jnp.int32, (C2, C2), 1)
    strict = ri > ci
    incl = ri >= ci
    eye = jnp.where(ri == ci, 1.0, 0.0)
    same_head = (ri < HEAD) == (ci < HEAD)
    ones_head = jnp.where(same_head, 1.0, 0.0).astype(BF16)
    tri = (lax.broadcasted_iota(jnp.int32, (C, C), 0)
           >= lax.broadcasted_iota(jnp.int32, (C, C), 1))
    tri_ones = jnp.where(tri, 1.0, 0.0).astype(BF16)

    vec = vec_ref[...]
    decay_base, a_base, k_k, k_a = vec[0:1], vec[1:2], vec[2:3], vec[3:4]
    r_k, lnx_g, lnx_b, v_base = vec[4:5], vec[5:6], vec[6:7], vec[7:8]
    wdu = _split(wdu_ref[...])
    wau = _split(wau_ref[...])
    wgu = _split(wgu_ref[...])
    if has_vres:
        wvu = _split(wvu_ref[...])

    st_ref[...] = jnp.zeros_like(st_ref)

    def chunk_body(c, carry):
        rows = pl.ds(pl.multiple_of(c * C, C), C)
        r = r_ref[rows, :]
        k = k_ref[rows, :]
        v = v_ref[rows, :]
        dl = decay_base + _dot3s(_split(jnp.tanh(wd_ref[rows, :])), wdu)
        z = -dl
        softplus = jnp.maximum(z, 0.0) + jnp.log(1.0 + jnp.exp(-jnp.abs(z)))
        lw = -jnp.exp(-softplus - 0.5)
        a = _sigmoid(a_base + _dot3s(_split(ad_ref[rows, :]), wau))
        g = _dot3s(_split(_sigmoid(gd_ref[rows, :])), wgu)
        if has_vres:
            mix = _sigmoid(v_base + _dot3s(_split(vd_ref[rows, :]), wvu))
            v = v + (vf_ref[rows, :] - v) * mix
        kkr = k * k_k
        kk = kkr / jnp.maximum(jnp.sqrt(_dot2c(kkr * kkr, ones_head)), 1e-12)
        k2 = k * (1.0 + (a - 1.0) * k_a)
        avec = -kk
        bvec = kk * a

        lwh, lwl = _split(lw)
        cl = _dot(tri_ones, lwh) + _dot(tri_ones, lwl)
        cl_last = cl[C - 1:C, :]
        w_fwd = jnp.exp(cl)
        w_prev = jnp.exp(cl - lw)
        w_inv = jnp.exp(-cl)
        w_tail = jnp.exp(cl_last - cl)
        w_all = jnp.exp(cl_last)

        rt = _split(_stack(r * w_fwd, lane_lo))
        at = _split(_stack(avec * w_prev, lane_lo))
        bt = _split(_stack(bvec * w_inv, lane_lo))
        kt = _split(_stack(k2 * w_inv, lane_lo))
        v_st = _stack(v, lane_lo)
        vs = _split(v_st)

        a_ab = jnp.where(strict, _dot3s(at, bt, _NT), 0.0)
        a_ak = jnp.where(strict, _dot3s(at, kt, _NT), 0.0)
        a_rb = jnp.where(incl, _dot3s(rt, bt, _NT), 0.0)
        a_rk = jnp.where(incl, _dot3s(rt, kt, _NT), 0.0)

        pw = a_ab
        inv = eye + a_ab
        n_sq = int(math.log2(C)) - 1
        for _ in range(n_sq):
            pws = _split(pw)
            pw = _dot3s(pws, pws)
            inv = inv + _dot3s(_split(inv), _split(pw))

        s_prev = st_ref[...]
        ss = _split(s_prev)
        x_st = _dot3s(at, ss, _NT) + _dot3s(_split(a_ak), vs)
        u_st = _dot3(inv, x_st)
        us = _split(u_st)
        y_st = _dot3s(rt, ss, _NT) + _dot3s(_split(a_rb), us) + _dot3s(_split(a_rk), vs)
        y = y_st[:C] + y_st[C:]

        bh = _split(_stack(bvec * w_tail, lane_lo))
        kh = _split(_stack(k2 * w_tail, lane_lo))
        st_ref[...] = (s_prev * w_all
                       + _dot3s(_split(u_st.T), bh) + _dot3s(_split(v_st.T), kh))

        inv_n = 1.0 / HEAD
        mu = _dot2c(y, ones_head) * inv_n
        yc = y - mu
        var = _dot2c(yc * yc, ones_head) * inv_n
        yn = yc * lax.rsqrt(var + GN_EPS) * lnx_g + lnx_b
        bonus = _dot2c(r * k2 * r_k, ones_head) * v
        y_ref[rows, :] = (yn + bonus) * g
        return carry

    lax.fori_loop(0, seq // C, chunk_body, 0)


def _rwkv_call(proj, proj_first, lora_w, vecs, B, seq, has_vres):
    T = proj.shape[0]
    n_pairs = vecs.shape[1] // PAIR
    col = lambda off: (lambda b, p: (b, off + p))
    fixed = lambda blk: (lambda b, p: (b, blk))
    in_specs = [pl.BlockSpec((seq, PAIR), col(0)),
                pl.BlockSpec((seq, PAIR), col(8)),
                pl.BlockSpec((seq, PAIR), col(16)),
                pl.BlockSpec((seq, PAIR), fixed(24)),
                pl.BlockSpec((seq, PAIR), fixed(25)),
                pl.BlockSpec((seq, 2 * PAIR), fixed(13))]
    args = [proj, proj, proj, proj, proj, proj]
    if has_vres:
        in_specs += [pl.BlockSpec((seq, PAIR), fixed(28)),
                     pl.BlockSpec((seq, PAIR), col(16))]
        args += [proj, proj_first]
    wspec = lambda rows: pl.BlockSpec((rows, PAIR), lambda b, p: (0, p))
    in_specs += [wspec(PAIR), wspec(PAIR), wspec(2 * PAIR)]
    args += [lora_w["decay"], lora_w["a"], lora_w["g"]]
    if has_vres:
        in_specs += [wspec(PAIR)]
        args += [lora_w["v"]]
    in_specs += [pl.BlockSpec((8, PAIR), lambda b, p: (0, p))]
    args += [vecs]
    return pl.pallas_call(
        functools.partial(_rwkv_kernel, has_vres=has_vres, seq=seq),
        grid=(B, n_pairs),
        in_specs=in_specs,
        out_specs=pl.BlockSpec((seq, PAIR), lambda b, p: (b, p)),
        out_shape=jax.ShapeDtypeStruct((T, n_pairs * PAIR), F32),
        scratch_shapes=[pltpu.VMEM((PAIR, PAIR), F32)],
        compiler_params=_params(("arbitrary", "arbitrary")),
        name="rwkv7",
    )(*args)


def _merge_kernel(x_ref, y_ref, pool_ref, gr_ref, gp_ref, pw_ref, ps_ref, wbr_ref, wbp_ref, wo_ref,
                  g1_ref, o_ref, hist_ref, *, tm, seq):
    i = pl.program_id(0)
    seq_off = (i * tm) % seq
    p = pool_ref[...]
    hist = jnp.where(seq_off == 0, 0.0, hist_ref[...])
    ext = jnp.concatenate([hist, p], axis=0)
    pos = (seq_off + 1 + lax.broadcasted_iota(jnp.int32, (tm, 1), 0)).astype(F32)
    gw = p.shape[1] // len(POOL_WINDOWS)
    zs = []
    for gi, win in enumerate(POOL_WINDOWS):
        e = ext[:, gi * gw:(gi + 1) * gw]
        s = e
        shift = 1
        while shift < win:
            n = s.shape[0]
            s = s[shift:, :] + s[:n - shift, :]
            shift *= 2
        lo = POOL_HIST + 1 - win
        wsum = s[lo:lo + tm, :]
        u = wsum / jnp.minimum(pos, float(win)) - p[:, gi * gw:(gi + 1) * gw]
        zs.append(_bdot(u, pw_ref[gi]))
    y_pool = jnp.concatenate(zs, axis=1) * ps_ref[...]
    hist_ref[...] = p[tm - POOL_HIST:, :]

    br = _bdot(y_ref[...], wbr_ref[...])
    bp = _bdot(y_pool, wbp_ref[...])
    merged = _sigmoid(gr_ref[...]) * br + _sigmoid(gp_ref[...]) * bp
    o_ref[...] = x_ref[...] + g1_ref[0] * _bdot(merged, wo_ref[...])


def _merge_call(x2, y_rwkv, proj, pool_w, pool_scale, w_br_rwkv, w_br_pool, w_out, g1, seq, tm=256):
    T, D = x2.shape
    DR = y_rwkv.shape[1]
    const = lambda shape: pl.BlockSpec(shape, lambda i: (0,) * len(shape),
                                       pipeline_mode=pl.Buffered(1))
    return pl.pallas_call(
        functools.partial(_merge_kernel, tm=tm, seq=seq),
        grid=(T // tm,),
        in_specs=[pl.BlockSpec((tm, D), lambda i: (i, 0)),
                  pl.BlockSpec((tm, DR), lambda i: (i, 0)),
                  pl.BlockSpec((tm, DR), lambda i: (i, 8)),
                  pl.BlockSpec((tm, D), lambda i: (i, 2)),
                  pl.BlockSpec((tm, D), lambda i: (i, 3)),
                  const(pool_w.shape),
                  const(pool_scale.shape),
                  const(w_br_rwkv.shape),
                  const(w_br_pool.shape),
                  const(w_out.shape),
                  pl.BlockSpec((1, 1, D), lambda i: ((i * tm) // seq, 0, 0))],
        out_specs=pl.BlockSpec((tm, D), lambda i: (i, 0)),
        out_shape=jax.ShapeDtypeStruct((T, D), F32),
        scratch_shapes=[pltpu.VMEM((POOL_HIST, DR), F32)],
        compiler_params=_params(("arbitrary",)),
        name="merge",
    )(x2, y_rwkv, proj, proj, proj, pool_w, pool_scale, w_br_rwkv, w_br_pool, w_out, g1)


def _first_argmax(cur, idx, axis):
    m = jnp.max(cur, axis=axis, keepdims=True)
    first = jnp.min(jnp.where(cur == m, idx, 1 << 20), axis=axis, keepdims=True)
    return m, first


def _router_kernel(x_ref, g_ref, sc_ref, sh_ref, wr_ref, bias_ref, h_ref, gt_ref, idx_ref, *, tm):
    h = _rms_mod(x_ref[...], g_ref[...], sc_ref[0], sh_ref[0])
    h_ref[...] = h
    scores = _sigmoid(_dot3(wr_ref[...], h, _NT))
    choice = scores + bias_ref[...]
    per = N_EXPERTS // N_GROUPS
    ch3 = choice.reshape(N_GROUPS, per, tm)
    within = lax.broadcasted_iota(jnp.int32, ch3.shape, 1)
    m1, i1 = _first_argmax(ch3, within, 1)
    m2 = jnp.max(jnp.where(within == i1, NEG_INF, ch3), axis=1, keepdims=True)
    grp = (m1 + m2).reshape(N_GROUPS, tm)
    gidx = lax.broadcasted_iota(jnp.int32, grp.shape, 0)
    gsel = jnp.zeros(grp.shape, F32)
    for _ in range(TOPK_GROUPS):
        _, gi = _first_argmax(grp, gidx, 0)
        pick = gidx == gi
        gsel = jnp.where(pick, 1.0, gsel)
        grp = jnp.where(pick, NEG_INF, grp)
    gsel3 = jnp.broadcast_to(gsel.reshape(N_GROUPS, 1, tm), ch3.shape)
    cur = jnp.where(gsel3 > 0.5, ch3, NEG_INF).reshape(N_EXPERTS, tm)
    eidx = lax.broadcasted_iota(jnp.int32, cur.shape, 0)
    sel = jnp.zeros(cur.shape, F32)
    picks = []
    for _ in range(TOP_K):
        _, ei = _first_argmax(cur, eidx, 0)
        pick = eidx == ei
        sel = jnp.where(pick, 1.0, sel)
        cur = jnp.where(pick, NEG_INF, cur)
        picks.append(ei)
    w = jnp.where(sel > 0.5, scores, 0.0)
    gt_ref[...] = w / jnp.sum(w, axis=0, keepdims=True) * ROUTED_SCALE
    idx_ref[...] = jnp.concatenate(picks, axis=0)


def _router_call(x2, g, sc, sh, w_router_t, bias_col, seq, tm=256):
    T, D = x2.shape
    bmap = lambda i: ((i * tm) // seq, 0, 0)
    return pl.pallas_call(
        functools.partial(_router_kernel, tm=tm),
        grid=(T // tm,),
        in_specs=[pl.BlockSpec((tm, D), lambda i: (i, 0)),
                  pl.BlockSpec((1, D), lambda i: (0, 0)),
                  pl.BlockSpec((1, 1, D), bmap),
                  pl.BlockSpec((1, 1, D), bmap),
                  pl.BlockSpec((N_EXPERTS, D), lambda i: (0, 0)),
                  pl.BlockSpec((N_EXPERTS, 1), lambda i: (0, 0))],
        out_specs=[pl.BlockSpec((tm, D), lambda i: (i, 0)),
                   pl.BlockSpec((N_EXPERTS, tm), lambda i: (0, i)),
                   pl.BlockSpec((TOP_K, tm), lambda i: (0, i))],
        out_shape=[jax.ShapeDtypeStruct((T, D), F32),
                   jax.ShapeDtypeStruct((N_EXPERTS, T), F32),
                   jax.ShapeDtypeStruct((TOP_K, T), jnp.int32)],
        compiler_params=_params(("arbitrary",)),
        name="router",
    )(x2, g, sc, sh, w_router_t, bias_col)


def _row_copy(src, src_row, dst, dst_row, sem):
    return pltpu.make_async_copy(src.at[pl.ds(src_row, 1), :], dst.at[pl.ds(dst_row, 1), :], sem)


def _dispatch_kernel(dest_ref, h_ref, xs_ref, sem, *, tm):
    def issue(t, carry):
        for j in range(TOP_K):
            _row_copy(h_ref, t, xs_ref, dest_ref[t * TOP_K + j], sem).start()
        return carry

    lax.fori_loop(0, tm, issue, 0)
    for _ in range(TOP_K):
        pltpu.make_async_copy(h_ref, xs_ref.at[pl.ds(0, tm), :], sem).wait()


def _dispatch_call(dest_flat, h2, tm=256):
    T, D = h2.shape
    return pl.pallas_call(
        functools.partial(_dispatch_kernel, tm=tm),
        grid=(T // tm,),
        in_specs=[pl.BlockSpec((tm * TOP_K,), lambda i: (i,), memory_space=pltpu.SMEM),
                  pl.BlockSpec((tm, D), lambda i: (i, 0))],
        out_specs=pl.BlockSpec(memory_space=pl.ANY),
        out_shape=jax.ShapeDtypeStruct((T * TOP_K, D), F32),
        scratch_shapes=[pltpu.SemaphoreType.DMA(())],
        compiler_params=_params(("arbitrary",)),
        name="dispatch",
    )(dest_flat, h2)


def _expert_kernel(tile_ref, exp_ref, lo_ref, hi_ref, n_ref, x_ref, wg_ref, wu_ref, wd_ref, o_ref, *, te):
    s = pl.program_id(0)

    @pl.when(s < n_ref[0])
    def _():
        tile = tile_ref[s]
        x = x_ref[...].astype(BF16)
        hid = _silu(_dot(x, wg_ref[0].astype(BF16))) * _dot(x, wu_ref[0].astype(BF16))
        y = _bdot(hid, wd_ref[0])
        rows = tile * te + lax.broadcasted_iota(jnp.int32, (te, 1), 0)
        mine = (rows >= lo_ref[s]) & (rows < hi_ref[s])
        first_visit = (s == 0) | (tile_ref[jnp.maximum(s - 1, 0)] != tile)

        @pl.when(first_visit)
        def _():
            o_ref[...] = jnp.where(mine, y, 0.0)

        @pl.when(jnp.logical_not(first_visit))
        def _():
            o_ref[...] = jnp.where(mine, y, o_ref[...])


def _expert_call(meta, x_sorted, w_gate, w_up, w_down):
    A, D = x_sorted.shape
    E, _, DE = w_gate.shape
    te = EXPERT_TILE
    n_steps = A // te + E - 1
    grid_spec = pltpu.PrefetchScalarGridSpec(
        num_scalar_prefetch=5,
        grid=(n_steps,),
        in_specs=[pl.BlockSpec((te, D), lambda s, tile, exp, lo, hi, n: (tile[s], 0)),
                  pl.BlockSpec((1, D, DE), lambda s, tile, exp, lo, hi, n: (exp[s], 0, 0)),
                  pl.BlockSpec((1, D, DE), lambda s, tile, exp, lo, hi, n: (exp[s], 0, 0)),
                  pl.BlockSpec((1, DE, D), lambda s, tile, exp, lo, hi, n: (exp[s], 0, 0))],
        out_specs=pl.BlockSpec((te, D), lambda s, tile, exp, lo, hi, n: (tile[s], 0)),
    )
    return pl.pallas_call(
        functools.partial(_expert_kernel, te=te),
        grid_spec=grid_spec,
        out_shape=jax.ShapeDtypeStruct((A, D), F32),
        compiler_params=_params(("arbitrary",)),
        name="experts",
    )(*meta, x_sorted, w_gate, w_up, w_down)


def _expert_steps(counts, n_rows):
    te = EXPERT_TILE
    E = counts.shape[0]
    n_steps = n_rows // te + E - 1
    ends = jnp.cumsum(counts)
    starts = ends - counts
    first_tile = starts // te
    last_tile = jnp.maximum(ends - 1, 0) // te
    n_tiles = jnp.where(counts > 0, last_tile - first_tile + 1, 0)
    step_end = jnp.cumsum(n_tiles)
    step_start = step_end - n_tiles
    total = step_end[-1]
    s = jnp.arange(n_steps, dtype=jnp.int32)
    s_eff = jnp.minimum(s, total - 1)
    exp = jnp.searchsorted(step_end, s_eff, side="right").astype(jnp.int32)
    tile = (first_tile[exp] + (s_eff - step_start[exp])).astype(jnp.int32)
    return (tile, exp, starts[exp].astype(jnp.int32), ends[exp].astype(jnp.int32),
            total.reshape(1).astype(jnp.int32))


def _combine_kernel(dest_ref, x_ref, h_ref, w_ref, ys_ref, wsg_ref, wsu_ref, wsd_ref, g2_ref, fg_ref,
                    o_ref, buf, sem, *, tm, final_norm):
    def issue(t, carry):
        for j in range(TOP_K):
            _row_copy(ys_ref, dest_ref[t * TOP_K + j], buf.at[j], t, sem).start()
        return carry

    lax.fori_loop(0, tm, issue, 0)
    h = h_ref[...].astype(BF16)
    hid = _silu(_dot(h, wsg_ref[...])) * _dot(h, wsu_ref[...])
    acc = _bdot(hid, wsd_ref[...])
    for j in range(TOP_K):
        pltpu.make_async_copy(ys_ref.at[pl.ds(0, tm), :], buf.at[j], sem).wait()
    w = w_ref[...]
    for j in range(TOP_K):
        acc = acc + buf[j] * w[:, j:j + 1]
    out = x_ref[...] + g2_ref[0] * acc
    if final_norm:
        ms = jnp.mean(out * out, axis=-1, keepdims=True)
        out = out * lax.rsqrt(ms + RMS_EPS) * fg_ref[...]
    o_ref[...] = out


def _combine_call(dest_flat, x2, h2, w_tok, y_sorted, w_sh_gate, w_sh_up, w_sh_down, g2, final_g,
                  seq, final_norm, tm=256):
    T, D = x2.shape
    const = lambda shape: pl.BlockSpec(shape, lambda i: (0,) * len(shape),
                                       pipeline_mode=pl.Buffered(1))
    return pl.pallas_call(
        functools.partial(_combine_kernel, tm=tm, final_norm=final_norm),
        grid=(T // tm,),
        in_specs=[pl.BlockSpec((tm * TOP_K,), lambda i: (i,), memory_space=pltpu.SMEM),
                  pl.BlockSpec((tm, D), lambda i: (i, 0)),
                  pl.BlockSpec((tm, D), lambda i: (i, 0)),
                  pl.BlockSpec((tm, TOP_K), lambda i: (i, 0)),
                  pl.BlockSpec(memory_space=pl.ANY),
                  const(w_sh_gate.shape),
                  const(w_sh_up.shape),
                  const(w_sh_down.shape),
                  pl.BlockSpec((1, 1, D), lambda i: ((i * tm) // seq, 0, 0)),
                  const(final_g.shape)],
        out_specs=pl.BlockSpec((tm, D), lambda i: (i, 0)),
        out_shape=jax.ShapeDtypeStruct((T, D), F32),
        scratch_shapes=[pltpu.VMEM((TOP_K, tm, D), F32), pltpu.SemaphoreType.DMA(())],
        compiler_params=_params(("arbitrary",)),
        name="combine",
    )(dest_flat, x2, h2, w_tok, y_sorted, w_sh_gate, w_sh_up, w_sh_down, g2, final_g)


def _pad_cols(w, n):
    return jnp.pad(w, ((0, 0), (0, n - w.shape[1])))


def _pad_rows(w, n):
    return jnp.pad(w, ((0, n - w.shape[0]), (0, 0)))


def _pack_cols(parts):
    return jnp.concatenate([_pad_cols(a, n) for a, n in parts], axis=1)


def kernel(x, c, ada_w, ada_b, norm1_g, norm2_g, w_in, mu_shift, w_decay_up, decay_base, w_a_up, a_base, w_g_up, k_k, k_a, r_k, lnx_g, lnx_b, w_v_down, mu_v, w_v_up, v_base, pool_w, pool_scale, w_br_rwkv, w_br_pool, w_out, w_router, router_bias, w_exp_gate, w_exp_up, w_exp_down, w_sh_gate, w_sh_up, w_sh_down, final_g):
    B, S, D = x.shape
    T = B * S
    L = ada_w.shape[0]
    DR = w_decay_up.shape[2]
    n_dec, n_a, n_g, n_v = w_decay_up.shape[1], w_a_up.shape[1], w_g_up.shape[1], w_v_down.shape[2]
    o_k, o_v, o_wd = DR, 2 * DR, 3 * DR
    o_ad = o_wd + n_dec
    o_gd = o_ad + n_a
    n_shift = o_gd + n_g
    o_gate = n_shift + DR

    c_pad = jnp.pad(c, ((0, 8 - B), (0, 0)))
    mod = _ada_call(c_pad, ada_w, ada_b)[:, :B]
    mod = mod.reshape(L, B, 6, 1, D)

    x2 = x.reshape(T, D)
    proj_first = None
    for l in range(L):
        sh1, sc1, g1, sh2, sc2, g2 = (mod[l, :, q] for q in range(6))
        has_vres = l > 0
        wl = w_in[l]
        zero_cols = jnp.zeros((D, 0), F32)
        vd_w = w_v_down[l - 1] if has_vres else zero_cols
        w_packed = _pack_cols([
            (wl[:, :o_wd], 3 * DR),
            (wl[:, o_wd:o_ad], PAIR), (wl[:, o_ad:o_gd], PAIR), (wl[:, o_gd:n_shift], 2 * PAIR),
            (vd_w, PAIR), (zero_cols, PROJ_TN - 5 * PAIR),
            (wl[:, o_gate:], 2 * D),
            (wl[:, n_shift:o_gate], DR)]).astype(BF16)
        ml = mu_shift[l][None, :]
        zero_mu = jnp.zeros((1, 0), F32)
        vd_mu = mu_v[l - 1][None, :] if has_vres else zero_mu
        mu_packed = _pack_cols([
            (ml[:, :o_wd], 3 * DR),
            (ml[:, o_wd:o_ad], PAIR), (ml[:, o_ad:o_gd], PAIR), (ml[:, o_gd:n_shift], 2 * PAIR),
            (vd_mu, PAIR), (zero_mu, PROJ_TN - 5 * PAIR), (zero_mu, 2 * D + DR)])
        proj = _proj_call(x2, norm1_g[l][None, :], sc1, sh1, w_packed, mu_packed, S)
        if l == 0:
            proj_first = proj

        lora_w = {"decay": _pad_rows(w_decay_up[l], PAIR), "a": _pad_rows(w_a_up[l], PAIR),
                  "g": w_g_up[l]}
        zeros_dr = jnp.zeros((DR,), F32)
        if has_vres:
            lora_w["v"] = _pad_rows(w_v_up[l - 1], PAIR)
        vecs = jnp.stack([decay_base[l], a_base[l], k_k[l], k_a[l], r_k[l].reshape(DR), lnx_g[l],
                          lnx_b[l], v_base[l - 1] if has_vres else zeros_dr])
        y_rwkv = _rwkv_call(proj, proj_first, lora_w, vecs, B, S, has_vres)

        x2 = _merge_call(x2, y_rwkv, proj, pool_w[l].astype(BF16), pool_scale[l][None, :],
                         w_br_rwkv[l].astype(BF16), w_br_pool[l].astype(BF16),
                         w_out[l].astype(BF16), g1, S)

        h2, gate_t, idx_t = _router_call(x2, norm2_g[l][None, :], sc2, sh2, w_router[l].T,
                                         router_bias[l][:, None], S)
        experts = jnp.arange(N_EXPERTS, dtype=jnp.int32)[None, :, None]
        sel = jnp.any(idx_t[:, None, :] == experts, axis=0).astype(jnp.int32)
        incl = jnp.cumsum(sel, axis=1)
        counts = incl[:, -1]
        start = jnp.cumsum(counts) - counts
        slot = start[:, None] + incl - sel
        dest = jnp.take_along_axis(slot, idx_t, axis=0)
        dest_flat = dest.T.reshape(T * TOP_K).astype(jnp.int32)
        w_tok = jnp.take_along_axis(gate_t, idx_t, axis=0).T

        x_sorted = _dispatch_call(dest_flat, h2)
        meta = _expert_steps(counts, T * TOP_K)
        y_sorted = _expert_call(meta, x_sorted, w_exp_gate[l], w_exp_up[l], w_exp_down[l])
        x2 = _combine_call(dest_flat, x2, h2, w_tok, y_sorted, w_sh_gate[l].astype(BF16),
                           w_sh_up[l].astype(BF16), w_sh_down[l].astype(BF16), g2,
                           final_g[None, :], S, final_norm=(l == L - 1))
    return x2.reshape(B, S, D)
```

```python
import functools
import math

import jax
import jax.numpy as jnp
from jax import lax
from jax.experimental import pallas as pl
from jax.experimental.pallas import tpu as pltpu

F32 = jnp.float32
BF16 = jnp.bfloat16

HEAD = 64
PAIR = 2 * HEAD
CHUNK = 64
POOL_WINDOWS = (2, 4, 8, 16)
POOL_HIST = 16
N_EXPERTS = 64
TOP_K = 8
N_GROUPS = 8
TOPK_GROUPS = 4
ROUTED_SCALE = 2.5
GN_EPS = 64e-5
RMS_EPS = 1e-6
NEG_INF = float("-inf")

PROJ_TN = 1024
N_SHIFT_TILES = 4
EXPERT_TILE = 256
VMEM_LIMIT = 56 * 1024 * 1024

_NN = (((1,), (0,)), ((), ()))
_NT = (((1,), (1,)), ((), ()))


def _dot(a, b, dims=_NN):
    return lax.dot_general(a, b, dims, preferred_element_type=F32)


def _bdot(a, b, dims=_NN):
    return _dot(a.astype(BF16), b.astype(BF16), dims)


def _split(a):
    hi = a.astype(BF16)
    lo = (a - hi.astype(F32)).astype(BF16)
    return hi, lo


def _dot3s(ap, bp, dims=_NN):
    (ah, al), (bh, bl) = ap, bp
    return _dot(ah, bh, dims) + (_dot(ah, bl, dims) + _dot(al, bh, dims))


def _dot3(a, b, dims=_NN):
    return _dot3s(_split(a), _split(b), dims)


def _dot2c(a, c_bf16, dims=_NN):
    ah, al = _split(a)
    return _dot(ah, c_bf16, dims) + _dot(al, c_bf16, dims)


def _sigmoid(x):
    return 1.0 / (1.0 + jnp.exp(-x))


def _silu(x):
    return x * _sigmoid(x)


def _rms_mod(x, g, sc, sh):
    ms = jnp.mean(x * x, axis=-1, keepdims=True)
    return (x * lax.rsqrt(ms + RMS_EPS) * g) * (1.0 + sc) + sh


def _params(sem):
    return pltpu.CompilerParams(dimension_semantics=sem, vmem_limit_bytes=VMEM_LIMIT)


def _ada_kernel(c_ref, w_ref, b_ref, o_ref):
    cond = _silu(c_ref[...])
    o_ref[0] = _dot3(cond, w_ref[0]) + b_ref[0]


def _ada_call(c_pad, ada_w, ada_b):
    L, D, N6 = ada_w.shape
    tn = 1024
    return pl.pallas_call(
        _ada_kernel,
        grid=(L, N6 // tn),
        in_specs=[pl.BlockSpec(c_pad.shape, lambda l, j: (0, 0)),
                  pl.BlockSpec((1, D, tn), lambda l, j: (l, 0, j)),
                  pl.BlockSpec((1, 1, tn), lambda l, j: (l, 0, j))],
        out_specs=pl.BlockSpec((1, c_pad.shape[0], tn), lambda l, j: (l, 0, j)),
        out_shape=jax.ShapeDtypeStruct((L, c_pad.shape[0], N6), F32),
        compiler_params=_params(("arbitrary", "arbitrary")),
        name="adaln",
    )(c_pad, ada_w, ada_b.reshape(L, 1, N6))


def _proj_kernel(x_ref, g_ref, sc_ref, sh_ref, w_ref, mu_ref, o_ref, h_scr, prev_scr, *, tm, seq):
    i = pl.program_id(0)
    j = pl.program_id(1)

    @pl.when(j == 0)
    def _():
        h_scr[...] = _rms_mod(x_ref[...], g_ref[...], sc_ref[0], sh_ref[0]).astype(BF16)

    p = _dot(h_scr[...], w_ref[...])

    @pl.when(j < N_SHIFT_TILES)
    def _():
        seq_start = (i * tm) % seq == 0
        carry = jnp.where(seq_start, 0.0, prev_scr[j])
        row = lax.broadcasted_iota(jnp.int32, p.shape, 0)
        prev = jnp.where(row == 0, carry, pltpu.roll(p, 1, 0))
        o_ref[...] = p + (prev - p) * mu_ref[...]
        prev_scr[j] = p[tm - 1:tm, :]

    @pl.when(j >= N_SHIFT_TILES)
    def _():
        o_ref[...] = p


def _proj_call(x2, g, sc, sh, w_packed, mu_packed, seq, tm=512):
    T, D = x2.shape
    NP = w_packed.shape[1]
    nj = NP // PROJ_TN
    bmap = lambda i, j: ((i * tm) // seq, 0, 0)
    return pl.pallas_call(
        functools.partial(_proj_kernel, tm=tm, seq=seq),
        grid=(T // tm, nj),
        in_specs=[pl.BlockSpec((tm, D), lambda i, j: (i, 0)),
                  pl.BlockSpec((1, D), lambda i, j: (0, 0)),
                  pl.BlockSpec((1, 1, D), bmap),
                  pl.BlockSpec((1, 1, D), bmap),
                  pl.BlockSpec((D, PROJ_TN), lambda i, j: (0, j)),
                  pl.BlockSpec((1, PROJ_TN), lambda i, j: (0, j))],
        out_specs=pl.BlockSpec((tm, PROJ_TN), lambda i, j: (i, j)),
        out_shape=jax.ShapeDtypeStruct((T, NP), F32),
        scratch_shapes=[pltpu.VMEM((tm, D), BF16),
                        pltpu.VMEM((N_SHIFT_TILES, 1, PROJ_TN), F32)],
        compiler_params=_params(("arbitrary", "arbitrary")),
        name="proj",
    )(x2, g, sc, sh, w_packed, mu_packed)


RWKV_PASSES = {"lora": 1, "gram": 1, "inv": 1, "apply": 1}
RWKV_BATCH = 4
RWKV_ROWS = 512


def _splitp(a, passes):
    hi = a.astype(BF16)
    if passes == 1:
        return hi, None
    return hi, (a - hi.astype(F32)).astype(BF16)


def _mm(ap, bp, passes, dims=_NN):
    (ah, al), (bh, bl) = ap, bp
    out = _dot(ah, bh, dims)
    if passes == 1:
        return out
    lo = None
    if bl is not None:
        lo = _dot(ah, bl, dims)
    if al is not None:
        t = _dot(al, bh, dims)
        lo = t if lo is None else lo + t
    return out if lo is None else out + lo


def _stack(z, lane_lo):
    return jnp.concatenate([jnp.where(lane_lo, z, 0.0), jnp.where(lane_lo, 0.0, z)], axis=0)


def _rwkv_kernel(*refs, has_vres, nb, rg):
    if has_vres:
        (r_ref, k_ref, v_ref, wd_ref, ad_ref, gd_ref, vd_ref, vf_ref,
         wdu_ref, wau_ref, wgu_ref, wvu_ref, vec_ref, y_ref, st_ref) = refs
    else:
        (r_ref, k_ref, v_ref, wd_ref, ad_ref, gd_ref,
         wdu_ref, wau_ref, wgu_ref, vec_ref, y_ref, st_ref) = refs
    p_lora, p_gram, p_inv, p_app = (RWKV_PASSES[s] for s in ("lora", "gram", "inv", "apply"))
    p_in = max(p_gram, p_app)
    C = CHUNK
    C2 = 2 * C

    @pl.when(pl.program_id(2) == 0)
    def _():
        st_ref[...] = jnp.zeros_like(st_ref)

    lane_lo = lax.broadcasted_iota(jnp.int32, (C, PAIR), 1) < HEAD
    ri = lax.broadcasted_iota(jnp.int32, (C2, C2), 0)
    ci = lax.broadcasted_iota(jnp.int32, (C2, C2), 1)
    strict = ri > ci
    incl = ri >= ci
    eye = jnp.where(ri == ci, 1.0, 0.0)
    ones_head = jnp.where((ri < HEAD) == (ci < HEAD), 1.0, 0.0).astype(BF16)
    rb = lax.broadcasted_iota(jnp.int32, (nb * C, nb * C), 0)
    cb = lax.broadcasted_iota(jnp.int32, (nb * C, nb * C), 1)
    same_seq = (rb // C) == (cb // C)
    tri_blk = jnp.where(same_seq & (rb >= cb), 1.0, 0.0).astype(BF16)
    ones_blk = jnp.where(same_seq, 1.0, 0.0).astype(BF16)

    vec = vec_ref[...]
    decay_base, a_base, k_k, k_a = vec[0:1], vec[1:2], vec[2:3], vec[3:4]
    r_k, lnx_g, lnx_b, v_base = vec[4:5], vec[5:6], vec[6:7], vec[7:8]
    wdu = _splitp(wdu_ref[...], p_lora)
    wau = _splitp(wau_ref[...], p_lora)
    wgu = _splitp(wgu_ref[...], p_lora)
    if has_vres:
        wvu = _splitp(wvu_ref[...], p_lora)

    seqs = range(nb)

    def chunk_body(c, carry):
        rows = pl.ds(pl.multiple_of(c * C, C), C)
        cat = lambda ref: jnp.concatenate([ref[b, rows, :] for b in seqs], axis=0)
        part = lambda x, b: x[b * C:(b + 1) * C]
        r, k, v = cat(r_ref), cat(k_ref), cat(v_ref)
        dl = decay_base + _mm(_splitp(jnp.tanh(cat(wd_ref)), p_lora), wdu, p_lora)
        z = -dl
        softplus = jnp.maximum(z, 0.0) + jnp.log(1.0 + jnp.exp(-jnp.abs(z)))
        lw = -jnp.exp(-softplus - 0.5)
        a = _sigmoid(a_base + _mm(_splitp(cat(ad_ref), p_lora), wau, p_lora))
        g = _mm(_splitp(_sigmoid(cat(gd_ref)), p_lora), wgu, p_lora)
        if has_vres:
            mix = _sigmoid(v_base + _mm(_splitp(cat(vd_ref), p_lora), wvu, p_lora))
            v = v + (cat(vf_ref) - v) * mix
        kkr = k * k_k
        kk = kkr / jnp.maximum(jnp.sqrt(_dot2c(kkr * kkr, ones_head)), 1e-12)
        k2 = k * (1.0 + (a - 1.0) * k_a)
        avec = -kk
        bvec = kk * a

        lwh, lwl = _split(lw)
        cl = _dot(tri_blk, lwh) + _dot(tri_blk, lwl)
        tot = _dot(ones_blk, lwh) + _dot(ones_blk, lwl)
        w_inv = jnp.exp(-cl)
        w_tail = jnp.exp(tot - cl)
        w_tot = jnp.exp(tot)
        a_t = avec * jnp.exp(cl - lw)
        r_t = r * jnp.exp(cl)
        b_t = bvec * w_inv
        k_t = k2 * w_inv
        b_h = bvec * w_tail
        k_h = k2 * w_tail

        pair2 = lambda x, y, b: jnp.concatenate([_stack(part(x, b), lane_lo),
                                                 _stack(part(y, b), lane_lo)], axis=0)
        lhs = [_splitp(pair2(a_t, r_t, b), p_in) for b in seqs]
        rhs = [_splitp(pair2(b_t, k_t, b), p_gram) for b in seqs]
        gram = [_mm(lhs[b], rhs[b], p_gram, _NT) for b in seqs]
        a_ab = [jnp.where(strict, gm[:C2, :C2], 0.0) for gm in gram]
        a_ak = [jnp.where(strict, gm[:C2, C2:], 0.0) for gm in gram]
        a_r = [jnp.concatenate([jnp.where(incl, gm[C2:, :C2], 0.0),
                                jnp.where(incl, gm[C2:, C2:], 0.0)], axis=1) for gm in gram]

        pw = a_ab
        inv = [eye + m for m in a_ab]
        for _ in range(int(math.log2(C)) - 1):
            pws = [_splitp(m, p_inv) for m in pw]
            pw = [_mm(m, m, p_inv) for m in pws]
            inv = [inv[b] + _mm(_splitp(inv[b], p_inv), _splitp(pw[b], p_inv), p_inv) for b in seqs]

        s_prev = [st_ref[b] for b in seqs]
        sx = [_mm(lhs[b], _splitp(s_prev[b], p_app), p_app, _NT) for b in seqs]
        v_st = [_stack(part(v, b), lane_lo) for b in seqs]
        x_st = [sx[b][:C2] + _mm(_splitp(a_ak[b], p_app), _splitp(v_st[b], p_app), p_app) for b in seqs]
        u_st = [_mm(_splitp(inv[b], p_app), _splitp(x_st[b], p_app), p_app) for b in seqs]
        uv = [jnp.concatenate([u_st[b], v_st[b]], axis=0) for b in seqs]
        y_st = [sx[b][C2:] + _mm(_splitp(a_r[b], p_app), _splitp(uv[b], p_app), p_app) for b in seqs]
        for b in seqs:
            bk = _splitp(pair2(b_h, k_h, b), p_app)
            st_ref[b] = (s_prev[b] * w_tot[b * C:b * C + 1]
                         + _mm(_splitp(uv[b].T, p_app), bk, p_app))
        y = jnp.concatenate([ys[:C] + ys[C:] for ys in y_st], axis=0)

        inv_n = 1.0 / HEAD
        mu = _dot2c(y, ones_head) * inv_n
        yc = y - mu
        var = _dot2c(yc * yc, ones_head) * inv_n
        yn = yc * lax.rsqrt(var + GN_EPS) * lnx_g + lnx_b
        bonus = _dot2c(r * k2 * r_k, ones_head) * v
        out = (yn + bonus) * g
        for b in seqs:
            y_ref[b, rows, :] = part(out, b)
        return carry

    lax.fori_loop(0, rg // C, chunk_body, 0)


def _rwkv_call(proj, proj_first, lora_w, vecs, has_vres):
    B, S, _ = proj.shape
    nb, rg = min(RWKV_BATCH, B), min(RWKV_ROWS, S)
    n_pairs = vecs.shape[1] // PAIR
    col = lambda off: (lambda bb, p, t: (bb, t, off + p))
    fixed = lambda blk: (lambda bb, p, t: (bb, t, blk))
    blk = lambda w, imap: pl.BlockSpec((nb, rg, w), imap)
    in_specs = [blk(PAIR, col(0)),
                blk(PAIR, col(8)),
                blk(PAIR, col(16)),
                blk(PAIR, fixed(24)),
                blk(PAIR, fixed(25)),
                blk(2 * PAIR, fixed(13))]
    args = [proj] * 6
    if has_vres:
        in_specs += [blk(PAIR, fixed(28)),
                     blk(PAIR, col(16))]
        args += [proj, proj_first]
    wspec = lambda rows: pl.BlockSpec((rows, PAIR), lambda bb, p, t: (0, p))
    in_specs += [wspec(PAIR), wspec(PAIR), wspec(2 * PAIR)]
    args += [lora_w["decay"], lora_w["a"], lora_w["g"]]
    if has_vres:
        in_specs += [wspec(PAIR)]
        args += [lora_w["v"]]
    in_specs += [pl.BlockSpec((8, PAIR), lambda bb, p, t: (0, p))]
    args += [vecs]
    return pl.pallas_call(
        functools.partial(_rwkv_kernel, has_vres=has_vres, nb=nb, rg=rg),
        grid=(B // nb, n_pairs, S // rg),
        in_specs=in_specs,
        out_specs=pl.BlockSpec((nb, rg, PAIR), lambda bb, p, t: (bb, t, p)),
        out_shape=jax.ShapeDtypeStruct((B, S, n_pairs * PAIR), F32),
        scratch_shapes=[pltpu.VMEM((nb, PAIR, PAIR), F32)],
        compiler_params=_params(("arbitrary", "arbitrary", "arbitrary")),
        name="rwkv7",
    )(*args)


def _merge_kernel(x_ref, y_ref, pool_ref, gr_ref, gp_ref, pw_ref, ps_ref, wbr_ref, wbp_ref, wo_ref,
                  g1_ref, o_ref, hist_ref, *, tm, seq):
    i = pl.program_id(0)
    seq_off = (i * tm) % seq
    p = pool_ref[...]
    hist = jnp.where(seq_off == 0, 0.0, hist_ref[...])
    ext = jnp.concatenate([hist, p], axis=0)
    pos = (seq_off + 1 + lax.broadcasted_iota(jnp.int32, (tm, 1), 0)).astype(F32)
    gw = p.shape[1] // len(POOL_WINDOWS)
    zs = []
    for gi, win in enumerate(POOL_WINDOWS):
        e = ext[:, gi * gw:(gi + 1) * gw]
        s = e
        shift = 1
        while shift < win:
            n = s.shape[0]
            s = s[shift:, :] + s[:n - shift, :]
            shift *= 2
        lo = POOL_HIST + 1 - win
        wsum = s[lo:lo + tm, :]
        u = wsum / jnp.minimum(pos, float(win)) - p[:, gi * gw:(gi + 1) * gw]
        zs.append(_bdot(u, pw_ref[gi]))
    y_pool = jnp.concatenate(zs, axis=1) * ps_ref[...]
    hist_ref[...] = p[tm - POOL_HIST:, :]

    br = _bdot(y_ref[...], wbr_ref[...])
    bp = _bdot(y_pool, wbp_ref[...])
    merged = _sigmoid(gr_ref[...]) * br + _sigmoid(gp_ref[...]) * bp
    o_ref[...] = x_ref[...] + g1_ref[0] * _bdot(merged, wo_ref[...])


def _merge_call(x2, y_rwkv, proj, pool_w, pool_scale, w_br_rwkv, w_br_pool, w_out, g1, seq, tm=256):
    T, D = x2.shape
    DR = y_rwkv.shape[1]
    const = lambda shape: pl.BlockSpec(shape, lambda i: (0,) * len(shape),
                                       pipeline_mode=pl.Buffered(1))
    return pl.pallas_call(
        functools.partial(_merge_kernel, tm=tm, seq=seq),
        grid=(T // tm,),
        in_specs=[pl.BlockSpec((tm, D), lambda i: (i, 0)),
                  pl.BlockSpec((tm, DR), lambda i: (i, 0)),
                  pl.BlockSpec((tm, DR), lambda i: (i, 8)),
                  pl.BlockSpec((tm, D), lambda i: (i, 2)),
                  pl.BlockSpec((tm, D), lambda i: (i, 3)),
                  const(pool_w.shape),
                  const(pool_scale.shape),
                  const(w_br_rwkv.shape),
                  const(w_br_pool.shape),
                  const(w_out.shape),
                  pl.BlockSpec((1, 1, D), lambda i: ((i * tm) // seq, 0, 0))],
        out_specs=pl.BlockSpec((tm, D), lambda i: (i, 0)),
        out_shape=jax.ShapeDtypeStruct((T, D), F32),
        scratch_shapes=[pltpu.VMEM((POOL_HIST, DR), F32)],
        compiler_params=_params(("arbitrary",)),
        name="merge",
    )(x2, y_rwkv, proj, proj, proj, pool_w, pool_scale, w_br_rwkv, w_br_pool, w_out, g1)


def _first_argmax(cur, idx, axis):
    m = jnp.max(cur, axis=axis, keepdims=True)
    first = jnp.min(jnp.where(cur == m, idx, 1 << 20), axis=axis, keepdims=True)
    return m, first


def _router_kernel(x_ref, g_ref, sc_ref, sh_ref, wr_ref, bias_ref, h_ref, gt_ref, idx_ref, *, tm):
    h = _rms_mod(x_ref[...], g_ref[...], sc_ref[0], sh_ref[0])
    h_ref[...] = h
    scores = _sigmoid(_dot3(wr_ref[...], h, _NT))
    choice = scores + bias_ref[...]
    per = N_EXPERTS // N_GROUPS
    ch3 = choice.reshape(N_GROUPS, per, tm)
    within = lax.broadcasted_iota(jnp.int32, ch3.shape, 1)
    m1, i1 = _first_argmax(ch3, within, 1)
    m2 = jnp.max(jnp.where(within == i1, NEG_INF, ch3), axis=1, keepdims=True)
    grp = (m1 + m2).reshape(N_GROUPS, tm)
    gidx = lax.broadcasted_iota(jnp.int32, grp.shape, 0)
    gsel = jnp.zeros(grp.shape, F32)
    for _ in range(TOPK_GROUPS):
        _, gi = _first_argmax(grp, gidx, 0)
        pick = gidx == gi
        gsel = jnp.where(pick, 1.0, gsel)
        grp = jnp.where(pick, NEG_INF, grp)
    gsel3 = jnp.broadcast_to(gsel.reshape(N_GROUPS, 1, tm), ch3.shape)
    cur = jnp.where(gsel3 > 0.5, ch3, NEG_INF).reshape(N_EXPERTS, tm)
    eidx = lax.broadcasted_iota(jnp.int32, cur.shape, 0)
    sel = jnp.zeros(cur.shape, F32)
    picks = []
    for _ in range(TOP_K):
        _, ei = _first_argmax(cur, eidx, 0)
        pick = eidx == ei
        sel = jnp.where(pick, 1.0, sel)
        cur = jnp.where(pick, NEG_INF, cur)
        picks.append(ei)
    w = jnp.where(sel > 0.5, scores, 0.0)
    gt_ref[...] = w / jnp.sum(w, axis=0, keepdims=True) * ROUTED_SCALE
    idx_ref[...] = jnp.concatenate(picks, axis=0)


def _router_call(x2, g, sc, sh, w_router_t, bias_col, seq, tm=256):
    T, D = x2.shape
    bmap = lambda i: ((i * tm) // seq, 0, 0)
    return pl.pallas_call(
        functools.partial(_router_kernel, tm=tm),
        grid=(T // tm,),
        in_specs=[pl.BlockSpec((tm, D), lambda i: (i, 0)),
                  pl.BlockSpec((1, D), lambda i: (0, 0)),
                  pl.BlockSpec((1, 1, D), bmap),
                  pl.BlockSpec((1, 1, D), bmap),
                  pl.BlockSpec((N_EXPERTS, D), lambda i: (0, 0)),
                  pl.BlockSpec((N_EXPERTS, 1), lambda i: (0, 0))],
        out_specs=[pl.BlockSpec((tm, D), lambda i: (i, 0)),
                   pl.BlockSpec((N_EXPERTS, tm), lambda i: (0, i)),
                   pl.BlockSpec((TOP_K, tm), lambda i: (0, i))],
        out_shape=[jax.ShapeDtypeStruct((T, D), F32),
                   jax.ShapeDtypeStruct((N_EXPERTS, T), F32),
                   jax.ShapeDtypeStruct((TOP_K, T), jnp.int32)],
        compiler_params=_params(("arbitrary",)),
        name="router",
    )(x2, g, sc, sh, w_router_t, bias_col)


def _row_copy(src, src_row, dst, dst_row, sem):
    return pltpu.make_async_copy(src.at[pl.ds(src_row, 1), :], dst.at[pl.ds(dst_row, 1), :], sem)


def _dispatch_kernel(dest_ref, h_ref, xs_ref, sem, *, tm):
    def issue(t, carry):
        for j in range(TOP_K):
            _row_copy(h_ref, t, xs_ref, dest_ref[t * TOP_K + j], sem).start()
        return carry

    lax.fori_loop(0, tm, issue, 0)
    for _ in range(TOP_K):
        pltpu.make_async_copy(h_ref, xs_ref.at[pl.ds(0, tm), :], sem).wait()


def _dispatch_call(dest_flat, h2, tm=256):
    T, D = h2.shape
    return pl.pallas_call(
        functools.partial(_dispatch_kernel, tm=tm),
        grid=(T // tm,),
        in_specs=[pl.BlockSpec((tm * TOP_K,), lambda i: (i,), memory_space=pltpu.SMEM),
                  pl.BlockSpec((tm, D), lambda i: (i, 0))],
        out_specs=pl.BlockSpec(memory_space=pl.ANY),
        out_shape=jax.ShapeDtypeStruct((T * TOP_K, D), F32),
        scratch_shapes=[pltpu.SemaphoreType.DMA(())],
        compiler_params=_params(("arbitrary",)),
        name="dispatch",
    )(dest_flat, h2)


def _expert_kernel(tile_ref, exp_ref, lo_ref, hi_ref, n_ref, x_ref, wg_ref, wu_ref, wd_ref, o_ref, *, te):
    s = pl.program_id(0)

    @pl.when(s < n_ref[0])
    def _():
        tile = tile_ref[s]
        x = x_ref[...].astype(BF16)
        hid = _silu(_dot(x, wg_ref[0, 0].astype(BF16))) * _dot(x, wu_ref[0, 0].astype(BF16))
        y = _bdot(hid, wd_ref[0, 0])
        rows = tile * te + lax.broadcasted_iota(jnp.int32, (te, 1), 0)
        mine = (rows >= lo_ref[s]) & (rows < hi_ref[s])
        first_visit = (s == 0) | (tile_ref[jnp.maximum(s - 1, 0)] != tile)

        @pl.when(first_visit)
        def _():
            o_ref[...] = jnp.where(mine, y, 0.0)

        @pl.when(jnp.logical_not(first_visit))
        def _():
            o_ref[...] = jnp.where(mine, y, o_ref[...])


def _expert_call(meta, x_sorted, w_gate, w_up, w_down, layer):
    A, D = x_sorted.shape
    _, E, _, DE = w_gate.shape
    te = EXPERT_TILE
    n_steps = A // te + E - 1
    grid_spec = pltpu.PrefetchScalarGridSpec(
        num_scalar_prefetch=5,
        grid=(n_steps,),
        in_specs=[pl.BlockSpec((te, D), lambda s, tile, exp, lo, hi, n: (tile[s], 0)),
                  pl.BlockSpec((1, 1, D, DE), lambda s, tile, exp, lo, hi, n: (layer, exp[s], 0, 0)),
                  pl.BlockSpec((1, 1, D, DE), lambda s, tile, exp, lo, hi, n: (layer, exp[s], 0, 0)),
                  pl.BlockSpec((1, 1, DE, D), lambda s, tile, exp, lo, hi, n: (layer, exp[s], 0, 0))],
        out_specs=pl.BlockSpec((te, D), lambda s, tile, exp, lo, hi, n: (tile[s], 0)),
    )
    return pl.pallas_call(
        functools.partial(_expert_kernel, te=te),
        grid_spec=grid_spec,
        out_shape=jax.ShapeDtypeStruct((A, D), F32),
        compiler_params=_params(("arbitrary",)),
        name="experts",
    )(*meta, x_sorted, w_gate, w_up, w_down)


def _expert_steps(counts, n_rows):
    te = EXPERT_TILE
    E = counts.shape[0]
    n_steps = n_rows // te + E - 1
    ends = jnp.cumsum(counts)
    starts = ends - counts
    first_tile = starts // te
    last_tile = jnp.maximum(ends - 1, 0) // te
    n_tiles = jnp.where(counts > 0, last_tile - first_tile + 1, 0)
    step_end = jnp.cumsum(n_tiles)
    step_start = step_end - n_tiles
    total = step_end[-1]
    s = jnp.arange(n_steps, dtype=jnp.int32)
    s_eff = jnp.minimum(s, total - 1)
    exp = jnp.searchsorted(step_end, s_eff, side="right").astype(jnp.int32)
    tile = (first_tile[exp] + (s_eff - step_start[exp])).astype(jnp.int32)
    return (tile, exp, starts[exp].astype(jnp.int32), ends[exp].astype(jnp.int32),
            total.reshape(1).astype(jnp.int32))


def _combine_kernel(dest_ref, x_ref, h_ref, w_ref, ys_ref, wsg_ref, wsu_ref, wsd_ref, g2_ref, fg_ref,
                    o_ref, buf, sem, *, tm, final_norm):
    def issue(t, carry):
        for j in range(TOP_K):
            _row_copy(ys_ref, dest_ref[t * TOP_K + j], buf.at[j], t, sem).start()
        return carry

    lax.fori_loop(0, tm, issue, 0)
    h = h_ref[...].astype(BF16)
    hid = _silu(_dot(h, wsg_ref[...])) * _dot(h, wsu_ref[...])
    acc = _bdot(hid, wsd_ref[...])
    for j in range(TOP_K):
        pltpu.make_async_copy(ys_ref.at[pl.ds(0, tm), :], buf.at[j], sem).wait()
    w = w_ref[...]
    for j in range(TOP_K):
        acc = acc + buf[j] * w[:, j:j + 1]
    out = x_ref[...] + g2_ref[0] * acc
    if final_norm:
        ms = jnp.mean(out * out, axis=-1, keepdims=True)
        out = out * lax.rsqrt(ms + RMS_EPS) * fg_ref[...]
    o_ref[...] = out


def _combine_call(dest_flat, x2, h2, w_tok, y_sorted, w_sh_gate, w_sh_up, w_sh_down, g2, final_g,
                  seq, final_norm, tm=256):
    T, D = x2.shape
    const = lambda shape: pl.BlockSpec(shape, lambda i: (0,) * len(shape),
                                       pipeline_mode=pl.Buffered(1))
    return pl.pallas_call(
        functools.partial(_combine_kernel, tm=tm, final_norm=final_norm),
        grid=(T // tm,),
        in_specs=[pl.BlockSpec((tm * TOP_K,), lambda i: (i,), memory_space=pltpu.SMEM),
                  pl.BlockSpec((tm, D), lambda i: (i, 0)),
                  pl.BlockSpec((tm, D), lambda i: (i, 0)),
                  pl.BlockSpec((tm, TOP_K), lambda i: (i, 0)),
                  pl.BlockSpec(memory_space=pl.ANY),
                  const(w_sh_gate.shape),
                  const(w_sh_up.shape),
                  const(w_sh_down.shape),
                  pl.BlockSpec((1, 1, D), lambda i: ((i * tm) // seq, 0, 0)),
                  const(final_g.shape)],
        out_specs=pl.BlockSpec((tm, D), lambda i: (i, 0)),
        out_shape=jax.ShapeDtypeStruct((T, D), F32),
        scratch_shapes=[pltpu.VMEM((TOP_K, tm, D), F32), pltpu.SemaphoreType.DMA(())],
        compiler_params=_params(("arbitrary",)),
        name="combine",
    )(dest_flat, x2, h2, w_tok, y_sorted, w_sh_gate, w_sh_up, w_sh_down, g2, final_g)


def _pad_cols(w, n):
    return jnp.pad(w, ((0, 0), (0, n - w.shape[1])))


def _pad_rows(w, n):
    return jnp.pad(w, ((0, n - w.shape[0]), (0, 0)))


def _pack_cols(parts):
    return jnp.concatenate([_pad_cols(a, n) for a, n in parts], axis=1)


def kernel(x, c, ada_w, ada_b, norm1_g, norm2_g, w_in, mu_shift, w_decay_up, decay_base, w_a_up, a_base, w_g_up, k_k, k_a, r_k, lnx_g, lnx_b, w_v_down, mu_v, w_v_up, v_base, pool_w, pool_scale, w_br_rwkv, w_br_pool, w_out, w_router, router_bias, w_exp_gate, w_exp_up, w_exp_down, w_sh_gate, w_sh_up, w_sh_down, final_g):
    B, S, D = x.shape
    T = B * S
    L = ada_w.shape[0]
    DR = w_decay_up.shape[2]
    n_dec, n_a, n_g, n_v = w_decay_up.shape[1], w_a_up.shape[1], w_g_up.shape[1], w_v_down.shape[2]
    o_k, o_v, o_wd = DR, 2 * DR, 3 * DR
    o_ad = o_wd + n_dec
    o_gd = o_ad + n_a
    n_shift = o_gd + n_g
    o_gate = n_shift + DR

    c_pad = jnp.pad(c, ((0, 8 - B), (0, 0)))
    mod = _ada_call(c_pad, ada_w, ada_b)[:, :B]
    mod = mod.reshape(L, B, 6, 1, D)

    x2 = x.reshape(T, D)
    proj_first = None
    for l in range(L):
        sh1, sc1, g1, sh2, sc2, g2 = (mod[l, :, q] for q in range(6))
        has_vres = l > 0
        wl = w_in[l]
        zero_cols = jnp.zeros((D, 0), F32)
        vd_w = w_v_down[l - 1] if has_vres else zero_cols
        w_packed = _pack_cols([
            (wl[:, :o_wd], 3 * DR),
            (wl[:, o_wd:o_ad], PAIR), (wl[:, o_ad:o_gd], PAIR), (wl[:, o_gd:n_shift], 2 * PAIR),
            (vd_w, PAIR), (zero_cols, PROJ_TN - 5 * PAIR),
            (wl[:, o_gate:], 2 * D),
            (wl[:, n_shift:o_gate], DR)]).astype(BF16)
        ml = mu_shift[l][None, :]
        zero_mu = jnp.zeros((1, 0), F32)
        vd_mu = mu_v[l - 1][None, :] if has_vres else zero_mu
        mu_packed = _pack_cols([
            (ml[:, :o_wd], 3 * DR),
            (ml[:, o_wd:o_ad], PAIR), (ml[:, o_ad:o_gd], PAIR), (ml[:, o_gd:n_shift], 2 * PAIR),
            (vd_mu, PAIR), (zero_mu, PROJ_TN - 5 * PAIR), (zero_mu, 2 * D + DR)])
        proj = _proj_call(x2, norm1_g[l][None, :], sc1, sh1, w_packed, mu_packed, S)
        if l == 0:
            proj_first = proj

        lora_w = {"decay": _pad_rows(w_decay_up[l], PAIR), "a": _pad_rows(w_a_up[l], PAIR),
                  "g": w_g_up[l]}
        zeros_dr = jnp.zeros((DR,), F32)
        if has_vres:
            lora_w["v"] = _pad_rows(w_v_up[l - 1], PAIR)
        vecs = jnp.stack([decay_base[l], a_base[l], k_k[l], k_a[l], r_k[l].reshape(DR), lnx_g[l],
                          lnx_b[l], v_base[l - 1] if has_vres else zeros_dr])
        y_rwkv = _rwkv_call(proj.reshape(B, S, -1), proj_first.reshape(B, S, -1), lora_w, vecs,
                            has_vres).reshape(T, DR)

        x2 = _merge_call(x2, y_rwkv, proj, pool_w[l].astype(BF16), pool_scale[l][None, :],
                         w_br_rwkv[l].astype(BF16), w_br_pool[l].astype(BF16),
                         w_out[l].astype(BF16), g1, S)

        h2, gate_t, idx_t = _router_call(x2, norm2_g[l][None, :], sc2, sh2, w_router[l].T,
                                         router_bias[l][:, None], S)
        experts = jnp.arange(N_EXPERTS, dtype=jnp.int32)[None, :, None]
        sel = jnp.any(idx_t[:, None, :] == experts, axis=0).astype(jnp.int32)
        incl = jnp.cumsum(sel, axis=1)
        counts = incl[:, -1]
        start = jnp.cumsum(counts) - counts
        slot = start[:, None] + incl - sel
        dest = jnp.take_along_axis(slot, idx_t, axis=0)
        dest_flat = dest.T.reshape(T * TOP_K).astype(jnp.int32)
        w_tok = jnp.take_along_axis(gate_t, idx_t, axis=0).T

        x_sorted = _dispatch_call(dest_flat, h2)
        meta = _expert_steps(counts, T * TOP_K)
        y_sorted = _expert_call(meta, x_sorted, w_exp_gate, w_exp_up, w_exp_down, l)
        x2 = _combine_call(dest_flat, x2, h2, w_tok, y_sorted, w_sh_gate[l].astype(BF16),
                           w_sh_up[l].astype(BF16), w_sh_down[l].astype(BF16), g2,
                           final_g[None, :], S, final_norm=(l == L - 1))
    return x2.reshape(B, S, D)
```

```python
import functools
import math

import jax
import jax.numpy as jnp
from jax import lax
from jax.experimental import pallas as pl
from jax.experimental.pallas import tpu as pltpu

F32 = jnp.float32
BF16 = jnp.bfloat16

HEAD = 64
PAIR = 2 * HEAD
CHUNK = 64
POOL_WINDOWS = (2, 4, 8, 16)
POOL_HIST = 16
N_EXPERTS = 64
TOP_K = 8
N_GROUPS = 8
TOPK_GROUPS = 4
ROUTED_SCALE = 2.5
GN_EPS = 64e-5
RMS_EPS = 1e-6
NEG_INF = float("-inf")

PROJ_TN = 1024
N_SHIFT_TILES = 4
EXPERT_TILE = 256
VMEM_LIMIT = 56 * 1024 * 1024

_NN = (((1,), (0,)), ((), ()))
_NT = (((1,), (1,)), ((), ()))


def _dot(a, b, dims=_NN):
    return lax.dot_general(a, b, dims, preferred_element_type=F32)


def _bdot(a, b, dims=_NN):
    return _dot(a.astype(BF16), b.astype(BF16), dims)


def _split(a):
    hi = a.astype(BF16)
    lo = (a - hi.astype(F32)).astype(BF16)
    return hi, lo


def _dot3s(ap, bp, dims=_NN):
    (ah, al), (bh, bl) = ap, bp
    return _dot(ah, bh, dims) + (_dot(ah, bl, dims) + _dot(al, bh, dims))


def _dot3(a, b, dims=_NN):
    return _dot3s(_split(a), _split(b), dims)


def _dot2c(a, c_bf16, dims=_NN):
    ah, al = _split(a)
    return _dot(ah, c_bf16, dims) + _dot(al, c_bf16, dims)


def _sigmoid(x):
    return 1.0 / (1.0 + jnp.exp(-x))


def _silu(x):
    return x * _sigmoid(x)


def _rms_mod(x, g, sc, sh):
    ms = jnp.mean(x * x, axis=-1, keepdims=True)
    return (x * lax.rsqrt(ms + RMS_EPS) * g) * (1.0 + sc) + sh


HI16 = 0xFFFF0000


def _pack_halves(x):
    n = x.shape[1] // 2
    bits = lax.bitcast_convert_type(x.astype(BF16).astype(F32), jnp.uint32)
    return (bits[:, n:] & jnp.uint32(HI16)) | (bits[:, :n] >> 16)


def _unpack_halves(p):
    lo = lax.bitcast_convert_type(p << 16, F32)
    hi = lax.bitcast_convert_type(p & jnp.uint32(HI16), F32)
    return lo, hi


def _params(sem):
    return pltpu.CompilerParams(dimension_semantics=sem, vmem_limit_bytes=VMEM_LIMIT)


def _ada_kernel(c_ref, w_ref, b_ref, o_ref):
    cond = _silu(c_ref[...])
    o_ref[0] = _dot3(cond, w_ref[0]) + b_ref[0]


def _ada_call(c_pad, ada_w, ada_b):
    L, D, N6 = ada_w.shape
    tn = 1024
    return pl.pallas_call(
        _ada_kernel,
        grid=(L, N6 // tn),
        in_specs=[pl.BlockSpec(c_pad.shape, lambda l, j: (0, 0)),
                  pl.BlockSpec((1, D, tn), lambda l, j: (l, 0, j)),
                  pl.BlockSpec((1, 1, tn), lambda l, j: (l, 0, j))],
        out_specs=pl.BlockSpec((1, c_pad.shape[0], tn), lambda l, j: (l, 0, j)),
        out_shape=jax.ShapeDtypeStruct((L, c_pad.shape[0], N6), F32),
        compiler_params=_params(("arbitrary", "arbitrary")),
        name="adaln",
    )(c_pad, ada_w, ada_b.reshape(L, 1, N6))


def _proj_kernel(x_ref, g_ref, sc_ref, sh_ref, w_ref, mu_ref, o_ref, h_scr, prev_scr, *, tm, seq):
    i = pl.program_id(0)
    j = pl.program_id(1)

    @pl.when(j == 0)
    def _():
        h_scr[...] = _rms_mod(x_ref[...], g_ref[...], sc_ref[0], sh_ref[0]).astype(BF16)

    p = _dot(h_scr[...], w_ref[...])

    @pl.when(j < N_SHIFT_TILES)
    def _():
        seq_start = (i * tm) % seq == 0
        carry = jnp.where(seq_start, 0.0, prev_scr[j])
        row = lax.broadcasted_iota(jnp.int32, p.shape, 0)
        prev = jnp.where(row == 0, carry, pltpu.roll(p, 1, 0))
        o_ref[...] = (p + (prev - p) * mu_ref[...]).astype(o_ref.dtype)
        prev_scr[j] = p[tm - 1:tm, :]

    @pl.when(j >= N_SHIFT_TILES)
    def _():
        o_ref[...] = p.astype(o_ref.dtype)


def _proj_call(x2, g, sc, sh, w_packed, mu_packed, seq, tm=1024):
    T, D = x2.shape
    NP = w_packed.shape[1]
    nj = NP // PROJ_TN
    bmap = lambda i, j: ((i * tm) // seq, 0, 0)
    return pl.pallas_call(
        functools.partial(_proj_kernel, tm=tm, seq=seq),
        grid=(T // tm, nj),
        in_specs=[pl.BlockSpec((tm, D), lambda i, j: (i, 0)),
                  pl.BlockSpec((1, D), lambda i, j: (0, 0)),
                  pl.BlockSpec((1, 1, D), bmap),
                  pl.BlockSpec((1, 1, D), bmap),
                  pl.BlockSpec((D, PROJ_TN), lambda i, j: (0, j)),
                  pl.BlockSpec((1, PROJ_TN), lambda i, j: (0, j))],
        out_specs=pl.BlockSpec((tm, PROJ_TN), lambda i, j: (i, j)),
        out_shape=jax.ShapeDtypeStruct((T, NP), BF16),
        scratch_shapes=[pltpu.VMEM((tm, D), BF16),
                        pltpu.VMEM((N_SHIFT_TILES, 1, PROJ_TN), F32)],
        compiler_params=_params(("arbitrary", "arbitrary")),
        name="proj",
    )(x2, g, sc, sh, w_packed, mu_packed)


RWKV_PASSES = {"lora": 1, "gram": 1, "inv": 1, "apply": 1}
RWKV_BATCH = 4
RWKV_PAIRS = 2
RWKV_ROWS = 512


def _splitp(a, passes):
    hi = a.astype(BF16)
    if passes == 1:
        return hi, None
    return hi, (a - hi.astype(F32)).astype(BF16)


def _mm(ap, bp, passes, dims=_NN):
    (ah, al), (bh, bl) = ap, bp
    out = _dot(ah, bh, dims)
    if passes == 1:
        return out
    lo = None
    if bl is not None:
        lo = _dot(ah, bl, dims)
    if al is not None:
        t = _dot(al, bh, dims)
        lo = t if lo is None else lo + t
    return out if lo is None else out + lo


def _stack(z, lane_lo):
    return jnp.concatenate([jnp.where(lane_lo, z, 0.0), jnp.where(lane_lo, 0.0, z)], axis=0)


def _rwkv_kernel(*refs, has_vres, nb, npl, rg):
    if has_vres:
        (r_ref, k_ref, v_ref, wd_ref, ad_ref, gd_ref, vd_ref, vf_ref,
         wdu_ref, wau_ref, wgu_ref, wvu_ref, vec_ref, y_ref, st_ref) = refs
    else:
        (r_ref, k_ref, v_ref, wd_ref, ad_ref, gd_ref,
         wdu_ref, wau_ref, wgu_ref, vec_ref, y_ref, st_ref) = refs
    p_lora, p_gram, p_inv, p_app = (RWKV_PASSES[s] for s in ("lora", "gram", "inv", "apply"))
    p_in = max(p_gram, p_app)
    C = CHUNK
    C2 = 2 * C

    @pl.when(pl.program_id(2) == 0)
    def _():
        st_ref[...] = jnp.zeros_like(st_ref)

    lane_lo = lax.broadcasted_iota(jnp.int32, (C, PAIR), 1) < HEAD
    ri = lax.broadcasted_iota(jnp.int32, (C2, C2), 0)
    ci = lax.broadcasted_iota(jnp.int32, (C2, C2), 1)
    strict = ri > ci
    incl = ri >= ci
    eye = jnp.where(ri == ci, 1.0, 0.0)
    wl = npl * PAIR
    ones_head = jnp.where(lax.broadcasted_iota(jnp.int32, (wl, wl), 0) // HEAD
                          == lax.broadcasted_iota(jnp.int32, (wl, wl), 1) // HEAD,
                          1.0, 0.0).astype(BF16)
    rb = lax.broadcasted_iota(jnp.int32, (nb * C, nb * C), 0)
    cb = lax.broadcasted_iota(jnp.int32, (nb * C, nb * C), 1)
    same_seq = (rb // C) == (cb // C)
    tri_blk = jnp.where(same_seq & (rb >= cb), 1.0, 0.0).astype(BF16)
    ones_blk = jnp.where(same_seq, 1.0, 0.0).astype(BF16)

    vec = vec_ref[...]
    decay_base, a_base, k_k, k_a = vec[0:1], vec[1:2], vec[2:3], vec[3:4]
    r_k, lnx_g, lnx_b, v_base = vec[4:5], vec[5:6], vec[6:7], vec[7:8]
    wdu = _splitp(wdu_ref[...], p_lora)
    wau = _splitp(wau_ref[...], p_lora)
    wgu = _splitp(wgu_ref[...], p_lora)
    if has_vres:
        wvu = _splitp(wvu_ref[...], p_lora)

    chains = [(b, q) for b in range(nb) for q in range(npl)]
    seqs = range(len(chains))

    def chunk_body(c, carry):
        rows = pl.ds(pl.multiple_of(c * C, C), C)
        cat = lambda ref: jnp.concatenate([ref[b, rows, :].astype(F32) for b in range(nb)], axis=0)

        def part(x, i):
            b, q = chains[i]
            return x[b * C:(b + 1) * C, q * PAIR:(q + 1) * PAIR]

        r, k, v = cat(r_ref), cat(k_ref), cat(v_ref)
        dl = decay_base + _mm(_splitp(jnp.tanh(cat(wd_ref)), p_lora), wdu, p_lora)
        z = -dl
        softplus = jnp.maximum(z, 0.0) + jnp.log(1.0 + jnp.exp(-jnp.abs(z)))
        lw = -jnp.exp(-softplus - 0.5)
        a = _sigmoid(a_base + _mm(_splitp(cat(ad_ref), p_lora), wau, p_lora))
        g = _mm(_splitp(_sigmoid(cat(gd_ref)), p_lora), wgu, p_lora)
        if has_vres:
            mix = _sigmoid(v_base + _mm(_splitp(cat(vd_ref), p_lora), wvu, p_lora))
            v = v + (cat(vf_ref) - v) * mix
        kkr = k * k_k
        kk = kkr / jnp.maximum(jnp.sqrt(_dot2c(kkr * kkr, ones_head)), 1e-12)
        k2 = k * (1.0 + (a - 1.0) * k_a)
        avec = -kk
        bvec = kk * a

        lwh, lwl = _split(lw)
        cl = _dot(tri_blk, lwh) + _dot(tri_blk, lwl)
        tot = _dot(ones_blk, lwh) + _dot(ones_blk, lwl)
        w_inv = jnp.exp(-cl)
        w_tail = jnp.exp(tot - cl)
        w_tot = jnp.exp(tot)
        a_t = avec * jnp.exp(cl - lw)
        r_t = r * jnp.exp(cl)
        b_t = bvec * w_inv
        k_t = k2 * w_inv
        b_h = bvec * w_tail
        k_h = k2 * w_tail

        pair2 = lambda x, y, b: jnp.concatenate([_stack(part(x, b), lane_lo),
                                                 _stack(part(y, b), lane_lo)], axis=0)
        lhs = [_splitp(pair2(a_t, r_t, b), p_in) for b in seqs]
        rhs = [_splitp(pair2(b_t, k_t, b), p_gram) for b in seqs]
        gram = [_mm(lhs[b], rhs[b], p_gram, _NT) for b in seqs]
        a_ab = [jnp.where(strict, gm[:C2, :C2], 0.0) for gm in gram]
        a_ak = [jnp.where(strict, gm[:C2, C2:], 0.0) for gm in gram]
        a_r = [jnp.concatenate([jnp.where(incl, gm[C2:, :C2], 0.0),
                                jnp.where(incl, gm[C2:, C2:], 0.0)], axis=1) for gm in gram]

        pw = a_ab
        inv = [eye + m for m in a_ab]
        for _ in range(int(math.log2(C)) - 1):
            pws = [_splitp(m, p_inv) for m in pw]
            pw = [_mm(m, m, p_inv) for m in pws]
            inv = [inv[b] + _mm(_splitp(inv[b], p_inv), _splitp(pw[b], p_inv), p_inv) for b in seqs]

        s_prev = [st_ref[b] for b in seqs]
        sx = [_mm(lhs[b], _splitp(s_prev[b], p_app), p_app, _NT) for b in seqs]
        v_st = [_stack(part(v, b), lane_lo) for b in seqs]
        x_st = [sx[b][:C2] + _mm(_splitp(a_ak[b], p_app), _splitp(v_st[b], p_app), p_app) for b in seqs]
        u_st = [_mm(_splitp(inv[b], p_app), _splitp(x_st[b], p_app), p_app) for b in seqs]
        uv = [jnp.concatenate([u_st[b], v_st[b]], axis=0) for b in seqs]
        y_st = [sx[b][C2:] + _mm(_splitp(a_r[b], p_app), _splitp(uv[b], p_app), p_app) for b in seqs]
        for b in seqs:
            bk = _splitp(pair2(b_h, k_h, b), p_app)
            st_ref[b] = (s_prev[b] * part(w_tot, b)[0:1]
                         + _mm(_splitp(uv[b].T, p_app), bk, p_app))
        y_ch = [ys[:C] + ys[C:] for ys in y_st]
        y = jnp.concatenate([jnp.concatenate(y_ch[b * npl:(b + 1) * npl], axis=1)
                             for b in range(nb)], axis=0)

        inv_n = 1.0 / HEAD
        mu = _dot2c(y, ones_head) * inv_n
        yc = y - mu
        var = _dot2c(yc * yc, ones_head) * inv_n
        yn = yc * lax.rsqrt(var + GN_EPS) * lnx_g + lnx_b
        bonus = _dot2c(r * k2 * r_k, ones_head) * v
        out = (yn + bonus) * g
        for b in range(nb):
            y_ref[b, rows, :] = out[b * C:(b + 1) * C]
        return carry

    lax.fori_loop(0, rg // C, chunk_body, 0)


def _rwkv_call(proj, proj_first, lora_w, vecs, has_vres):
    B, S, _ = proj.shape
    nb, rg, npl = min(RWKV_BATCH, B), min(RWKV_ROWS, S), RWKV_PAIRS
    dr = vecs.shape[1]
    wl = npl * PAIR
    n_blk = dr // wl
    col = lambda off: (lambda bb, p, t: (bb, t, off * n_blk + p))
    fixed = lambda blk: (lambda bb, p, t: (bb, t, blk))
    blk = lambda w, imap: pl.BlockSpec((nb, rg, w), imap)
    lora0 = 3 * dr // PAIR
    in_specs = [blk(wl, col(0)),
                blk(wl, col(1)),
                blk(wl, col(2)),
                blk(PAIR, fixed(lora0)),
                blk(PAIR, fixed(lora0 + 1)),
                blk(2 * PAIR, fixed(lora0 // 2 + 1))]
    args = [proj] * 6
    if has_vres:
        in_specs += [blk(PAIR, fixed(lora0 + 4)),
                     blk(wl, col(2))]
        args += [proj, proj_first]
    wspec = lambda rows: pl.BlockSpec((rows, wl), lambda bb, p, t: (0, p))
    in_specs += [wspec(PAIR), wspec(PAIR), wspec(2 * PAIR)]
    args += [lora_w["decay"], lora_w["a"], lora_w["g"]]
    if has_vres:
        in_specs += [wspec(PAIR)]
        args += [lora_w["v"]]
    in_specs += [pl.BlockSpec((8, wl), lambda bb, p, t: (0, p))]
    args += [vecs]
    return pl.pallas_call(
        functools.partial(_rwkv_kernel, has_vres=has_vres, nb=nb, npl=npl, rg=rg),
        grid=(B // nb, n_blk, S // rg),
        in_specs=in_specs,
        out_specs=pl.BlockSpec((nb, rg, wl), lambda bb, p, t: (bb, t, p)),
        out_shape=jax.ShapeDtypeStruct((B, S, dr), F32),
        scratch_shapes=[pltpu.VMEM((nb * npl, PAIR, PAIR), F32)],
        compiler_params=_params(("arbitrary", "arbitrary", "arbitrary")),
        name="rwkv7",
    )(*args)


def _merge_kernel(x_ref, y_ref, pool_ref, gr_ref, gp_ref, pw_ref, ps_ref, wbr_ref, wbp_ref, wo_ref,
                  g1_ref, o_ref, hist_ref, *, tm, seq):
    i = pl.program_id(0)
    seq_off = (i * tm) % seq
    p = pool_ref[...].astype(F32)
    hist =jnp.where(seq_off == 0, 0.0, hist_ref[...])
    ext = jnp.concatenate([hist, p], axis=0)
    pos = (seq_off + 1 + lax.broadcasted_iota(jnp.int32, (tm, 1), 0)).astype(F32)
    gw = p.shape[1] // len(POOL_WINDOWS)
    zs = []
    for gi, win in enumerate(POOL_WINDOWS):
        e = ext[:, gi * gw:(gi + 1) * gw]
        s = e
        shift = 1
        while shift < win:
            n = s.shape[0]
            s = s[shift:, :] + s[:n - shift, :]
            shift *= 2
        lo = POOL_HIST + 1 - win
        wsum = s[lo:lo + tm, :]
        u = wsum / jnp.minimum(pos, float(win)) - p[:, gi * gw:(gi + 1) * gw]
        zs.append(_bdot(u, pw_ref[gi]))
    y_pool = jnp.concatenate(zs, axis=1) * ps_ref[...]
    hist_ref[...] = p[tm - POOL_HIST:, :]

    br = _bdot(y_ref[...], wbr_ref[...])
    bp = _bdot(y_pool, wbp_ref[...])
    merged = _sigmoid(gr_ref[...].astype(F32)) * br + _sigmoid(gp_ref[...].astype(F32)) * bp
    o_ref[...] = x_ref[...] + g1_ref[0] * _bdot(merged, wo_ref[...])


def _merge_call(x2, y_rwkv, proj, pool_w, pool_scale, w_br_rwkv, w_br_pool, w_out, g1, seq, tm=256):
    T, D = x2.shape
    DR = y_rwkv.shape[1]
    const = lambda shape: pl.BlockSpec(shape, lambda i: (0,) * len(shape),
                                       pipeline_mode=pl.Buffered(1))
    return pl.pallas_call(
        functools.partial(_merge_kernel, tm=tm, seq=seq),
        grid=(T // tm,),
        in_specs=[pl.BlockSpec((tm, D), lambda i: (i, 0)),
                  pl.BlockSpec((tm, DR), lambda i: (i, 0)),
                  pl.BlockSpec((tm, DR), lambda i: (i, 8)),
                  pl.BlockSpec((tm, D), lambda i: (i, 2)),
                  pl.BlockSpec((tm, D), lambda i: (i, 3)),
                  const(pool_w.shape),
                  const(pool_scale.shape),
                  const(w_br_rwkv.shape),
                  const(w_br_pool.shape),
                  const(w_out.shape),
                  pl.BlockSpec((1, 1, D), lambda i: ((i * tm) // seq, 0, 0))],
        out_specs=pl.BlockSpec((tm, D), lambda i: (i, 0)),
        out_shape=jax.ShapeDtypeStruct((T, D), F32),
        scratch_shapes=[pltpu.VMEM((POOL_HIST, DR), F32)],
        compiler_params=_params(("arbitrary",)),
        name="merge",
    )(x2, y_rwkv, proj, proj, proj, pool_w, pool_scale, w_br_rwkv, w_br_pool, w_out, g1)


def _first_argmax(cur, idx, axis):
    m = jnp.max(cur, axis=axis, keepdims=True)
    first = jnp.min(jnp.where(cur == m, idx, 1 << 20), axis=axis, keepdims=True)
    return m, first


def _router_kernel(x_ref, g_ref, sc_ref, sh_ref, wr_ref, bias_ref, h_ref, gt_ref, idx_ref, *, tm):
    h = _rms_mod(x_ref[...], g_ref[...], sc_ref[0], sh_ref[0])
    h_ref[...] = _pack_halves(h)
    scores =_sigmoid(_dot3(wr_ref[...], h, _NT))
    choice = scores + bias_ref[...]
    per = N_EXPERTS // N_GROUPS
    ch3 = choice.reshape(N_GROUPS, per, tm)
    within = lax.broadcasted_iota(jnp.int32, ch3.shape, 1)
    m1, i1 = _first_argmax(ch3, within, 1)
    m2 = jnp.max(jnp.where(within == i1, NEG_INF, ch3), axis=1, keepdims=True)
    grp = (m1 + m2).reshape(N_GROUPS, tm)
    gidx = lax.broadcasted_iota(jnp.int32, grp.shape, 0)
    gsel = jnp.zeros(grp.shape, F32)
    for _ in range(TOPK_GROUPS):
        _, gi = _first_argmax(grp, gidx, 0)
        pick = gidx == gi
        gsel = jnp.where(pick, 1.0, gsel)
        grp = jnp.where(pick, NEG_INF, grp)
    gsel3 = jnp.broadcast_to(gsel.reshape(N_GROUPS, 1, tm), ch3.shape)
    cur = jnp.where(gsel3 > 0.5, ch3, NEG_INF).reshape(N_EXPERTS, tm)
    eidx = lax.broadcasted_iota(jnp.int32, cur.shape, 0)
    sel = jnp.zeros(cur.shape, F32)
    picks = []
    for _ in range(TOP_K):
        _, ei = _first_argmax(cur, eidx, 0)
        pick = eidx == ei
        sel = jnp.where(pick, 1.0, sel)
        cur = jnp.where(pick, NEG_INF, cur)
        picks.append(ei)
    w = jnp.where(sel > 0.5, scores, 0.0)
    gt_ref[...] = w / jnp.sum(w, axis=0, keepdims=True) * ROUTED_SCALE
    idx_ref[...] = jnp.concatenate(picks, axis=0)


def _router_call(x2, g, sc, sh, w_router_t, bias_col, seq, tm=256):
    T, D = x2.shape
    bmap = lambda i: ((i * tm) // seq, 0, 0)
    return pl.pallas_call(
        functools.partial(_router_kernel, tm=tm),
        grid=(T // tm,),
        in_specs=[pl.BlockSpec((tm, D), lambda i: (i, 0)),
                  pl.BlockSpec((1, D), lambda i: (0, 0)),
                  pl.BlockSpec((1, 1, D), bmap),
                  pl.BlockSpec((1, 1, D), bmap),
                  pl.BlockSpec((N_EXPERTS, D), lambda i: (0, 0)),
                  pl.BlockSpec((N_EXPERTS, 1), lambda i: (0, 0))],
        out_specs=[pl.BlockSpec((tm, D // 2), lambda i: (i, 0)),
                   pl.BlockSpec((N_EXPERTS, tm), lambda i: (0, i)),
                   pl.BlockSpec((TOP_K, tm), lambda i: (0, i))],
        out_shape=[jax.ShapeDtypeStruct((T, D // 2), jnp.uint32),
                   jax.ShapeDtypeStruct((N_EXPERTS, T), F32),
                   jax.ShapeDtypeStruct((TOP_K, T), jnp.int32)],
        compiler_params=_params(("arbitrary",)),
        name="router",
    )(x2, g, sc, sh, w_router_t, bias_col)


def _row_copy(src, src_row, dst, dst_row, sem):
    return pltpu.make_async_copy(src.at[pl.ds(src_row, 1), :], dst.at[pl.ds(dst_row, 1), :], sem)


def _dispatch_kernel(dest_ref, h_ref, xs_ref, sem, *, tm):
    def issue(t, carry):
        for j in range(TOP_K):
            _row_copy(h_ref, t, xs_ref, dest_ref[t * TOP_K + j], sem).start()
        return carry

    lax.fori_loop(0, tm, issue, 0)
    for _ in range(TOP_K):
        pltpu.make_async_copy(h_ref, xs_ref.at[pl.ds(0, tm), :], sem).wait()


def _dispatch_call(dest_flat, h2, tm=256):
    T, D = h2.shape
    return pl.pallas_call(
        functools.partial(_dispatch_kernel, tm=tm),
        grid=(T // tm,),
        in_specs=[pl.BlockSpec((tm * TOP_K,), lambda i: (i,), memory_space=pltpu.SMEM),
                  pl.BlockSpec((tm, D), lambda i: (i, 0))],
        out_specs=pl.BlockSpec(memory_space=pl.ANY),
        out_shape=jax.ShapeDtypeStruct((T * TOP_K, D), h2.dtype),
        scratch_shapes=[pltpu.SemaphoreType.DMA(())],
        compiler_params=_params(("arbitrary",)),
        name="dispatch",
    )(dest_flat, h2)


def _expert_kernel(tile_ref, exp_ref, lo_ref, hi_ref, n_ref, x_ref, wg_ref, wu_ref, wd_ref, o_ref,
                   wg_bf, wu_bf, wd_bf, *, te):
    s = pl.program_id(0)

    @pl.when(s < n_ref[0])
    def _():
        tile = tile_ref[s]
        prev = jnp.maximum(s - 1, 0)

        @pl.when((s == 0) | (exp_ref[prev] != exp_ref[s]))
        def _():
            wg_bf[...] = wg_ref[0, 0].astype(BF16)
            wu_bf[...] = wu_ref[0, 0].astype(BF16)
            wd_bf[...] = wd_ref[0, 0].astype(BF16)

        xl, xh = (t.astype(BF16) for t in _unpack_halves(x_ref[...]))
        half = xl.shape[1]
        gate = _dot(xl, wg_bf[0:half, :]) + _dot(xh, wg_bf[half:, :])
        up = _dot(xl, wu_bf[0:half, :]) + _dot(xh, wu_bf[half:, :])
        y = _pack_halves(_dot((_silu(gate) * up).astype(BF16), wd_bf[...]))
        row0 = tile * te
        whole = (lo_ref[s] <= row0) & (hi_ref[s] >= row0 + te)
        first_visit = (s == 0) | (tile_ref[prev] != tile)
        rows = row0 + lax.broadcasted_iota(jnp.int32, (te, 1), 0)
        mine = (rows >= lo_ref[s]) & (rows < hi_ref[s])

        @pl.when(whole)
        def _():
            o_ref[...] = y

        @pl.when(jnp.logical_not(whole) & first_visit)
        def _():
            o_ref[...] = jnp.where(mine, y, jnp.uint32(0))

        @pl.when(jnp.logical_not(whole) & jnp.logical_not(first_visit))
        def _():
            o_ref[...] = jnp.where(mine, y, o_ref[...])


def _expert_call(meta, x_sorted, w_gate, w_up, w_down, layer):
    A, DH = x_sorted.shape
    _, E, D, DE = w_gate.shape
    te = EXPERT_TILE
    n_steps = A // te + E - 1
    wspec = lambda r, c: pl.BlockSpec((1, 1, r, c), lambda s, tile, exp, lo, hi, n: (layer, exp[s], 0, 0))
    grid_spec = pltpu.PrefetchScalarGridSpec(
        num_scalar_prefetch=5,
        grid=(n_steps,),
        in_specs=[pl.BlockSpec((te, DH), lambda s, tile, exp, lo, hi, n: (tile[s], 0)),
                  wspec(D, DE), wspec(D, DE), wspec(DE, D)],
        out_specs=pl.BlockSpec((te, DH), lambda s, tile, exp, lo, hi, n: (tile[s], 0)),
        scratch_shapes=[pltpu.VMEM((D, DE), BF16), pltpu.VMEM((D, DE), BF16), pltpu.VMEM((DE, D), BF16)],
    )
    return pl.pallas_call(
        functools.partial(_expert_kernel, te=te),
        grid_spec=grid_spec,
        out_shape=jax.ShapeDtypeStruct((A, DH), jnp.uint32),
        compiler_params=_params(("arbitrary",)),
        name="experts",
    )(*meta, x_sorted, w_gate, w_up, w_down)


def _expert_steps(counts, n_rows):
    te = EXPERT_TILE
    E = counts.shape[0]
    n_steps = n_rows // te + E - 1
    ends = jnp.cumsum(counts)
    starts = ends - counts
    first_tile = starts // te
    last_tile = jnp.maximum(ends - 1, 0) // te
    n_tiles = jnp.where(counts > 0, last_tile - first_tile + 1, 0)
    step_end = jnp.cumsum(n_tiles)
    step_start = step_end - n_tiles
    total = step_end[-1]
    s = jnp.arange(n_steps, dtype=jnp.int32)
    s_eff = jnp.minimum(s, total - 1)
    exp = jnp.searchsorted(step_end, s_eff, side="right").astype(jnp.int32)
    tile = (first_tile[exp] + (s_eff - step_start[exp])).astype(jnp.int32)
    return (tile, exp, starts[exp].astype(jnp.int32), ends[exp].astype(jnp.int32),
            total.reshape(1).astype(jnp.int32))


def _combine_kernel(dest_ref, dest_next_ref, x_ref, h_ref, w_ref, ys_ref, wsg_ref, wsu_ref, wsd_ref,
                    g2_ref, fg_ref, o_ref, buf, sem, *, tm, n_tiles, final_norm):
    i = pl.program_id(0)
    slot = i % 2

    def issue(dref, sl):
        def body(t, carry):
            for j in range(TOP_K):
                _row_copy(ys_ref, dref[t * TOP_K + j], buf.at[sl, j], t, sem.at[sl]).start()
            return carry

        lax.fori_loop(0, tm, body, 0)

    @pl.when(i == 0)
    def _():
        issue(dest_ref, 0)

    @pl.when(i + 1 < n_tiles)
    def _():
        issue(dest_next_ref, 1 - slot)

    hl, hh = (t.astype(BF16) for t in _unpack_halves(h_ref[...]))
    half = hl.shape[1]
    gate = _dot(hl, wsg_ref[0:half, :]) + _dot(hh, wsg_ref[half:, :])
    up = _dot(hl, wsu_ref[0:half, :]) + _dot(hh, wsu_ref[half:, :])
    shared = _dot((_silu(gate) * up).astype(BF16), wsd_ref[...])
    for j in range(TOP_K):
        pltpu.make_async_copy(ys_ref.at[pl.ds(0, tm), :], buf.at[slot, j], sem.at[slot]).wait()
    w = w_ref[...]
    acc_lo = shared[:, :half]
    acc_hi = shared[:, half:]
    for j in range(TOP_K):
        lo, hi = _unpack_halves(buf[slot, j])
        acc_lo = acc_lo + lo * w[:, j:j + 1]
        acc_hi = acc_hi + hi * w[:, j:j + 1]
    acc = jnp.concatenate([acc_lo, acc_hi], axis=1)
    out = x_ref[...] + g2_ref[0] * acc
    if final_norm:
        ms = jnp.mean(out * out, axis=-1, keepdims=True)
        out = out * lax.rsqrt(ms + RMS_EPS) * fg_ref[...]
    o_ref[...] = out


def _combine_call(dest_flat, x2, h2, w_tok, y_sorted, w_sh_gate, w_sh_up, w_sh_down, g2, final_g,
                  seq, final_norm, tm=256):
    T, D = x2.shape
    n_tiles = T // tm
    const = lambda shape: pl.BlockSpec(shape, lambda i: (0,) * len(shape),
                                       pipeline_mode=pl.Buffered(1))
    return pl.pallas_call(
        functools.partial(_combine_kernel, tm=tm, n_tiles=n_tiles, final_norm=final_norm),
        grid=(n_tiles,),
        in_specs=[pl.BlockSpec((tm * TOP_K,), lambda i: (i,), memory_space=pltpu.SMEM),
                  pl.BlockSpec((tm * TOP_K,), lambda i: (jnp.minimum(i + 1, n_tiles - 1),),
                               memory_space=pltpu.SMEM),
                  pl.BlockSpec((tm, D), lambda i: (i, 0)),
                  pl.BlockSpec((tm, D // 2), lambda i: (i, 0)),
                  pl.BlockSpec((tm, TOP_K), lambda i: (i, 0)),
                  pl.BlockSpec(memory_space=pl.ANY),
                  const(w_sh_gate.shape),
                  const(w_sh_up.shape),
                  const(w_sh_down.shape),
                  pl.BlockSpec((1, 1, D), lambda i: ((i * tm) // seq, 0, 0)),
                  const(final_g.shape)],
        out_specs=pl.BlockSpec((tm, D), lambda i: (i, 0)),
        out_shape=jax.ShapeDtypeStruct((T, D), F32),
        scratch_shapes=[pltpu.VMEM((2, TOP_K, tm, D // 2), jnp.uint32), pltpu.SemaphoreType.DMA((2,))],
        compiler_params=_params(("arbitrary",)),
        name="combine",
    )(dest_flat, dest_flat, x2, h2, w_tok, y_sorted, w_sh_gate, w_sh_up, w_sh_down, g2, final_g)


def _pad_cols(w, n):
    return jnp.pad(w, ((0, 0), (0, n - w.shape[1])))


def _pad_rows(w, n):
    return jnp.pad(w, ((0, n - w.shape[0]), (0, 0)))


def _pack_cols(parts):
    return jnp.concatenate([_pad_cols(a, n) for a, n in parts], axis=1)


def kernel(x, c, ada_w, ada_b, norm1_g, norm2_g, w_in, mu_shift, w_decay_up, decay_base, w_a_up, a_base, w_g_up, k_k, k_a, r_k, lnx_g, lnx_b, w_v_down, mu_v, w_v_up, v_base, pool_w, pool_scale, w_br_rwkv, w_br_pool, w_out, w_router, router_bias, w_exp_gate, w_exp_up, w_exp_down, w_sh_gate, w_sh_up, w_sh_down, final_g):
    B, S, D = x.shape
    T = B * S
    L = ada_w.shape[0]
    DR = w_decay_up.shape[2]
    n_dec, n_a, n_g, n_v = w_decay_up.shape[1], w_a_up.shape[1], w_g_up.shape[1], w_v_down.shape[2]
    o_k, o_v, o_wd = DR, 2 * DR, 3 * DR
    o_ad = o_wd + n_dec
    o_gd = o_ad + n_a
    n_shift = o_gd + n_g
    o_gate = n_shift + DR

    c_pad = jnp.pad(c, ((0, 8 - B), (0, 0)))
    mod = _ada_call(c_pad, ada_w, ada_b)[:, :B]
    mod = mod.reshape(L, B, 6, 1, D)

    x2 = x.reshape(T, D)
    proj_first = None
    for l in range(L):
        sh1, sc1, g1, sh2, sc2, g2 = (mod[l, :, q] for q in range(6))
        has_vres = l > 0
        wl = w_in[l]
        zero_cols = jnp.zeros((D, 0), F32)
        vd_w = w_v_down[l - 1] if has_vres else zero_cols
        w_packed = _pack_cols([
            (wl[:, :o_wd], 3 * DR),
            (wl[:, o_wd:o_ad], PAIR), (wl[:, o_ad:o_gd], PAIR), (wl[:, o_gd:n_shift], 2 * PAIR),
            (vd_w, PAIR), (zero_cols, PROJ_TN - 5 * PAIR),
            (wl[:, o_gate:], 2 * D),
            (wl[:, n_shift:o_gate], DR)]).astype(BF16)
        ml = mu_shift[l][None, :]
        zero_mu = jnp.zeros((1, 0), F32)
        vd_mu = mu_v[l - 1][None, :] if has_vres else zero_mu
        mu_packed = _pack_cols([
            (ml[:, :o_wd], 3 * DR),
            (ml[:, o_wd:o_ad], PAIR), (ml[:, o_ad:o_gd], PAIR), (ml[:, o_gd:n_shift], 2 * PAIR),
            (vd_mu, PAIR), (zero_mu, PROJ_TN - 5 * PAIR), (zero_mu, 2 * D + DR)])
        proj = _proj_call(x2, norm1_g[l][None, :], sc1, sh1, w_packed, mu_packed, S)
        if l == 0:
            proj_first = proj

        lora_w = {"decay": _pad_rows(w_decay_up[l], PAIR), "a": _pad_rows(w_a_up[l], PAIR),
                  "g": w_g_up[l]}
        zeros_dr = jnp.zeros((DR,), F32)
        if has_vres:
            lora_w["v"] = _pad_rows(w_v_up[l - 1], PAIR)
        vecs = jnp.stack([decay_base[l], a_base[l], k_k[l], k_a[l], r_k[l].reshape(DR), lnx_g[l],
                          lnx_b[l], v_base[l - 1] if has_vres else zeros_dr])
        y_rwkv = _rwkv_call(proj.reshape(B, S, -1), proj_first.reshape(B, S, -1), lora_w, vecs,
                            has_vres).reshape(T, DR)

        x2 = _merge_call(x2, y_rwkv, proj, pool_w[l].astype(BF16), pool_scale[l][None, :],
                         w_br_rwkv[l].astype(BF16), w_br_pool[l].astype(BF16),
                         w_out[l].astype(BF16), g1, S)

        h2, gate_t, idx_t = _router_call(x2, norm2_g[l][None, :], sc2, sh2, w_router[l].T,
                                         router_bias[l][:, None], S)
        experts = jnp.arange(N_EXPERTS, dtype=jnp.int32)[None, :, None]
        sel = jnp.any(idx_t[:, None, :] == experts, axis=0).astype(jnp.int32)
        incl = jnp.cumsum(sel, axis=1)
        counts = incl[:, -1]
        start = jnp.cumsum(counts) - counts
        slot = start[:, None] + incl - sel
        dest = jnp.take_along_axis(slot, idx_t, axis=0)
        dest_flat = dest.T.reshape(T * TOP_K).astype(jnp.int32)
        w_tok = jnp.take_along_axis(gate_t, idx_t, axis=0).T

        x_sorted = _dispatch_call(dest_flat, h2)
        meta = _expert_steps(counts, T * TOP_K)
        y_sorted = _expert_call(meta, x_sorted, w_exp_gate, w_exp_up, w_exp_down, l)
        x2 = _combine_call(dest_flat, x2, h2, w_tok, y_sorted, w_sh_gate[l].astype(BF16),
                           w_sh_up[l].astype(BF16), w_sh_down[l].astype(BF16), g2,
                           final_g[None, :], S, final_norm=(l == L - 1))
    return x2.reshape(B, S, D)
```

```python
import functools
import math

import jax
import jax.numpy as jnp
from jax import lax
from jax.experimental import pallas as pl
from jax.experimental.pallas import tpu as pltpu

F32 = jnp.float32
BF16 = jnp.bfloat16

HEAD = 64
PAIR = 2 * HEAD
CHUNK = 64
POOL_WINDOWS = (2, 4, 8, 16)
POOL_HIST = 16
N_EXPERTS = 64
TOP_K = 8
N_GROUPS = 8
TOPK_GROUPS = 4
ROUTED_SCALE = 2.5
GN_EPS = 64e-5
RMS_EPS = 1e-6
NEG_INF = float("-inf")

PROJ_TN = 1024
N_SHIFT_TILES = 4
EXPERT_TILE = 256
VMEM_LIMIT = 56 * 1024 * 1024

_NN = (((1,), (0,)), ((), ()))
_NT = (((1,), (1,)), ((), ()))


def _dot(a, b, dims=_NN):
    return lax.dot_general(a, b, dims, preferred_element_type=F32)


def _bdot(a, b, dims=_NN):
    return _dot(a.astype(BF16), b.astype(BF16), dims)


def _split(a):
    hi = a.astype(BF16)
    lo = (a - hi.astype(F32)).astype(BF16)
    return hi, lo


def _dot3s(ap, bp, dims=_NN):
    (ah, al), (bh, bl) = ap, bp
    return _dot(ah, bh, dims) + (_dot(ah, bl, dims) + _dot(al, bh, dims))


def _dot3(a, b, dims=_NN):
    return _dot3s(_split(a), _split(b), dims)


def _dot2c(a, c_bf16, dims=_NN):
    ah, al = _split(a)
    return _dot(ah, c_bf16, dims) + _dot(al, c_bf16, dims)


def _sigmoid(x):
    return 1.0 / (1.0 + jnp.exp(-x))


def _silu(x):
    return x * _sigmoid(x)


def _rms_mod(x, g, sc, sh):
    ms = jnp.mean(x * x, axis=-1, keepdims=True)
    return (x * lax.rsqrt(ms + RMS_EPS) * g) * (1.0 + sc) + sh


HI16 = 0xFFFF0000


def _pack_halves(x):
    n = x.shape[1] // 2
    bits = lax.bitcast_convert_type(x.astype(BF16).astype(F32), jnp.uint32)
    return (bits[:, n:] & jnp.uint32(HI16)) | (bits[:, :n] >> 16)


def _unpack_halves(p):
    lo = lax.bitcast_convert_type(p << 16, F32)
    hi = lax.bitcast_convert_type(p & jnp.uint32(HI16), F32)
    return lo, hi


def _params(sem):
    return pltpu.CompilerParams(dimension_semantics=sem, vmem_limit_bytes=VMEM_LIMIT)


def _ada_kernel(c_ref, w_ref, b_ref, o_ref):
    cond = _silu(c_ref[...])
    o_ref[0] = _bdot(cond, w_ref[0]) + b_ref[0]


def _ada_call(c_pad, ada_w, ada_b):
    L, D, N6 = ada_w.shape
    tn = 1024
    return pl.pallas_call(
        _ada_kernel,
        grid=(L, N6 // tn),
        in_specs=[pl.BlockSpec(c_pad.shape, lambda l, j: (0, 0)),
                  pl.BlockSpec((1, D, tn), lambda l, j: (l, 0, j)),
                  pl.BlockSpec((1, 1, tn), lambda l, j: (l, 0, j))],
        out_specs=pl.BlockSpec((1, c_pad.shape[0], tn), lambda l, j: (l, 0, j)),
        out_shape=jax.ShapeDtypeStruct((L, c_pad.shape[0], N6), F32),
        compiler_params=_params(("arbitrary", "arbitrary")),
        name="adaln",
    )(c_pad, ada_w, ada_b.reshape(L, 1, N6))


def _proj_kernel(x_ref, g_ref, sc_ref, sh_ref, w_ref, mu_ref, o_ref, h_scr, prev_scr, *, tm, seq):
    i = pl.program_id(0)
    j = pl.program_id(1)

    @pl.when(j == 0)
    def _():
        h_scr[...] = _rms_mod(x_ref[...], g_ref[...], sc_ref[0], sh_ref[0]).astype(BF16)

    p = _dot(h_scr[...], w_ref[...])

    @pl.when(j < N_SHIFT_TILES)
    def _():
        seq_start = (i * tm) % seq == 0
        carry = jnp.where(seq_start, 0.0, prev_scr[j])
        row = lax.broadcasted_iota(jnp.int32, p.shape, 0)
        prev = jnp.where(row == 0, carry, pltpu.roll(p, 1, 0))
        o_ref[...] = (p + (prev - p) * mu_ref[...]).astype(o_ref.dtype)
        prev_scr[j] = p[tm - 1:tm, :]

    @pl.when(j >= N_SHIFT_TILES)
    def _():
        o_ref[...] = p.astype(o_ref.dtype)


def _proj_call(x2, g, sc, sh, w_packed, mu_packed, seq, tm=1024):
    T, D = x2.shape
    NP = w_packed.shape[1]
    nj = NP // PROJ_TN
    bmap = lambda i, j: ((i * tm) // seq, 0, 0)
    return pl.pallas_call(
        functools.partial(_proj_kernel, tm=tm, seq=seq),
        grid=(T // tm, nj),
        in_specs=[pl.BlockSpec((tm, D), lambda i, j: (i, 0)),
                  pl.BlockSpec((1, D), lambda i, j: (0, 0)),
                  pl.BlockSpec((1, 1, D), bmap),
                  pl.BlockSpec((1, 1, D), bmap),
                  pl.BlockSpec((D, PROJ_TN), lambda i, j: (0, j)),
                  pl.BlockSpec((1, PROJ_TN), lambda i, j: (0, j))],
        out_specs=pl.BlockSpec((tm, PROJ_TN), lambda i, j: (i, j)),
        out_shape=jax.ShapeDtypeStruct((T, NP), BF16),
        scratch_shapes=[pltpu.VMEM((tm, D), BF16),
                        pltpu.VMEM((N_SHIFT_TILES, 1, PROJ_TN), F32)],
        compiler_params=_params(("arbitrary", "arbitrary")),
        name="proj",
    )(x2, g, sc, sh, w_packed, mu_packed)


RWKV_PASSES = {"lora": 1, "gram": 1, "inv": 1, "apply": 1}
RWKV_BATCH = 4
RWKV_PAIRS = 2
RWKV_ROWS = 512


def _splitp(a, passes):
    hi = a.astype(BF16)
    if passes == 1:
        return hi, None
    return hi, (a - hi.astype(F32)).astype(BF16)


def _mm(ap, bp, passes, dims=_NN):
    (ah, al), (bh, bl) = ap, bp
    out = _dot(ah, bh, dims)
    if passes == 1:
        return out
    lo = None
    if bl is not None:
        lo = _dot(ah, bl, dims)
    if al is not None:
        t = _dot(al, bh, dims)
        lo = t if lo is None else lo + t
    return out if lo is None else out + lo


def _stack(z, lane_lo):
    return jnp.concatenate([jnp.where(lane_lo, z, 0.0), jnp.where(lane_lo, 0.0, z)], axis=0)


def _rwkv_kernel(*refs, has_vres, nb, npl, rg):
    if has_vres:
        (r_ref, k_ref, v_ref, wd_ref, ad_ref, gd_ref, vd_ref, vf_ref,
         wdu_ref, wau_ref, wgu_ref, wvu_ref, vec_ref, y_ref, st_ref) = refs
    else:
        (r_ref, k_ref, v_ref, wd_ref, ad_ref, gd_ref,
         wdu_ref, wau_ref, wgu_ref, vec_ref, y_ref, st_ref) = refs
    p_lora, p_gram, p_inv, p_app = (RWKV_PASSES[s] for s in ("lora", "gram", "inv", "apply"))
    p_in = max(p_gram, p_app)
    C = CHUNK
    C2 = 2 * C

    @pl.when(pl.program_id(2) == 0)
    def _():
        st_ref[...] = jnp.zeros_like(st_ref)

    lane_lo = lax.broadcasted_iota(jnp.int32, (C, PAIR), 1) < HEAD
    ri = lax.broadcasted_iota(jnp.int32, (C2, C2), 0)
    ci = lax.broadcasted_iota(jnp.int32, (C2, C2), 1)
    strict = ri > ci
    incl = ri >= ci
    eye = jnp.where(ri == ci, 1.0, 0.0)
    wl = npl * PAIR
    ones_head = jnp.where(lax.broadcasted_iota(jnp.int32, (wl, wl), 0) // HEAD
                          == lax.broadcasted_iota(jnp.int32, (wl, wl), 1) // HEAD,
                          1.0, 0.0).astype(BF16)
    rb = lax.broadcasted_iota(jnp.int32, (nb * C, nb * C), 0)
    cb = lax.broadcasted_iota(jnp.int32, (nb * C, nb * C), 1)
    same_seq = (rb // C) == (cb // C)
    tri_blk = jnp.where(same_seq & (rb >= cb), 1.0, 0.0).astype(BF16)
    ones_blk = jnp.where(same_seq, 1.0, 0.0).astype(BF16)

    vec = vec_ref[...]
    decay_base, a_base, k_k, k_a = vec[0:1], vec[1:2], vec[2:3], vec[3:4]
    r_k, lnx_g, lnx_b, v_base = vec[4:5], vec[5:6], vec[6:7], vec[7:8]
    wdu = _splitp(wdu_ref[...], p_lora)
    wau = _splitp(wau_ref[...], p_lora)
    wgu = _splitp(wgu_ref[...], p_lora)
    if has_vres:
        wvu = _splitp(wvu_ref[...], p_lora)

    chains = [(b, q) for b in range(nb) for q in range(npl)]
    seqs = range(len(chains))

    def chunk_body(c, carry):
        rows = pl.ds(pl.multiple_of(c * C, C), C)
        cat = lambda ref: jnp.concatenate([ref[b, rows, :].astype(F32) for b in range(nb)], axis=0)

        def part(x, i):
            b, q = chains[i]
            return x[b * C:(b + 1) * C, q * PAIR:(q + 1) * PAIR]

        r, k, v = cat(r_ref), cat(k_ref), cat(v_ref)
        dl = decay_base + _mm(_splitp(jnp.tanh(cat(wd_ref)), p_lora), wdu, p_lora)
        z = -dl
        softplus = jnp.maximum(z, 0.0) + jnp.log(1.0 + jnp.exp(-jnp.abs(z)))
        lw = -jnp.exp(-softplus - 0.5)
        a = _sigmoid(a_base + _mm(_splitp(cat(ad_ref), p_lora), wau, p_lora))
        g = _mm(_splitp(_sigmoid(cat(gd_ref)), p_lora), wgu, p_lora)
        if has_vres:
            mix = _sigmoid(v_base + _mm(_splitp(cat(vd_ref), p_lora), wvu, p_lora))
            v = v + (cat(vf_ref) - v) * mix
        kkr = k * k_k
        kk = kkr / jnp.maximum(jnp.sqrt(_dot2c(kkr * kkr, ones_head)), 1e-12)
        k2 = k * (1.0 + (a - 1.0) * k_a)
        avec = -kk
        bvec = kk * a

        lwh, lwl = _split(lw)
        cl = _dot(tri_blk, lwh) + _dot(tri_blk, lwl)
        tot = _dot(ones_blk, lwh) + _dot(ones_blk, lwl)
        w_inv = jnp.exp(-cl)
        w_tail = jnp.exp(tot - cl)
        w_tot = jnp.exp(tot)
        a_t = avec * jnp.exp(cl - lw)
        r_t = r * jnp.exp(cl)
        b_t = bvec * w_inv
        k_t = k2 * w_inv
        b_h = bvec * w_tail
        k_h = k2 * w_tail

        pair2 = lambda x, y, b: jnp.concatenate([_stack(part(x, b), lane_lo),
                                                 _stack(part(y, b), lane_lo)], axis=0)
        lhs = [_splitp(pair2(a_t, r_t, b), p_in) for b in seqs]
        rhs = [_splitp(pair2(b_t, k_t, b), p_gram) for b in seqs]
        gram = [_mm(lhs[b], rhs[b], p_gram, _NT) for b in seqs]
        a_ab = [jnp.where(strict, gm[:C2, :C2], 0.0) for gm in gram]
        a_ak = [jnp.where(strict, gm[:C2, C2:], 0.0) for gm in gram]
        a_r = [jnp.concatenate([jnp.where(incl, gm[C2:, :C2], 0.0),
                                jnp.where(incl, gm[C2:, C2:], 0.0)], axis=1) for gm in gram]

        pw = a_ab
        inv = [eye + m for m in a_ab]
        for _ in range(int(math.log2(C)) - 1):
            pws = [_splitp(m, p_inv) for m in pw]
            pw = [_mm(m, m, p_inv) for m in pws]
            inv = [inv[b] + _mm(_splitp(inv[b], p_inv), _splitp(pw[b], p_inv), p_inv) for b in seqs]

        s_prev = [st_ref[b] for b in seqs]
        sx = [_mm(lhs[b], _splitp(s_prev[b], p_app), p_app, _NT) for b in seqs]
        v_st = [_stack(part(v, b), lane_lo) for b in seqs]
        x_st = [sx[b][:C2] + _mm(_splitp(a_ak[b], p_app), _splitp(v_st[b], p_app), p_app) for b in seqs]
        u_st = [_mm(_splitp(inv[b], p_app), _splitp(x_st[b], p_app), p_app) for b in seqs]
        uv = [jnp.concatenate([u_st[b], v_st[b]], axis=0) for b in seqs]
        y_st = [sx[b][C2:] + _mm(_splitp(a_r[b], p_app), _splitp(uv[b], p_app), p_app) for b in seqs]
        for b in seqs:
            bk = _splitp(pair2(b_h, k_h, b), p_app)
            st_ref[b] = (s_prev[b] * part(w_tot, b)[0:1]
                         + _mm(_splitp(uv[b].T, p_app), bk, p_app))
        y_ch = [ys[:C] + ys[C:] for ys in y_st]
        y = jnp.concatenate([jnp.concatenate(y_ch[b * npl:(b + 1) * npl], axis=1)
                             for b in range(nb)], axis=0)

        inv_n = 1.0 / HEAD
        mu = _dot2c(y, ones_head) * inv_n
        yc = y - mu
        var = _dot2c(yc * yc, ones_head) * inv_n
        yn = yc * lax.rsqrt(var + GN_EPS) * lnx_g + lnx_b
        bonus = _dot2c(r * k2 * r_k, ones_head) * v
        out = (yn + bonus) * g
        for b in range(nb):
            y_ref[b, rows, :] = out[b * C:(b + 1) * C]
        return carry

    lax.fori_loop(0, rg // C, chunk_body, 0)


def _rwkv_call(proj, proj_first, lora_w, vecs, has_vres):
    B, S, _ = proj.shape
    nb, rg, npl = min(RWKV_BATCH, B), min(RWKV_ROWS, S), RWKV_PAIRS
    dr = vecs.shape[1]
    wl = npl * PAIR
    n_blk = dr // wl
    col = lambda off: (lambda bb, p, t: (bb, t, off * n_blk + p))
    fixed = lambda blk: (lambda bb, p, t: (bb, t, blk))
    blk = lambda w, imap: pl.BlockSpec((nb, rg, w), imap)
    lora0 = 3 * dr // PAIR
    in_specs = [blk(wl, col(0)),
                blk(wl, col(1)),
                blk(wl, col(2)),
                blk(PAIR, fixed(lora0)),
                blk(PAIR, fixed(lora0 + 1)),
                blk(2 * PAIR, fixed(lora0 // 2 + 1))]
    args = [proj] * 6
    if has_vres:
        in_specs += [blk(PAIR, fixed(lora0 + 4)),
                     blk(wl, col(2))]
        args += [proj, proj_first]
    wspec = lambda rows: pl.BlockSpec((rows, wl), lambda bb, p, t: (0, p))
    in_specs += [wspec(PAIR), wspec(PAIR), wspec(2 * PAIR)]
    args += [lora_w["decay"], lora_w["a"], lora_w["g"]]
    if has_vres:
        in_specs += [wspec(PAIR)]
        args += [lora_w["v"]]
    in_specs += [pl.BlockSpec((8, wl), lambda bb, p, t: (0, p))]
    args += [vecs]
    return pl.pallas_call(
        functools.partial(_rwkv_kernel, has_vres=has_vres, nb=nb, npl=npl, rg=rg),
        grid=(B // nb, n_blk, S // rg),
        in_specs=in_specs,
        out_specs=pl.BlockSpec((nb, rg, wl), lambda bb, p, t: (bb, t, p)),
        out_shape=jax.ShapeDtypeStruct((B, S, dr), F32),
        scratch_shapes=[pltpu.VMEM((nb * npl, PAIR, PAIR), F32)],
        compiler_params=_params(("arbitrary", "arbitrary", "arbitrary")),
        name="rwkv7",
    )(*args)


def _merge_kernel(x_ref, y_ref, pool_ref, gr_ref, gp_ref, pw_ref, ps_ref, wbr_ref, wbp_ref, wo_ref,
                  g1_ref, o_ref, hist_ref, *, tm, seq):
    i = pl.program_id(0)
    seq_off = (i * tm) % seq
    p = pool_ref[...].astype(F32)
    hist =jnp.where(seq_off == 0, 0.0, hist_ref[...])
    ext = jnp.concatenate([hist, p], axis=0)
    pos = (seq_off + 1 + lax.broadcasted_iota(jnp.int32, (tm, 1), 0)).astype(F32)
    gw = p.shape[1] // len(POOL_WINDOWS)
    zs = []
    for gi, win in enumerate(POOL_WINDOWS):
        e = ext[:, gi * gw:(gi + 1) * gw]
        s = e
        shift = 1
        while shift < win:
            n = s.shape[0]
            s = s[shift:, :] + s[:n - shift, :]
            shift *= 2
        lo = POOL_HIST + 1 - win
        wsum = s[lo:lo + tm, :]
        u = wsum / jnp.minimum(pos, float(win)) - p[:, gi * gw:(gi + 1) * gw]
        zs.append(_bdot(u, pw_ref[gi]))
    y_pool = jnp.concatenate(zs, axis=1) * ps_ref[...]
    hist_ref[...] = p[tm - POOL_HIST:, :]

    br = _bdot(y_ref[...], wbr_ref[...])
    bp = _bdot(y_pool, wbp_ref[...])
    merged = _sigmoid(gr_ref[...].astype(F32)) * br + _sigmoid(gp_ref[...].astype(F32)) * bp
    o_ref[...] = x_ref[...] + g1_ref[0] * _bdot(merged, wo_ref[...])


def _merge_call(x2, y_rwkv, proj, pool_w, pool_scale, w_br_rwkv, w_br_pool, w_out, g1, seq, tm=256):
    T, D = x2.shape
    DR = y_rwkv.shape[1]
    const = lambda shape: pl.BlockSpec(shape, lambda i: (0,) * len(shape),
                                       pipeline_mode=pl.Buffered(1))
    return pl.pallas_call(
        functools.partial(_merge_kernel, tm=tm, seq=seq),
        grid=(T // tm,),
        in_specs=[pl.BlockSpec((tm, D), lambda i: (i, 0)),
                  pl.BlockSpec((tm, DR), lambda i: (i, 0)),
                  pl.BlockSpec((tm, DR), lambda i: (i, 8)),
                  pl.BlockSpec((tm, D), lambda i: (i, 2)),
                  pl.BlockSpec((tm, D), lambda i: (i, 3)),
                  const(pool_w.shape),
                  const(pool_scale.shape),
                  const(w_br_rwkv.shape),
                  const(w_br_pool.shape),
                  const(w_out.shape),
                  pl.BlockSpec((1, 1, D), lambda i: ((i * tm) // seq, 0, 0))],
        out_specs=pl.BlockSpec((tm, D), lambda i: (i, 0)),
        out_shape=jax.ShapeDtypeStruct((T, D), F32),
        scratch_shapes=[pltpu.VMEM((POOL_HIST, DR), F32)],
        compiler_params=_params(("arbitrary",)),
        name="merge",
    )(x2, y_rwkv, proj, proj, proj, pool_w, pool_scale, w_br_rwkv, w_br_pool, w_out, g1)


def _first_argmax(cur, idx, axis):
    m = jnp.max(cur, axis=axis, keepdims=True)
    first = jnp.min(jnp.where(cur == m, idx, 1 << 20), axis=axis, keepdims=True)
    return m, first


def _router_kernel(x_ref, g_ref, sc_ref, sh_ref, wr_ref, bias_ref, h_ref, gt_ref, idx_ref, *, tm):
    h = _rms_mod(x_ref[...], g_ref[...], sc_ref[0], sh_ref[0])
    h_ref[...] = _pack_halves(h)
    scores =_sigmoid(_dot3(wr_ref[...], h, _NT))
    choice = scores + bias_ref[...]
    per = N_EXPERTS // N_GROUPS
    ch3 = choice.reshape(N_GROUPS, per, tm)
    within = lax.broadcasted_iota(jnp.int32, ch3.shape, 1)
    m1, i1 = _first_argmax(ch3, within, 1)
    m2 = jnp.max(jnp.where(within == i1, NEG_INF, ch3), axis=1, keepdims=True)
    grp = (m1 + m2).reshape(N_GROUPS, tm)
    gidx = lax.broadcasted_iota(jnp.int32, grp.shape, 0)
    gsel = jnp.zeros(grp.shape, F32)
    for _ in range(TOPK_GROUPS):
        _, gi = _first_argmax(grp, gidx, 0)
        pick = gidx == gi
        gsel = jnp.where(pick, 1.0, gsel)
        grp = jnp.where(pick, NEG_INF, grp)
    gsel3 = jnp.broadcast_to(gsel.reshape(N_GROUPS, 1, tm), ch3.shape)
    cur = jnp.where(gsel3 > 0.5, ch3, NEG_INF).reshape(N_EXPERTS, tm)
    eidx = lax.broadcasted_iota(jnp.int32, cur.shape, 0)
    sel = jnp.zeros(cur.shape, F32)
    picks = []
    for _ in range(TOP_K):
        _, ei = _first_argmax(cur, eidx, 0)
        pick = eidx == ei
        sel = jnp.where(pick, 1.0, sel)
        cur = jnp.where(pick, NEG_INF, cur)
        picks.append(ei)
    w = jnp.where(sel > 0.5, scores, 0.0)
    gt_ref[...] = w / jnp.sum(w, axis=0, keepdims=True) * ROUTED_SCALE
    idx_ref[...] = jnp.concatenate(picks, axis=0)


def _router_call(x2, g, sc, sh, w_router_t, bias_col, seq, tm=256):
    T, D = x2.shape
    bmap = lambda i: ((i * tm) // seq, 0, 0)
    return pl.pallas_call(
        functools.partial(_router_kernel, tm=tm),
        grid=(T // tm,),
        in_specs=[pl.BlockSpec((tm, D), lambda i: (i, 0)),
                  pl.BlockSpec((1, D), lambda i: (0, 0)),
                  pl.BlockSpec((1, 1, D), bmap),
                  pl.BlockSpec((1, 1, D), bmap),
                  pl.BlockSpec((N_EXPERTS, D), lambda i: (0, 0)),
                  pl.BlockSpec((N_EXPERTS, 1), lambda i: (0, 0))],
        out_specs=[pl.BlockSpec((tm, D // 2), lambda i: (i, 0)),
                   pl.BlockSpec((N_EXPERTS, tm), lambda i: (0, i)),
                   pl.BlockSpec((TOP_K, tm), lambda i: (0, i))],
        out_shape=[jax.ShapeDtypeStruct((T, D // 2), jnp.uint32),
                   jax.ShapeDtypeStruct((N_EXPERTS, T), F32),
                   jax.ShapeDtypeStruct((TOP_K, T), jnp.int32)],
        compiler_params=_params(("arbitrary",)),
        name="router",
    )(x2, g, sc, sh, w_router_t, bias_col)


def _row_copy(src, src_row, dst, dst_row, sem):
    return pltpu.make_async_copy(src.at[pl.ds(src_row, 1), :], dst.at[pl.ds(dst_row, 1), :], sem)


def _dispatch_kernel(dest_ref, h_ref, xs_ref, sem, *, tm):
    def issue(t, carry):
        for j in range(TOP_K):
            _row_copy(h_ref, t, xs_ref, dest_ref[t * TOP_K + j], sem).start()
        return carry

    lax.fori_loop(0, tm, issue, 0)
    for _ in range(TOP_K):
        pltpu.make_async_copy(h_ref, xs_ref.at[pl.ds(0, tm), :], sem).wait()


def _dispatch_call(dest_flat, h2, tm=256):
    T, D = h2.shape
    return pl.pallas_call(
        functools.partial(_dispatch_kernel, tm=tm),
        grid=(T // tm,),
        in_specs=[pl.BlockSpec((tm * TOP_K,), lambda i: (i,), memory_space=pltpu.SMEM),
                  pl.BlockSpec((tm, D), lambda i: (i, 0))],
        out_specs=pl.BlockSpec(memory_space=pl.ANY),
        out_shape=jax.ShapeDtypeStruct((T * TOP_K, D), h2.dtype),
        scratch_shapes=[pltpu.SemaphoreType.DMA(())],
        compiler_params=_params(("arbitrary",)),
        name="dispatch",
    )(dest_flat, h2)


def _expert_kernel(tile_ref, exp_ref, nxt_ref, slot_ref, lo_ref, hi_ref, n_ref, x_ref, wg_hbm, wu_hbm, wd_hbm,
                   o_ref, wg_st, wu_st, wd_st, wg_bf, wu_bf, wd_bf, sem, *, te, layer):
    s = pl.program_id(0)

    def weight_copies(e, sl):
        return [pltpu.make_async_copy(src.at[layer, e], dst.at[sl], sem.at[k, sl])
                for k, (src, dst) in enumerate(((wg_hbm, wg_st), (wu_hbm, wu_st), (wd_hbm, wd_st)))]

    @pl.when(s < n_ref[0])
    def _():
        tile = tile_ref[s]
        prev = jnp.maximum(s - 1, 0)
        e = exp_ref[s]
        sl = slot_ref[s]

        @pl.when(s == 0)
        def _():
            for cp in weight_copies(e, sl):
                cp.start()

        @pl.when((s == 0) | (exp_ref[prev] != e))
        def _():
            for cp in weight_copies(e, sl):
                cp.wait()
            wg_bf[...] = wg_st[sl].astype(BF16)
            wu_bf[...] = wu_st[sl].astype(BF16)
            wd_bf[...] = wd_st[sl].astype(BF16)

            @pl.when(nxt_ref[s] >= 0)
            def _():
                for cp in weight_copies(nxt_ref[s], 1 - sl):
                    cp.start()

        xl, xh = (t.astype(BF16) for t in _unpack_halves(x_ref[...]))
        half = xl.shape[1]
        gate = _dot(xl, wg_bf[0:half, :]) + _dot(xh, wg_bf[half:, :])
        up = _dot(xl, wu_bf[0:half, :]) + _dot(xh, wu_bf[half:, :])
        y = _pack_halves(_dot((_silu(gate) * up).astype(BF16), wd_bf[...]))
        row0 = tile * te
        whole = (lo_ref[s] <= row0) & (hi_ref[s] >= row0 + te)
        first_visit = (s == 0) | (tile_ref[prev] != tile)
        rows = row0 + lax.broadcasted_iota(jnp.int32, (te, 1), 0)
        mine = (rows >= lo_ref[s]) & (rows < hi_ref[s])

        @pl.when(whole)
        def _():
            o_ref[...] = y

        @pl.when(jnp.logical_not(whole) & first_visit)
        def _():
            o_ref[...] = jnp.where(mine, y, jnp.uint32(0))

        @pl.when(jnp.logical_not(whole) & jnp.logical_not(first_visit))
        def _():
            o_ref[...] = jnp.where(mine, y, o_ref[...])


def _expert_call(meta, x_sorted, w_gate, w_up, w_down, layer):
    A, DH = x_sorted.shape
    _, E, D, DE = w_gate.shape
    te = EXPERT_TILE
    n_steps = A // te + E - 1
    hbm = pl.BlockSpec(memory_space=pl.ANY)
    grid_spec = pltpu.PrefetchScalarGridSpec(
        num_scalar_prefetch=7,
        grid=(n_steps,),
        in_specs=[pl.BlockSpec((te, DH), lambda s, tile, *_: (tile[s], 0)), hbm, hbm, hbm],
        out_specs=pl.BlockSpec((te, DH), lambda s, tile, *_: (tile[s], 0)),
        scratch_shapes=[pltpu.VMEM((2, D, DE), F32), pltpu.VMEM((2, D, DE), F32), pltpu.VMEM((2, DE, D), F32),
                        pltpu.VMEM((D, DE), BF16), pltpu.VMEM((D, DE), BF16), pltpu.VMEM((DE, D), BF16),
                        pltpu.SemaphoreType.DMA((3, 2))],
    )
    return pl.pallas_call(
        functools.partial(_expert_kernel, te=te, layer=layer),
        grid_spec=grid_spec,
        out_shape=jax.ShapeDtypeStruct((A, DH), jnp.uint32),
        compiler_params=_params(("arbitrary",)),
        name="experts",
    )(*meta, x_sorted, w_gate, w_up, w_down)


def _expert_steps(counts, n_rows):
    te = EXPERT_TILE
    E = counts.shape[0]
    n_steps = n_rows // te + E - 1
    ends = jnp.cumsum(counts)
    starts = ends - counts
    first_tile = starts // te
    last_tile = jnp.maximum(ends - 1, 0) // te
    n_tiles = jnp.where(counts > 0, last_tile - first_tile + 1, 0)
    step_end = jnp.cumsum(n_tiles)
    step_start = step_end - n_tiles
    total = step_end[-1]
    s = jnp.arange(n_steps, dtype=jnp.int32)
    s_eff = jnp.minimum(s, total - 1)
    exp = jnp.searchsorted(step_end, s_eff, side="right").astype(jnp.int32)
    tile = (first_tile[exp] + (s_eff - step_start[exp])).astype(jnp.int32)
    ids = jnp.arange(E, dtype=jnp.int32)
    used = counts > 0
    at_or_after = jnp.flip(lax.cummin(jnp.flip(jnp.where(used, ids, E))))
    nxt = jnp.concatenate([at_or_after[1:], jnp.full((1,), E, jnp.int32)])
    nxt = jnp.where(nxt >= E, -1, nxt).astype(jnp.int32)
    ordinal = (jnp.cumsum(used.astype(jnp.int32)) - 1).astype(jnp.int32)
    return (tile, exp, nxt[exp], ordinal[exp] % 2, starts[exp].astype(jnp.int32),
            ends[exp].astype(jnp.int32), total.reshape(1).astype(jnp.int32))


def _combine_kernel(dest_ref, dest_next_ref, x_ref, h_ref, w_ref, ys_ref, wsg_ref, wsu_ref, wsd_ref,
                    g2_ref, fg_ref, o_ref, buf, sem, *, tm, n_tiles, final_norm):
    i = pl.program_id(0)
    slot = i % 2

    def issue(dref, sl):
        def body(t, carry):
            for j in range(TOP_K):
                _row_copy(ys_ref, dref[t * TOP_K + j], buf.at[sl, j], t, sem.at[sl]).start()
            return carry

        lax.fori_loop(0, tm, body, 0)

    @pl.when(i == 0)
    def _():
        issue(dest_ref, 0)

    @pl.when(i + 1 < n_tiles)
    def _():
        issue(dest_next_ref, 1 - slot)

    hl, hh = (t.astype(BF16) for t in _unpack_halves(h_ref[...]))
    half = hl.shape[1]
    gate = _dot(hl, wsg_ref[0:half, :]) + _dot(hh, wsg_ref[half:, :])
    up = _dot(hl, wsu_ref[0:half, :]) + _dot(hh, wsu_ref[half:, :])
    shared = _dot((_silu(gate) * up).astype(BF16), wsd_ref[...])
    for j in range(TOP_K):
        pltpu.make_async_copy(ys_ref.at[pl.ds(0, tm), :], buf.at[slot, j], sem.at[slot]).wait()
    w = w_ref[...]
    acc_lo = shared[:, :half]
    acc_hi = shared[:, half:]
    for j in range(TOP_K):
        lo, hi = _unpack_halves(buf[slot, j])
        acc_lo = acc_lo + lo * w[:, j:j + 1]
        acc_hi = acc_hi + hi * w[:, j:j + 1]
    acc = jnp.concatenate([acc_lo, acc_hi], axis=1)
    out = x_ref[...] + g2_ref[0] * acc
    if final_norm:
        ms = jnp.mean(out * out, axis=-1, keepdims=True)
        out = out * lax.rsqrt(ms + RMS_EPS) * fg_ref[...]
    o_ref[...] = out


def _combine_call(dest_flat, x2, h2, w_tok, y_sorted, w_sh_gate, w_sh_up, w_sh_down, g2, final_g,
                  seq, final_norm, tm=256):
    T, D = x2.shape
    n_tiles = T // tm
    const = lambda shape: pl.BlockSpec(shape, lambda i: (0,) * len(shape),
                                       pipeline_mode=pl.Buffered(1))
    return pl.pallas_call(
        functools.partial(_combine_kernel, tm=tm, n_tiles=n_tiles, final_norm=final_norm),
        grid=(n_tiles,),
        in_specs=[pl.BlockSpec((tm * TOP_K,), lambda i: (i,), memory_space=pltpu.SMEM),
                  pl.BlockSpec((tm * TOP_K,), lambda i: (jnp.minimum(i + 1, n_tiles - 1),),
                               memory_space=pltpu.SMEM),
                  pl.BlockSpec((tm, D), lambda i: (i, 0)),
                  pl.BlockSpec((tm, D // 2), lambda i: (i, 0)),
                  pl.BlockSpec((tm, TOP_K), lambda i: (i, 0)),
                  pl.BlockSpec(memory_space=pl.ANY),
                  const(w_sh_gate.shape),
                  const(w_sh_up.shape),
                  const(w_sh_down.shape),
                  pl.BlockSpec((1, 1, D), lambda i: ((i * tm) // seq, 0, 0)),
                  const(final_g.shape)],
        out_specs=pl.BlockSpec((tm, D), lambda i: (i, 0)),
        out_shape=jax.ShapeDtypeStruct((T, D), F32),
        scratch_shapes=[pltpu.VMEM((2, TOP_K, tm, D // 2), jnp.uint32), pltpu.SemaphoreType.DMA((2,))],
        compiler_params=_params(("arbitrary",)),
        name="combine",
    )(dest_flat, dest_flat, x2, h2, w_tok, y_sorted, w_sh_gate, w_sh_up, w_sh_down, g2, final_g)


def _pad_cols(w, n):
    return jnp.pad(w, ((0, 0), (0, n - w.shape[1])))


def _pad_rows(w, n):
    return jnp.pad(w, ((0, n - w.shape[0]), (0, 0)))


def _pack_cols(parts):
    return jnp.concatenate([_pad_cols(a, n) for a, n in parts], axis=1)


def kernel(x, c, ada_w, ada_b, norm1_g, norm2_g, w_in, mu_shift, w_decay_up, decay_base, w_a_up, a_base, w_g_up, k_k, k_a, r_k, lnx_g, lnx_b, w_v_down, mu_v, w_v_up, v_base, pool_w, pool_scale, w_br_rwkv, w_br_pool, w_out, w_router, router_bias, w_exp_gate, w_exp_up, w_exp_down, w_sh_gate, w_sh_up, w_sh_down, final_g):
    B, S, D = x.shape
    T = B * S
    L = ada_w.shape[0]
    DR = w_decay_up.shape[2]
    n_dec, n_a, n_g, n_v = w_decay_up.shape[1], w_a_up.shape[1], w_g_up.shape[1], w_v_down.shape[2]
    o_k, o_v, o_wd = DR, 2 * DR, 3 * DR
    o_ad = o_wd + n_dec
    o_gd = o_ad + n_a
    n_shift = o_gd + n_g
    o_gate = n_shift + DR

    c_pad = jnp.pad(c, ((0, 8 - B), (0, 0)))
    mod = _ada_call(c_pad, ada_w, ada_b)[:, :B]
    mod = mod.reshape(L, B, 6, 1, D)

    x2 = x.reshape(T, D)
    proj_first = None
    for l in range(L):
        sh1, sc1, g1, sh2, sc2, g2 = (mod[l, :, q] for q in range(6))
        has_vres = l > 0
        wl = w_in[l]
        zero_cols = jnp.zeros((D, 0), F32)
        vd_w = w_v_down[l - 1] if has_vres else zero_cols
        w_packed = _pack_cols([
            (wl[:, :o_wd], 3 * DR),
            (wl[:, o_wd:o_ad], PAIR), (wl[:, o_ad:o_gd], PAIR), (wl[:, o_gd:n_shift], 2 * PAIR),
            (vd_w, PAIR), (zero_cols, PROJ_TN - 5 * PAIR),
            (wl[:, o_gate:], 2 * D),
            (wl[:, n_shift:o_gate], DR)]).astype(BF16)
        ml = mu_shift[l][None, :]
        zero_mu = jnp.zeros((1, 0), F32)
        vd_mu = mu_v[l - 1][None, :] if has_vres else zero_mu
        mu_packed = _pack_cols([
            (ml[:, :o_wd], 3 * DR),
            (ml[:, o_wd:o_ad], PAIR), (ml[:, o_ad:o_gd], PAIR), (ml[:, o_gd:n_shift], 2 * PAIR),
            (vd_mu, PAIR), (zero_mu, PROJ_TN - 5 * PAIR), (zero_mu, 2 * D + DR)])
        proj = _proj_call(x2, norm1_g[l][None, :], sc1, sh1, w_packed, mu_packed, S)
        if l == 0:
            proj_first = proj

        lora_w = {"decay": _pad_rows(w_decay_up[l], PAIR), "a": _pad_rows(w_a_up[l], PAIR),
                  "g": w_g_up[l]}
        zeros_dr = jnp.zeros((DR,), F32)
        if has_vres:
            lora_w["v"] = _pad_rows(w_v_up[l - 1], PAIR)
        vecs = jnp.stack([decay_base[l], a_base[l], k_k[l], k_a[l], r_k[l].reshape(DR), lnx_g[l],
                          lnx_b[l], v_base[l - 1] if has_vres else zeros_dr])
        y_rwkv = _rwkv_call(proj.reshape(B, S, -1), proj_first.reshape(B, S, -1), lora_w, vecs,
                            has_vres).reshape(T, DR)

        x2 = _merge_call(x2, y_rwkv, proj, pool_w[l].astype(BF16), pool_scale[l][None, :],
                         w_br_rwkv[l].astype(BF16), w_br_pool[l].astype(BF16),
                         w_out[l].astype(BF16), g1, S)

        h2, gate_t, idx_t = _router_call(x2, norm2_g[l][None, :], sc2, sh2, w_router[l].T,
                                         router_bias[l][:, None], S)
        experts = jnp.arange(N_EXPERTS, dtype=jnp.int32)[None, :, None]
        sel = jnp.any(idx_t[:, None, :] == experts, axis=0).astype(jnp.int32)
        incl = jnp.cumsum(sel, axis=1)
        counts = incl[:, -1]
        start = jnp.cumsum(counts) - counts
        slot = start[:, None] + incl - sel
        dest = jnp.take_along_axis(slot, idx_t, axis=0)
        dest_flat = dest.T.reshape(T * TOP_K).astype(jnp.int32)
        w_tok = jnp.take_along_axis(gate_t, idx_t, axis=0).T

        x_sorted = _dispatch_call(dest_flat, h2)
        meta = _expert_steps(counts, T * TOP_K)
        y_sorted = _expert_call(meta, x_sorted, w_exp_gate, w_exp_up, w_exp_down, l)
        x2 = _combine_call(dest_flat, x2, h2, w_tok, y_sorted, w_sh_gate[l].astype(BF16),
                           w_sh_up[l].astype(BF16), w_sh_down[l].astype(BF16), g2,
                           final_g[None, :], S, final_norm=(l == L - 1))
    return x2.reshape(B, S, D)
```

```python
import functools
import math

import jax
import jax.numpy as jnp
from jax import lax
from jax.experimental import pallas as pl
from jax.experimental.pallas import tpu as pltpu

F32 = jnp.float32
BF16 = jnp.bfloat16

HEAD = 64
PAIR = 2 * HEAD
CHUNK = 64
POOL_WINDOWS = (2, 4, 8, 16)
POOL_HIST = 16
N_EXPERTS = 64
TOP_K = 8
N_GROUPS = 8
TOPK_GROUPS = 4
ROUTED_SCALE = 2.5
GN_EPS = 64e-5
RMS_EPS = 1e-6
NEG_INF = float("-inf")

PROJ_TN = 1024
N_SHIFT_TILES = 4
EXPERT_TILE = 256
VMEM_LIMIT = 56 * 1024 * 1024

_NN = (((1,), (0,)), ((), ()))
_NT = (((1,), (1,)), ((), ()))


def _dot(a, b, dims=_NN):
    return lax.dot_general(a, b, dims, preferred_element_type=F32)


def _bdot(a, b, dims=_NN):
    return _dot(a.astype(BF16), b.astype(BF16), dims)


def _split(a):
    hi = a.astype(BF16)
    lo = (a - hi.astype(F32)).astype(BF16)
    return hi, lo


def _dot3s(ap, bp, dims=_NN):
    (ah, al), (bh, bl) = ap, bp
    return _dot(ah, bh, dims) + (_dot(ah, bl, dims) + _dot(al, bh, dims))


def _dot3(a, b, dims=_NN):
    return _dot3s(_split(a), _split(b), dims)


def _dot2c(a, c_bf16, dims=_NN):
    ah, al = _split(a)
    return _dot(ah, c_bf16, dims) + _dot(al, c_bf16, dims)


def _sigmoid(x):
    return 1.0 / (1.0 + jnp.exp(-x))


def _silu(x):
    return x * _sigmoid(x)


def _rms_mod(x, g, sc, sh):
    ms = jnp.mean(x * x, axis=-1, keepdims=True)
    return (x * lax.rsqrt(ms + RMS_EPS) * g) * (1.0 + sc) + sh


HI16 = 0xFFFF0000


def _pack_halves(x):
    n = x.shape[1] // 2
    bits = lax.bitcast_convert_type(x.astype(BF16).astype(F32), jnp.uint32)
    return (bits[:, n:] & jnp.uint32(HI16)) | (bits[:, :n] >> 16)


def _unpack_halves(p):
    lo = lax.bitcast_convert_type(p << 16, F32)
    hi = lax.bitcast_convert_type(p & jnp.uint32(HI16), F32)
    return lo, hi


def _params(sem):
    return pltpu.CompilerParams(dimension_semantics=sem, vmem_limit_bytes=VMEM_LIMIT)


def _ada_kernel(c_ref, w_ref, b_ref, o_ref):
    cond = _silu(c_ref[...])
    o_ref[0] = _bdot(cond, w_ref[0]) + b_ref[0]


def _ada_call(c_pad, ada_w, ada_b):
    L, D, N6 = ada_w.shape
    tn = 1024
    return pl.pallas_call(
        _ada_kernel,
        grid=(L, N6 // tn),
        in_specs=[pl.BlockSpec(c_pad.shape, lambda l, j: (0, 0)),
                  pl.BlockSpec((1, D, tn), lambda l, j: (l, 0, j)),
                  pl.BlockSpec((1, 1, tn), lambda l, j: (l, 0, j))],
        out_specs=pl.BlockSpec((1, c_pad.shape[0], tn), lambda l, j: (l, 0, j)),
        out_shape=jax.ShapeDtypeStruct((L, c_pad.shape[0], N6), F32),
        compiler_params=_params(("arbitrary", "arbitrary")),
        name="adaln",
    )(c_pad, ada_w, ada_b.reshape(L, 1, N6))


def _proj_kernel(x_ref, g_ref, sc_ref, sh_ref, w_ref, mu_ref, o_ref, h_scr, prev_scr, *, tm, seq):
    i = pl.program_id(0)
    j = pl.program_id(1)

    @pl.when(j == 0)
    def _():
        h_scr[...] = _rms_mod(x_ref[...], g_ref[...], sc_ref[0], sh_ref[0]).astype(BF16)

    p = _dot(h_scr[...], w_ref[...])

    @pl.when(j < N_SHIFT_TILES)
    def _():
        seq_start = (i * tm) % seq == 0
        carry = jnp.where(seq_start, 0.0, prev_scr[j])
        row = lax.broadcasted_iota(jnp.int32, p.shape, 0)
        prev = jnp.where(row == 0, carry, pltpu.roll(p, 1, 0))
        o_ref[...] = (p + (prev - p) * mu_ref[...]).astype(o_ref.dtype)
        prev_scr[j] = p[tm - 1:tm, :]

    @pl.when(j >= N_SHIFT_TILES)
    def _():
        o_ref[...] = p.astype(o_ref.dtype)


def _proj_call(x2, g, sc, sh, w_packed, mu_packed, seq, tm=1024):
    T, D = x2.shape
    NP = w_packed.shape[1]
    nj = NP // PROJ_TN
    bmap = lambda i, j: ((i * tm) // seq, 0, 0)
    return pl.pallas_call(
        functools.partial(_proj_kernel, tm=tm, seq=seq),
        grid=(T // tm, nj),
        in_specs=[pl.BlockSpec((tm, D), lambda i, j: (i, 0)),
                  pl.BlockSpec((1, D), lambda i, j: (0, 0)),
                  pl.BlockSpec((1, 1, D), bmap),
                  pl.BlockSpec((1, 1, D), bmap),
                  pl.BlockSpec((D, PROJ_TN), lambda i, j: (0, j)),
                  pl.BlockSpec((1, PROJ_TN), lambda i, j: (0, j))],
        out_specs=pl.BlockSpec((tm, PROJ_TN), lambda i, j: (i, j)),
        out_shape=jax.ShapeDtypeStruct((T, NP), BF16),
        scratch_shapes=[pltpu.VMEM((tm, D), BF16),
                        pltpu.VMEM((N_SHIFT_TILES, 1, PROJ_TN), F32)],
        compiler_params=_params(("arbitrary", "arbitrary")),
        name="proj",
    )(x2, g, sc, sh, w_packed, mu_packed)


RWKV_PASSES = {"lora": 1, "gram": 1, "inv": 1, "apply": 1}
RWKV_BATCH = 4
RWKV_PAIRS = 2
RWKV_ROWS = 512


def _splitp(a, passes):
    hi = a.astype(BF16)
    if passes == 1:
        return hi, None
    return hi, (a - hi.astype(F32)).astype(BF16)


def _mm(ap, bp, passes, dims=_NN):
    (ah, al), (bh, bl) = ap, bp
    out = _dot(ah, bh, dims)
    if passes == 1:
        return out
    lo = None
    if bl is not None:
        lo = _dot(ah, bl, dims)
    if al is not None:
        t = _dot(al, bh, dims)
        lo = t if lo is None else lo + t
    return out if lo is None else out + lo


def _stack(z, lane_lo):
    return jnp.concatenate([jnp.where(lane_lo, z, 0.0), jnp.where(lane_lo, 0.0, z)], axis=0)


def _rwkv_kernel(*refs, has_vres, nb, npl, rg):
    if has_vres:
        (r_ref, k_ref, v_ref, wd_ref, ad_ref, gd_ref, vd_ref, vf_ref,
         wdu_ref, wau_ref, wgu_ref, wvu_ref, vec_ref, y_ref, st_ref) = refs
    else:
        (r_ref, k_ref, v_ref, wd_ref, ad_ref, gd_ref,
         wdu_ref, wau_ref, wgu_ref, vec_ref, y_ref, st_ref) = refs
    p_lora, p_gram, p_inv, p_app = (RWKV_PASSES[s] for s in ("lora", "gram", "inv", "apply"))
    p_in = max(p_gram, p_app)
    C = CHUNK
    C2 = 2 * C

    @pl.when(pl.program_id(2) == 0)
    def _():
        st_ref[...] = jnp.zeros_like(st_ref)

    lane_lo = lax.broadcasted_iota(jnp.int32, (C, PAIR), 1) < HEAD
    ri = lax.broadcasted_iota(jnp.int32, (C2, C2), 0)
    ci = lax.broadcasted_iota(jnp.int32, (C2, C2), 1)
    strict = ri > ci
    incl = ri >= ci
    eye = jnp.where(ri == ci, 1.0, 0.0)
    wl = npl * PAIR
    ones_head = jnp.where(lax.broadcasted_iota(jnp.int32, (wl, wl), 0) // HEAD
                          == lax.broadcasted_iota(jnp.int32, (wl, wl), 1) // HEAD,
                          1.0, 0.0).astype(BF16)
    rb = lax.broadcasted_iota(jnp.int32, (nb * C, nb * C), 0)
    cb = lax.broadcasted_iota(jnp.int32, (nb * C, nb * C), 1)
    same_seq = (rb // C) == (cb // C)
    tri_blk = jnp.where(same_seq & (rb >= cb), 1.0, 0.0).astype(BF16)
    ones_blk = jnp.where(same_seq, 1.0, 0.0).astype(BF16)

    vec = vec_ref[...]
    decay_base, a_base, k_k, k_a = vec[0:1], vec[1:2], vec[2:3], vec[3:4]
    r_k, lnx_g, lnx_b, v_base = vec[4:5], vec[5:6], vec[6:7], vec[7:8]
    wdu = _splitp(wdu_ref[...], p_lora)
    wau = _splitp(wau_ref[...], p_lora)
    wgu = _splitp(wgu_ref[...], p_lora)
    if has_vres:
        wvu = _splitp(wvu_ref[...], p_lora)

    chains = [(b, q) for b in range(nb) for q in range(npl)]
    seqs = range(len(chains))

    def chunk_body(c, carry):
        rows = pl.ds(pl.multiple_of(c * C, C), C)
        cat = lambda ref: jnp.concatenate([ref[b, rows, :].astype(F32) for b in range(nb)], axis=0)

        def part(x, i):
            b, q = chains[i]
            return x[b * C:(b + 1) * C, q * PAIR:(q + 1) * PAIR]

        r, k, v = cat(r_ref), cat(k_ref), cat(v_ref)
        dl = decay_base + _mm(_splitp(jnp.tanh(cat(wd_ref)), p_lora), wdu, p_lora)
        z = -dl
        softplus = jnp.maximum(z, 0.0) + jnp.log(1.0 + jnp.exp(-jnp.abs(z)))
        lw = -jnp.exp(-softplus - 0.5)
        a = _sigmoid(a_base + _mm(_splitp(cat(ad_ref), p_lora), wau, p_lora))
        g = _mm(_splitp(_sigmoid(cat(gd_ref)), p_lora), wgu, p_lora)
        if has_vres:
            mix = _sigmoid(v_base + _mm(_splitp(cat(vd_ref), p_lora), wvu, p_lora))
            v = v + (cat(vf_ref) - v) * mix
        kkr = k * k_k
        kk = kkr / jnp.maximum(jnp.sqrt(_dot2c(kkr * kkr, ones_head)), 1e-12)
        k2 = k * (1.0 + (a - 1.0) * k_a)
        avec = -kk
        bvec = kk * a

        lwh, lwl = _split(lw)
        cl = _dot(tri_blk, lwh) + _dot(tri_blk, lwl)
        tot = _dot(ones_blk, lwh) + _dot(ones_blk, lwl)
        w_inv = jnp.exp(-cl)
        w_tail = jnp.exp(tot - cl)
        w_tot = jnp.exp(tot)
        a_t = avec * jnp.exp(cl - lw)
        r_t = r * jnp.exp(cl)
        b_t = bvec * w_inv
        k_t = k2 * w_inv
        b_h = bvec * w_tail
        k_h = k2 * w_tail

        pair2 = lambda x, y, b: jnp.concatenate([_stack(part(x, b), lane_lo),
                                                 _stack(part(y, b), lane_lo)], axis=0)
        lhs = [_splitp(pair2(a_t, r_t, b), p_in) for b in seqs]
        rhs = [_splitp(pair2(b_t, k_t, b), p_gram) for b in seqs]
        gram = [_mm(lhs[b], rhs[b], p_gram, _NT) for b in seqs]
        a_ab = [jnp.where(strict, gm[:C2, :C2], 0.0) for gm in gram]
        a_ak = [jnp.where(strict, gm[:C2, C2:], 0.0) for gm in gram]
        a_r = [jnp.concatenate([jnp.where(incl, gm[C2:, :C2], 0.0),
                                jnp.where(incl, gm[C2:, C2:], 0.0)], axis=1) for gm in gram]

        pw = a_ab
        inv = [eye + m for m in a_ab]
        for _ in range(int(math.log2(C)) - 1):
            pws = [_splitp(m, p_inv) for m in pw]
            pw = [_mm(m, m, p_inv) for m in pws]
            inv = [inv[b] + _mm(_splitp(inv[b], p_inv), _splitp(pw[b], p_inv), p_inv) for b in seqs]

        s_prev = [st_ref[b] for b in seqs]
        sx = [_mm(lhs[b], _splitp(s_prev[b], p_app), p_app, _NT) for b in seqs]
        v_st = [_stack(part(v, b), lane_lo) for b in seqs]
        x_st = [sx[b][:C2] + _mm(_splitp(a_ak[b], p_app), _splitp(v_st[b], p_app), p_app) for b in seqs]
        u_st = [_mm(_splitp(inv[b], p_app), _splitp(x_st[b], p_app), p_app) for b in seqs]
        uv = [jnp.concatenate([u_st[b], v_st[b]], axis=0) for b in seqs]
        y_st = [sx[b][C2:] + _mm(_splitp(a_r[b], p_app), _splitp(uv[b], p_app), p_app) for b in seqs]
        for b in seqs:
            bk = _splitp(pair2(b_h, k_h, b), p_app)
            st_ref[b] = (s_prev[b] * part(w_tot, b)[0:1]
                         + _mm(_splitp(uv[b].T, p_app), bk, p_app))
        y_ch = [ys[:C] + ys[C:] for ys in y_st]
        y = jnp.concatenate([jnp.concatenate(y_ch[b * npl:(b + 1) * npl], axis=1)
                             for b in range(nb)], axis=0)

        inv_n = 1.0 / HEAD
        mu = _dot2c(y, ones_head) * inv_n
        yc = y - mu
        var = _dot2c(yc * yc, ones_head) * inv_n
        yn = yc * lax.rsqrt(var + GN_EPS) * lnx_g + lnx_b
        bonus = _dot2c(r * k2 * r_k, ones_head) * v
        out = (yn + bonus) * g
        for b in range(nb):
            y_ref[b, rows, :] = out[b * C:(b + 1) * C]
        return carry

    lax.fori_loop(0, rg // C, chunk_body, 0)


def _rwkv_call(proj, proj_first, lora_w, vecs, has_vres):
    B, S, _ = proj.shape
    nb, rg, npl = min(RWKV_BATCH, B), min(RWKV_ROWS, S), RWKV_PAIRS
    dr = vecs.shape[1]
    wl = npl * PAIR
    n_blk = dr // wl
    col = lambda off: (lambda bb, p, t: (bb, t, off * n_blk + p))
    fixed = lambda blk: (lambda bb, p, t: (bb, t, blk))
    blk = lambda w, imap: pl.BlockSpec((nb, rg, w), imap)
    lora0 = 3 * dr // PAIR
    in_specs = [blk(wl, col(0)),
                blk(wl, col(1)),
                blk(wl, col(2)),
                blk(PAIR, fixed(lora0)),
                blk(PAIR, fixed(lora0 + 1)),
                blk(2 * PAIR, fixed(lora0 // 2 + 1))]
    args = [proj] * 6
    if has_vres:
        in_specs += [blk(PAIR, fixed(lora0 + 4)),
                     blk(wl, col(2))]
        args += [proj, proj_first]
    wspec = lambda rows: pl.BlockSpec((rows, wl), lambda bb, p, t: (0, p))
    in_specs += [wspec(PAIR), wspec(PAIR), wspec(2 * PAIR)]
    args += [lora_w["decay"], lora_w["a"], lora_w["g"]]
    if has_vres:
        in_specs += [wspec(PAIR)]
        args += [lora_w["v"]]
    in_specs += [pl.BlockSpec((8, wl), lambda bb, p, t: (0, p))]
    args += [vecs]
    return pl.pallas_call(
        functools.partial(_rwkv_kernel, has_vres=has_vres, nb=nb, npl=npl, rg=rg),
        grid=(B // nb, n_blk, S // rg),
        in_specs=in_specs,
        out_specs=pl.BlockSpec((nb, rg, wl), lambda bb, p, t: (bb, t, p)),
        out_shape=jax.ShapeDtypeStruct((B, S, dr), F32),
        scratch_shapes=[pltpu.VMEM((nb * npl, PAIR, PAIR), F32)],
        compiler_params=_params(("arbitrary", "arbitrary", "arbitrary")),
        name="rwkv7",
    )(*args)


def _merge_kernel(x_ref, y_ref, pool_ref, gr_ref, gp_ref, pw_ref, ps_ref, wbr_ref, wbp_ref, wo_ref,
                  g1_ref, o_ref, hist_ref, *, tm, seq):
    i = pl.program_id(0)
    seq_off = (i * tm) % seq
    p = pool_ref[...].astype(F32)
    hist = jnp.where(seq_off == 0, 0.0, hist_ref[...])
    ext = jnp.concatenate([hist, p], axis=0)
    pos = (seq_off + 1 + lax.broadcasted_iota(jnp.int32, (tm, 1), 0)).astype(F32)
    gw = p.shape[1] // len(POOL_WINDOWS)
    zs = []
    for gi, win in enumerate(POOL_WINDOWS):
        e = ext[:, gi * gw:(gi + 1) * gw]
        s = e
        shift = 1
        while shift < win:
            n = s.shape[0]
            s = s[shift:, :] + s[:n - shift, :]
            shift *= 2
        lo = POOL_HIST + 1 - win
        wsum = s[lo:lo + tm, :]
        u = wsum / jnp.minimum(pos, float(win)) - p[:, gi * gw:(gi + 1) * gw]
        zs.append(_bdot(u, pw_ref[gi]))
    y_pool = jnp.concatenate(zs, axis=1) * ps_ref[...]
    hist_ref[...] = p[tm - POOL_HIST:, :]

    br = _bdot(y_ref[...], wbr_ref[...])
    bp = _bdot(y_pool, wbp_ref[...])
    merged = _sigmoid(gr_ref[...].astype(F32)) * br + _sigmoid(gp_ref[...].astype(F32)) * bp
    o_ref[...] = x_ref[...] + g1_ref[0] * _bdot(merged, wo_ref[...])


def _merge_call(x2, y_rwkv, proj, pool_w, pool_scale, w_br_rwkv, w_br_pool, w_out, g1, seq, tm=256):
    T, D = x2.shape
    DR = y_rwkv.shape[1]
    const = lambda shape: pl.BlockSpec(shape, lambda i: (0,) * len(shape),
                                       pipeline_mode=pl.Buffered(1))
    return pl.pallas_call(
        functools.partial(_merge_kernel, tm=tm, seq=seq),
        grid=(T // tm,),
        in_specs=[pl.BlockSpec((tm, D), lambda i: (i, 0)),
                  pl.BlockSpec((tm, DR), lambda i: (i, 0)),
                  pl.BlockSpec((tm, DR), lambda i: (i, 8)),
                  pl.BlockSpec((tm, D), lambda i: (i, 2)),
                  pl.BlockSpec((tm, D), lambda i: (i, 3)),
                  const(pool_w.shape),
                  const(pool_scale.shape),
                  const(w_br_rwkv.shape),
                  const(w_br_pool.shape),
                  const(w_out.shape),
                  pl.BlockSpec((1, 1, D), lambda i: ((i * tm) // seq, 0, 0))],
        out_specs=pl.BlockSpec((tm, D), lambda i: (i, 0)),
        out_shape=jax.ShapeDtypeStruct((T, D), F32),
        scratch_shapes=[pltpu.VMEM((POOL_HIST, DR), F32)],
        compiler_params=_params(("arbitrary",)),
        name="merge",
    )(x2, y_rwkv, proj, proj, proj, pool_w, pool_scale, w_br_rwkv, w_br_pool, w_out, g1)


def _first_argmax(cur, idx, axis):
    m = jnp.max(cur, axis=axis, keepdims=True)
    first = jnp.min(jnp.where(cur == m, idx, 1 << 20), axis=axis, keepdims=True)
    return m, first


def _router_kernel(x_ref, g_ref, sc_ref, sh_ref, wr_ref, bias_ref, h_ref, gt_ref, idx_ref, *, tm):
    h = _rms_mod(x_ref[...], g_ref[...], sc_ref[0], sh_ref[0])
    h_ref[...] = _pack_halves(h)
    scores = _sigmoid(_dot3(wr_ref[...], h, _NT))
    choice = scores + bias_ref[...]
    per = N_EXPERTS // N_GROUPS
    ch3 = choice.reshape(N_GROUPS, per, tm)
    within = lax.broadcasted_iota(jnp.int32, ch3.shape, 1)
    m1, i1 = _first_argmax(ch3, within, 1)
    m2 = jnp.max(jnp.where(within == i1, NEG_INF, ch3), axis=1, keepdims=True)
    grp = (m1 + m2).reshape(N_GROUPS, tm)
    gidx = lax.broadcasted_iota(jnp.int32, grp.shape, 0)
    gsel = jnp.zeros(grp.shape, F32)
    for _ in range(TOPK_GROUPS):
        _, gi = _first_argmax(grp, gidx, 0)
        pick = gidx == gi
        gsel = jnp.where(pick, 1.0, gsel)
        grp = jnp.where(pick, NEG_INF, grp)
    gsel3 = jnp.broadcast_to(gsel.reshape(N_GROUPS, 1, tm), ch3.shape)
    cur = jnp.where(gsel3 > 0.5, ch3, NEG_INF).reshape(N_EXPERTS, tm)
    eidx = lax.broadcasted_iota(jnp.int32, cur.shape, 0)
    sel = jnp.zeros(cur.shape, F32)
    picks = []
    for _ in range(TOP_K):
        _, ei = _first_argmax(cur, eidx, 0)
        pick = eidx == ei
        sel = jnp.where(pick, 1.0, sel)
        cur = jnp.where(pick, NEG_INF, cur)
        picks.append(ei)
    w = jnp.where(sel > 0.5, scores, 0.0)
    gate = w / jnp.sum(w, axis=0, keepdims=True) * ROUTED_SCALE
    gt_ref[...] = jnp.concatenate(
        [jnp.sum(jnp.where(eidx == ei, gate, 0.0), axis=0, keepdims=True) for ei in picks], axis=0)
    idx_ref[...] = jnp.concatenate(picks, axis=0)


def _router_call(x2, g, sc, sh, w_router_t, bias_col, seq, tm=256):
    T, D = x2.shape
    bmap = lambda i: ((i * tm) // seq, 0, 0)
    return pl.pallas_call(
        functools.partial(_router_kernel, tm=tm),
        grid=(T // tm,),
        in_specs=[pl.BlockSpec((tm, D), lambda i: (i, 0)),
                  pl.BlockSpec((1, D), lambda i: (0, 0)),
                  pl.BlockSpec((1, 1, D), bmap),
                  pl.BlockSpec((1, 1, D), bmap),
                  pl.BlockSpec((N_EXPERTS, D), lambda i: (0, 0)),
                  pl.BlockSpec((N_EXPERTS, 1), lambda i: (0, 0))],
        out_specs=[pl.BlockSpec((tm, D // 2), lambda i: (i, 0)),
                   pl.BlockSpec((TOP_K, tm), lambda i: (0, i)),
                   pl.BlockSpec((TOP_K, tm), lambda i: (0, i))],
        out_shape=[jax.ShapeDtypeStruct((T, D // 2), jnp.uint32),
                   jax.ShapeDtypeStruct((TOP_K, T), F32),
                   jax.ShapeDtypeStruct((TOP_K, T), jnp.int32)],
        compiler_params=_params(("arbitrary",)),
        name="router",
    )(x2, g, sc, sh, w_router_t, bias_col)


def _row_copy(src, src_row, dst, dst_row, sem):
    return pltpu.make_async_copy(src.at[pl.ds(src_row, 1), :], dst.at[pl.ds(dst_row, 1), :], sem)


def _dispatch_kernel(dest_ref, h_ref, xs_ref, sem, *, tm):
    def issue(t, carry):
        for j in range(TOP_K):
            _row_copy(h_ref, t, xs_ref, dest_ref[t * TOP_K + j], sem).start(priority=j % 2)
        return carry

    lax.fori_loop(0, tm, issue, 0)
    for _ in range(TOP_K):
        pltpu.make_async_copy(h_ref, xs_ref.at[pl.ds(0, tm), :], sem).wait()


def _dispatch_call(dest_flat, h2, tm=256):
    T, D = h2.shape
    return pl.pallas_call(
        functools.partial(_dispatch_kernel, tm=tm),
        grid=(T // tm,),
        in_specs=[pl.BlockSpec((tm * TOP_K,), lambda i: (i,), memory_space=pltpu.SMEM),
                  pl.BlockSpec((tm, D), lambda i: (i, 0))],
        out_specs=pl.BlockSpec(memory_space=pl.ANY),
        out_shape=jax.ShapeDtypeStruct((T * TOP_K, D), h2.dtype),
        scratch_shapes=[pltpu.SemaphoreType.DMA(())],
        compiler_params=_params(("arbitrary",)),
        name="dispatch",
    )(dest_flat, h2)


def _expert_kernel(tile_ref, exp_ref, nxt_ref, slot_ref, lo_ref, hi_ref, n_ref, x_ref, wg_hbm, wu_hbm, wd_hbm,
                   o_ref, wg_st, wu_st, wd_st, wg_bf, wu_bf, wd_bf, sem, *, te, layer):
    s = pl.program_id(0)

    def weight_copies(e, sl):
        return [pltpu.make_async_copy(src.at[layer, e], dst.at[sl], sem.at[k, sl])
                for k, (src, dst) in enumerate(((wg_hbm, wg_st), (wu_hbm, wu_st), (wd_hbm, wd_st)))]

    @pl.when(s < n_ref[0])
    def _():
        tile = tile_ref[s]
        prev = jnp.maximum(s - 1, 0)
        e = exp_ref[s]
        sl = slot_ref[s]

        @pl.when(s == 0)
        def _():
            for cp in weight_copies(e, sl):
                cp.start()

        @pl.when((s == 0) | (exp_ref[prev] != e))
        def _():
            for cp in weight_copies(e, sl):
                cp.wait()
            wg_bf[...] = wg_st[sl].astype(BF16)
            wu_bf[...] = wu_st[sl].astype(BF16)
            wd_bf[...] = wd_st[sl].astype(BF16)

            @pl.when(nxt_ref[s] >= 0)
            def _():
                for cp in weight_copies(nxt_ref[s], 1 - sl):
                    cp.start()

        xl, xh = (t.astype(BF16) for t in _unpack_halves(x_ref[...]))
        half = xl.shape[1]
        gate = _dot(xl, wg_bf[0:half, :]) + _dot(xh, wg_bf[half:, :])
        up = _dot(xl, wu_bf[0:half, :]) + _dot(xh, wu_bf[half:, :])
        y = _pack_halves(_dot((_silu(gate) * up).astype(BF16), wd_bf[...]))
        row0 = tile * te
        whole = (lo_ref[s] <= row0) & (hi_ref[s] >= row0 + te)
        first_visit = (s == 0) | (tile_ref[prev] != tile)
        rows = row0 + lax.broadcasted_iota(jnp.int32, (te, 1), 0)
        mine = (rows >= lo_ref[s]) & (rows < hi_ref[s])

        @pl.when(whole)
        def _():
            o_ref[...] = y

        @pl.when(jnp.logical_not(whole) & first_visit)
        def _():
            o_ref[...] = jnp.where(mine, y, jnp.uint32(0))

        @pl.when(jnp.logical_not(whole) & jnp.logical_not(first_visit))
        def _():
            o_ref[...] = jnp.where(mine, y, o_ref[...])


def _expert_call(meta, x_sorted, w_gate, w_up, w_down, layer):
    A, DH = x_sorted.shape
    _, E, D, DE = w_gate.shape
    te = EXPERT_TILE
    n_steps = A // te + E - 1
    hbm = pl.BlockSpec(memory_space=pl.ANY)
    grid_spec = pltpu.PrefetchScalarGridSpec(
        num_scalar_prefetch=7,
        grid=(n_steps,),
        in_specs=[pl.BlockSpec((te, DH), lambda s, tile, *_: (tile[s], 0)), hbm, hbm, hbm],
        out_specs=pl.BlockSpec((te, DH), lambda s, tile, *_: (tile[s], 0)),
        scratch_shapes=[pltpu.VMEM((2, D, DE), F32), pltpu.VMEM((2, D, DE), F32), pltpu.VMEM((2, DE, D), F32),
                        pltpu.VMEM((D, DE), BF16), pltpu.VMEM((D, DE), BF16), pltpu.VMEM((DE, D), BF16),
                        pltpu.SemaphoreType.DMA((3, 2))],
    )
    return pl.pallas_call(
        functools.partial(_expert_kernel, te=te, layer=layer),
        grid_spec=grid_spec,
        out_shape=jax.ShapeDtypeStruct((A, DH), jnp.uint32),
        compiler_params=_params(("arbitrary",)),
        name="experts",
    )(*meta, x_sorted, w_gate, w_up, w_down)


def _expert_steps(counts, n_rows):
    te = EXPERT_TILE
    E = counts.shape[0]
    n_steps = n_rows // te + E - 1
    ends = jnp.cumsum(counts)
    starts = ends - counts
    first_tile = starts // te
    last_tile = jnp.maximum(ends - 1, 0) // te
    n_tiles = jnp.where(counts > 0, last_tile - first_tile + 1, 0)
    step_end = jnp.cumsum(n_tiles)
    step_start = step_end - n_tiles
    total = step_end[-1]
    s = jnp.arange(n_steps, dtype=jnp.int32)
    s_eff = jnp.minimum(s, total - 1)
    exp = jnp.sum(step_end[None, :] <= s_eff[:, None], axis=1, dtype=jnp.int32)
    ids = jnp.arange(E, dtype=jnp.int32)
    used = counts > 0
    at_or_after = jnp.flip(lax.cummin(jnp.flip(jnp.where(used, ids, E))))
    nxt = jnp.concatenate([at_or_after[1:], jnp.full((1,), E, jnp.int32)])
    nxt = jnp.where(nxt >= E, -1, nxt)
    ordinal = jnp.cumsum(used.astype(jnp.int32)) - 1
    is_exp = exp[:, None] == ids[None, :]
    of_step = lambda v: jnp.sum(jnp.where(is_exp, v[None, :], 0), axis=1, dtype=jnp.int32)
    tile = of_step(first_tile) + s_eff - of_step(step_start)
    return (tile, exp, of_step(nxt), of_step(ordinal) % 2, of_step(starts), of_step(ends),
            total.reshape(1).astype(jnp.int32))


def _combine_kernel(dest_ref, dest_next_ref, x_ref, h_ref, w_ref, ys_ref, wsg_ref, wsu_ref, wsd_ref,
                    g2_ref, fg_ref, o_ref, buf, sem, *, tm, n_tiles, final_norm):
    i = pl.program_id(0)
    slot = i % 2

    def issue(dref, sl):
        def body(t, carry):
            for j in range(TOP_K):
                _row_copy(ys_ref, dref[t * TOP_K + j], buf.at[sl, j], t, sem.at[sl]).start(priority=j % 2)
            return carry

        lax.fori_loop(0, tm, body, 0)

    @pl.when(i == 0)
    def _():
        issue(dest_ref, 0)

    @pl.when(i + 1 < n_tiles)
    def _():
        issue(dest_next_ref, 1 - slot)

    hl, hh = (t.astype(BF16) for t in _unpack_halves(h_ref[...]))
    half = hl.shape[1]
    gate = _dot(hl, wsg_ref[0:half, :]) + _dot(hh, wsg_ref[half:, :])
    up = _dot(hl, wsu_ref[0:half, :]) + _dot(hh, wsu_ref[half:, :])
    shared = _dot((_silu(gate) * up).astype(BF16), wsd_ref[...])
    for j in range(TOP_K):
        pltpu.make_async_copy(ys_ref.at[pl.ds(0, tm), :], buf.at[slot, j], sem.at[slot]).wait()
    w = w_ref[...]
    acc_lo = shared[:, :half]
    acc_hi = shared[:, half:]
    for j in range(TOP_K):
        lo, hi = _unpack_halves(buf[slot, j])
        acc_lo = acc_lo + lo * w[:, j:j + 1]
        acc_hi = acc_hi + hi * w[:, j:j + 1]
    acc = jnp.concatenate([acc_lo, acc_hi], axis=1)
    out = x_ref[...] + g2_ref[0] * acc
    if final_norm:
        ms = jnp.mean(out * out, axis=-1, keepdims=True)
        out = out * lax.rsqrt(ms + RMS_EPS) * fg_ref[...]
    o_ref[...] = out


def _combine_call(dest_flat, x2, h2, w_tok, y_sorted, w_sh_gate, w_sh_up, w_sh_down, g2, final_g,
                  seq, final_norm, tm=256):
    T, D = x2.shape
    n_tiles = T // tm
    const = lambda shape: pl.BlockSpec(shape, lambda i: (0,) * len(shape),
                                       pipeline_mode=pl.Buffered(1))
    return pl.pallas_call(
        functools.partial(_combine_kernel, tm=tm, n_tiles=n_tiles, final_norm=final_norm),
        grid=(n_tiles,),
        in_specs=[pl.BlockSpec((tm * TOP_K,), lambda i: (i,), memory_space=pltpu.SMEM),
                  pl.BlockSpec((tm * TOP_K,), lambda i: (jnp.minimum(i + 1, n_tiles - 1),),
                               memory_space=pltpu.SMEM),
                  pl.BlockSpec((tm, D), lambda i: (i, 0)),
                  pl.BlockSpec((tm, D // 2), lambda i: (i, 0)),
                  pl.BlockSpec((tm, TOP_K), lambda i: (i, 0)),
                  pl.BlockSpec(memory_space=pl.ANY),
                  const(w_sh_gate.shape),
                  const(w_sh_up.shape),
                  const(w_sh_down.shape),
                  pl.BlockSpec((1, 1, D), lambda i: ((i * tm) // seq, 0, 0)),
                  const(final_g.shape)],
        out_specs=pl.BlockSpec((tm, D), lambda i: (i, 0)),
        out_shape=jax.ShapeDtypeStruct((T, D), F32),
        scratch_shapes=[pltpu.VMEM((2, TOP_K, tm, D // 2), jnp.uint32), pltpu.SemaphoreType.DMA((2,))],
        compiler_params=_params(("arbitrary",)),
        name="combine",
    )(dest_flat, dest_flat, x2, h2, w_tok, y_sorted, w_sh_gate, w_sh_up, w_sh_down, g2, final_g)


def _pad_cols(w, n):
    return jnp.pad(w, ((0, 0), (0, n - w.shape[1])))


def _pad_rows(w, n):
    return jnp.pad(w, ((0, n - w.shape[0]), (0, 0)))


def _pack_cols(parts):
    return jnp.concatenate([_pad_cols(a, n) for a, n in parts], axis=1)


def kernel(x, c, ada_w, ada_b, norm1_g, norm2_g, w_in, mu_shift, w_decay_up, decay_base, w_a_up, a_base, w_g_up, k_k, k_a, r_k, lnx_g, lnx_b, w_v_down, mu_v, w_v_up, v_base, pool_w, pool_scale, w_br_rwkv, w_br_pool, w_out, w_router, router_bias, w_exp_gate, w_exp_up, w_exp_down, w_sh_gate, w_sh_up, w_sh_down, final_g):
    B, S, D = x.shape
    T = B * S
    L = ada_w.shape[0]
    DR = w_decay_up.shape[2]
    n_dec, n_a, n_g, n_v = w_decay_up.shape[1], w_a_up.shape[1], w_g_up.shape[1], w_v_down.shape[2]
    o_k, o_v, o_wd = DR, 2 * DR, 3 * DR
    o_ad = o_wd + n_dec
    o_gd = o_ad + n_a
    n_shift = o_gd + n_g
    o_gate = n_shift + DR

    c_pad = jnp.pad(c, ((0, 8 - B), (0, 0)))
    mod = _ada_call(c_pad, ada_w, ada_b)[:, :B]
    mod = mod.reshape(L, B, 6, 1, D)

    x2 = x.reshape(T, D)
    proj_first = None
    for l in range(L):
        sh1, sc1, g1, sh2, sc2, g2 = (mod[l, :, q] for q in range(6))
        has_vres = l > 0
        wl = w_in[l]
        zero_cols = jnp.zeros((D, 0), F32)
        vd_w = w_v_down[l - 1] if has_vres else zero_cols
        w_packed = _pack_cols([
            (wl[:, :o_wd], 3 * DR),
            (wl[:, o_wd:o_ad], PAIR), (wl[:, o_ad:o_gd], PAIR), (wl[:, o_gd:n_shift], 2 * PAIR),
            (vd_w, PAIR), (zero_cols, PROJ_TN - 5 * PAIR),
            (wl[:, o_gate:], 2 * D),
            (wl[:, n_shift:o_gate], DR)]).astype(BF16)
        ml = mu_shift[l][None, :]
        zero_mu = jnp.zeros((1, 0), F32)
        vd_mu = mu_v[l - 1][None, :] if has_vres else zero_mu
        mu_packed = _pack_cols([
            (ml[:, :o_wd], 3 * DR),
            (ml[:, o_wd:o_ad], PAIR), (ml[:, o_ad:o_gd], PAIR), (ml[:, o_gd:n_shift], 2 * PAIR),
            (vd_mu, PAIR), (zero_mu, PROJ_TN - 5 * PAIR), (zero_mu, 2 * D + DR)])
        proj = _proj_call(x2, norm1_g[l][None, :], sc1, sh1, w_packed, mu_packed, S)
        if l == 0:
            proj_first = proj

        lora_w = {"decay": _pad_rows(w_decay_up[l], PAIR), "a": _pad_rows(w_a_up[l], PAIR),
                  "g": w_g_up[l]}
        zeros_dr = jnp.zeros((DR,), F32)
        if has_vres:
            lora_w["v"] = _pad_rows(w_v_up[l - 1], PAIR)
        vecs = jnp.stack([decay_base[l], a_base[l], k_k[l], k_a[l], r_k[l].reshape(DR), lnx_g[l],
                          lnx_b[l], v_base[l - 1] if has_vres else zeros_dr])
        y_rwkv = _rwkv_call(proj.reshape(B, S, -1), proj_first.reshape(B, S, -1), lora_w, vecs,
                            has_vres).reshape(T, DR)

        x2 = _merge_call(x2, y_rwkv, proj, pool_w[l].astype(BF16), pool_scale[l][None, :],
                         w_br_rwkv[l].astype(BF16), w_br_pool[l].astype(BF16),
                         w_out[l].astype(BF16), g1, S)

        h2, w_t, idx_t = _router_call(x2, norm2_g[l][None, :], sc2, sh2, w_router[l].T,
                                      router_bias[l][:, None], S)
        experts = jnp.arange(N_EXPERTS, dtype=jnp.int32)[None, :, None]
        sel = jnp.any(idx_t[:, None, :] == experts, axis=0).astype(jnp.int32)
        incl = jnp.cumsum(sel, axis=1)
        counts = incl[:, -1]
        start = jnp.cumsum(counts) - counts
        slot = start[:, None] + incl - sel
        dest = jnp.take_along_axis(slot, idx_t, axis=0)
        dest_flat = dest.T.reshape(T * TOP_K).astype(jnp.int32)
        w_tok = w_t.T

        x_sorted = _dispatch_call(dest_flat, h2)
        meta = _expert_steps(counts, T * TOP_K)
        y_sorted = _expert_call(meta, x_sorted, w_exp_gate, w_exp_up, w_exp_down, l)
        x2 = _combine_call(dest_flat, x2, h2, w_tok, y_sorted, w_sh_gate[l].astype(BF16),
                           w_sh_up[l].astype(BF16), w_sh_down[l].astype(BF16), g2,
                           final_g[None, :], S, final_norm=(l == L - 1))
    return x2.reshape(B, S, D)
```

```python
import functools
import math

import jax
import jax.numpy as jnp
from jax import lax
from jax.experimental import pallas as pl
from jax.experimental.pallas import tpu as pltpu

F32 = jnp.float32
BF16 = jnp.bfloat16

HEAD = 64
PAIR = 2 * HEAD
CHUNK = 64
POOL_WINDOWS = (2, 4, 8, 16)
POOL_HIST = 16
N_EXPERTS = 64
TOP_K = 8
N_GROUPS = 8
TOPK_GROUPS = 4
ROUTED_SCALE = 2.5
GN_EPS = 64e-5
RMS_EPS = 1e-6
NEG_INF = float("-inf")

PROJ_TN = 1024
N_SHIFT_TILES = 4
EXPERT_TILE = 256
VMEM_LIMIT = 56 * 1024 * 1024

_NN = (((1,), (0,)), ((), ()))
_NT = (((1,), (1,)), ((), ()))


def _dot(a, b, dims=_NN):
    return lax.dot_general(a, b, dims, preferred_element_type=F32)


def _bdot(a, b, dims=_NN):
    return _dot(a.astype(BF16), b.astype(BF16), dims)


def _split(a):
    hi = a.astype(BF16)
    lo = (a - hi.astype(F32)).astype(BF16)
    return hi, lo


def _dot3s(ap, bp, dims=_NN):
    (ah, al), (bh, bl) = ap, bp
    return _dot(ah, bh, dims) + (_dot(ah, bl, dims) + _dot(al, bh, dims))


def _dot3(a, b, dims=_NN):
    return _dot3s(_split(a), _split(b), dims)


def _sigmoid(x):
    return 1.0 / (1.0 + jnp.exp(-x))


def _silu(x):
    return x * _sigmoid(x)


def _rms_mod(x, g, sc, sh):
    ms = jnp.mean(x * x, axis=-1, keepdims=True)
    return (x * lax.rsqrt(ms + RMS_EPS) * g) * (1.0 + sc) + sh


HI16 = 0xFFFF0000


def _pack_halves(x):
    n = x.shape[1] // 2
    bits = lax.bitcast_convert_type(x.astype(BF16).astype(F32), jnp.uint32)
    return (bits[:, n:] & jnp.uint32(HI16)) | (bits[:, :n] >> 16)


def _unpack_halves(p):
    lo = lax.bitcast_convert_type(p << 16, F32)
    hi = lax.bitcast_convert_type(p & jnp.uint32(HI16), F32)
    return lo, hi


def _params(sem):
    return pltpu.CompilerParams(dimension_semantics=sem, vmem_limit_bytes=VMEM_LIMIT)


def _ada_kernel(c_ref, w_ref, b_ref, o_ref):
    cond = _silu(c_ref[...])
    o_ref[0] = _bdot(cond, w_ref[0]) + b_ref[0]


def _ada_call(c_pad, ada_w, ada_b):
    L, D, N6 = ada_w.shape
    tn = 1024
    return pl.pallas_call(
        _ada_kernel,
        grid=(L, N6 // tn),
        in_specs=[pl.BlockSpec(c_pad.shape, lambda l, j: (0, 0)),
                  pl.BlockSpec((1, D, tn), lambda l, j: (l, 0, j)),
                  pl.BlockSpec((1, 1, tn), lambda l, j: (l, 0, j))],
        out_specs=pl.BlockSpec((1, c_pad.shape[0], tn), lambda l, j: (l, 0, j)),
        out_shape=jax.ShapeDtypeStruct((L, c_pad.shape[0], N6), F32),
        compiler_params=_params(("arbitrary", "arbitrary")),
        name="adaln",
    )(c_pad, ada_w, ada_b.reshape(L, 1, N6))


def _proj_kernel(x_ref, g_ref, sc_ref, sh_ref, w_ref, mu_ref, o_ref, h_scr, prev_scr, *, tm, seq):
    i = pl.program_id(0)
    j = pl.program_id(1)

    @pl.when(j == 0)
    def _():
        h_scr[...] = _rms_mod(x_ref[...], g_ref[...], sc_ref[0], sh_ref[0]).astype(BF16)

    p = _dot(h_scr[...], w_ref[...])

    @pl.when(j < N_SHIFT_TILES)
    def _():
        seq_start = (i * tm) % seq == 0
        carry = jnp.where(seq_start, 0.0, prev_scr[j])
        row = lax.broadcasted_iota(jnp.int32, p.shape, 0)
        prev = jnp.where(row == 0, carry, pltpu.roll(p, 1, 0))
        o_ref[...] = (p + (prev - p) * mu_ref[...]).astype(o_ref.dtype)
        prev_scr[j] = p[tm - 1:tm, :]

    @pl.when(j >= N_SHIFT_TILES)
    def _():
        o_ref[...] = p.astype(o_ref.dtype)


def _proj_call(x2, g, sc, sh, w_packed, mu_packed, seq, tm=1024):
    T, D = x2.shape
    NP = w_packed.shape[1]
    nj = NP // PROJ_TN
    bmap = lambda i, j: ((i * tm) // seq, 0, 0)
    return pl.pallas_call(
        functools.partial(_proj_kernel, tm=tm, seq=seq),
        grid=(T // tm, nj),
        in_specs=[pl.BlockSpec((tm, D), lambda i, j: (i, 0)),
                  pl.BlockSpec((1, D), lambda i, j: (0, 0)),
                  pl.BlockSpec((1, 1, D), bmap),
                  pl.BlockSpec((1, 1, D), bmap),
                  pl.BlockSpec((D, PROJ_TN), lambda i, j: (0, j)),
                  pl.BlockSpec((1, PROJ_TN), lambda i, j: (0, j))],
        out_specs=pl.BlockSpec((tm, PROJ_TN), lambda i, j: (i, j)),
        out_shape=jax.ShapeDtypeStruct((T, NP), BF16),
        scratch_shapes=[pltpu.VMEM((tm, D), BF16),
                        pltpu.VMEM((N_SHIFT_TILES, 1, PROJ_TN), F32)],
        compiler_params=_params(("arbitrary", "arbitrary")),
        name="proj",
    )(x2, g, sc, sh, w_packed, mu_packed)


RWKV_PASSES = {"lora": 1, "gram": 1, "inv": 1, "apply": 1, "norm": 1}
RWKV_BATCH = 4
RWKV_PAIRS = 4
RWKV_ROWS = 512


def _splitp(a, passes):
    hi = a.astype(BF16)
    if passes == 1:
        return hi, None
    return hi, (a - hi.astype(F32)).astype(BF16)


def _mm(ap, bp, passes, dims=_NN):
    (ah, al), (bh, bl) = ap, bp
    out = _dot(ah, bh, dims)
    if passes == 1:
        return out
    lo = None
    if bl is not None:
        lo = _dot(ah, bl, dims)
    if al is not None:
        t = _dot(al, bh, dims)
        lo = t if lo is None else lo + t
    return out if lo is None else out + lo


def _stack(z, lane_lo):
    return jnp.concatenate([jnp.where(lane_lo, z, 0.0), jnp.where(lane_lo, 0.0, z)], axis=0)


def _rwkv_kernel(*refs, has_vres, nb, npl, rg):
    if has_vres:
        (r_ref, k_ref, v_ref, wd_ref, ad_ref, gd_ref, vd_ref, vf_ref,
         wdu_ref, wau_ref, wgu_ref, wvu_ref, vec_ref, y_ref, st_ref) = refs
    else:
        (r_ref, k_ref, v_ref, wd_ref, ad_ref, gd_ref,
         wdu_ref, wau_ref, wgu_ref, vec_ref, y_ref, st_ref) = refs
    p_lora, p_gram, p_inv, p_app, p_norm = (RWKV_PASSES[s] for s in ("lora", "gram", "inv", "apply", "norm"))
    p_in = max(p_gram, p_app)
    C = CHUNK
    C2 = 2 * C

    @pl.when(pl.program_id(2) == 0)
    def _():
        st_ref[...] = jnp.zeros_like(st_ref)

    lane_lo = lax.broadcasted_iota(jnp.int32, (C, PAIR), 1) < HEAD
    ri = lax.broadcasted_iota(jnp.int32, (C2, C2), 0)
    ci = lax.broadcasted_iota(jnp.int32, (C2, C2), 1)
    strict = ri > ci
    incl = ri >= ci
    eye = jnp.where(ri == ci, 1.0, 0.0)
    wl = npl * PAIR
    ones_pair = jnp.where((ri < HEAD) == (ci < HEAD), 1.0, 0.0).astype(BF16)

    def head_sum(x):
        return jnp.concatenate(
            [_mm(_splitp(x[:, q * PAIR:(q + 1) * PAIR], p_norm), (ones_pair, None), p_norm)
             for q in range(npl)], axis=1)

    rb = lax.broadcasted_iota(jnp.int32, (nb * C, nb * C), 0)
    cb = lax.broadcasted_iota(jnp.int32, (nb * C, nb * C), 1)
    tri_blk = jnp.where(((rb // C) == (cb // C)) & (rb >= cb), 1.0, 0.0).astype(BF16)

    vec = vec_ref[...]
    decay_base, a_base, k_k, k_a = vec[0:1], vec[1:2], vec[2:3], vec[3:4]
    r_k, lnx_g, lnx_b, v_base = vec[4:5], vec[5:6], vec[6:7], vec[7:8]
    wdu = _splitp(wdu_ref[...], p_lora)
    wau = _splitp(wau_ref[...], p_lora)
    wgu = _splitp(wgu_ref[...], p_lora)
    if has_vres:
        wvu = _splitp(wvu_ref[...], p_lora)

    chains = [(b, q) for b in range(nb) for q in range(npl)]
    seqs = range(len(chains))

    def chunk_body(c, carry):
        rows = pl.ds(pl.multiple_of(c * C, C), C)
        cat = lambda ref: jnp.concatenate([ref[b, rows, :].astype(F32) for b in range(nb)], axis=0)

        def part(x, i):
            b, q = chains[i]
            return x[b * C:(b + 1) * C, q * PAIR:(q + 1) * PAIR]

        r, k, v = cat(r_ref), cat(k_ref), cat(v_ref)
        dl = decay_base + _mm(_splitp(jnp.tanh(cat(wd_ref)), p_lora), wdu, p_lora)
        z = -dl
        softplus = jnp.maximum(z, 0.0) + jnp.log(1.0 + jnp.exp(-jnp.abs(z)))
        lw = -jnp.exp(-softplus - 0.5)
        a = _sigmoid(a_base + _mm(_splitp(cat(ad_ref), p_lora), wau, p_lora))
        g = _mm(_splitp(_sigmoid(cat(gd_ref)), p_lora), wgu, p_lora)
        if has_vres:
            mix = _sigmoid(v_base + _mm(_splitp(cat(vd_ref), p_lora), wvu, p_lora))
            v = v + (cat(vf_ref) - v) * mix
        kkr = k * k_k
        kk = kkr / jnp.maximum(jnp.sqrt(head_sum(kkr * kkr)), 1e-12)
        k2 = k * (1.0 + (a - 1.0) * k_a)
        avec = -kk
        bvec = kk * a

        lwh, lwl = _split(lw)
        cl = _dot(tri_blk, lwh) + _dot(tri_blk, lwl)
        tot = jnp.concatenate([jnp.broadcast_to(cl[(b + 1) * C - 1:(b + 1) * C], (C, wl))
                               for b in range(nb)], axis=0)
        w_inv = jnp.exp(-cl)
        w_tail = jnp.exp(tot - cl)
        w_tot = jnp.exp(tot)
        a_t = avec * jnp.exp(cl - lw)
        r_t = r * jnp.exp(cl)
        b_t = bvec * w_inv
        k_t = k2 * w_inv
        b_h = bvec * w_tail
        k_h = k2 * w_tail

        pair2 = lambda x, y, b: jnp.concatenate([_stack(part(x, b), lane_lo),
                                                 _stack(part(y, b), lane_lo)], axis=0)
        lhs = [_splitp(pair2(a_t, r_t, b), p_in) for b in seqs]
        rhs = [_splitp(pair2(b_t, k_t, b), p_gram) for b in seqs]
        gram = [_mm(lhs[b], rhs[b], p_gram, _NT) for b in seqs]
        a_ab = [jnp.where(strict, gm[:C2, :C2], 0.0) for gm in gram]
        a_ak = [jnp.where(strict, gm[:C2, C2:], 0.0) for gm in gram]
        a_r = [jnp.concatenate([jnp.where(incl, gm[C2:, :C2], 0.0),
                                jnp.where(incl, gm[C2:, C2:], 0.0)], axis=1) for gm in gram]

        pw = a_ab
        inv = [eye + m for m in a_ab]
        for _ in range(int(math.log2(C)) - 1):
            pws = [_splitp(m, p_inv) for m in pw]
            pw = [_mm(m, m, p_inv) for m in pws]
            inv = [inv[b] + _mm(_splitp(inv[b], p_inv), _splitp(pw[b], p_inv), p_inv) for b in seqs]

        s_prev = [st_ref[b] for b in seqs]
        sx = [_mm(lhs[b], _splitp(s_prev[b], p_app), p_app, _NT) for b in seqs]
        v_st = [_stack(part(v, b), lane_lo) for b in seqs]
        x_st = [sx[b][:C2] + _mm(_splitp(a_ak[b], p_app), _splitp(v_st[b], p_app), p_app) for b in seqs]
        u_st = [_mm(_splitp(inv[b], p_app), _splitp(x_st[b], p_app), p_app) for b in seqs]
        uv = [jnp.concatenate([u_st[b], v_st[b]], axis=0) for b in seqs]
        y_st = [sx[b][C2:] + _mm(_splitp(a_r[b], p_app), _splitp(uv[b], p_app), p_app) for b in seqs]
        for b in seqs:
            bk = _splitp(pair2(b_h, k_h, b), p_app)
            st_ref[b] = (s_prev[b] * part(w_tot, b)[0:1]
                         + _mm(_splitp(uv[b].T, p_app), bk, p_app))
        y_ch = [ys[:C] + ys[C:] for ys in y_st]
        y = jnp.concatenate([jnp.concatenate(y_ch[b * npl:(b + 1) * npl], axis=1)
                             for b in range(nb)], axis=0)

        inv_n = 1.0 / HEAD
        mu = head_sum(y) * inv_n
        yc = y - mu
        var = head_sum(yc * yc) * inv_n
        yn = yc * lax.rsqrt(var + GN_EPS) * lnx_g + lnx_b
        bonus = head_sum(r * k2 * r_k) * v
        out = (yn + bonus) * g
        for b in range(nb):
            y_ref[b, rows, :] = out[b * C:(b + 1) * C]
        return carry

    lax.fori_loop(0, rg // C, chunk_body, 0)


def _rwkv_call(proj, proj_first, lora_w, vecs, has_vres):
    B, S, _ = proj.shape
    nb, rg, npl = min(RWKV_BATCH, B), min(RWKV_ROWS, S), RWKV_PAIRS
    dr = vecs.shape[1]
    wl = npl * PAIR
    n_blk = dr // wl
    col = lambda off: (lambda bb, p, t: (bb, t, off * n_blk + p))
    fixed = lambda blk: (lambda bb, p, t: (bb, t, blk))
    blk = lambda w, imap: pl.BlockSpec((nb, rg, w), imap)
    lora0 = 3 * dr // PAIR
    in_specs = [blk(wl, col(0)),
                blk(wl, col(1)),
                blk(wl, col(2)),
                blk(PAIR, fixed(lora0)),
                blk(PAIR, fixed(lora0 + 1)),
                blk(2 * PAIR, fixed(lora0 // 2 + 1))]
    args = [proj] * 6
    if has_vres:
        in_specs += [blk(PAIR, fixed(lora0 + 4)),
                     blk(wl, col(2))]
        args += [proj, proj_first]
    wspec = lambda rows: pl.BlockSpec((rows, wl), lambda bb, p, t: (0, p))
    in_specs += [wspec(PAIR), wspec(PAIR), wspec(2 * PAIR)]
    args += [lora_w["decay"], lora_w["a"], lora_w["g"]]
    if has_vres:
        in_specs += [wspec(PAIR)]
        args += [lora_w["v"]]
    in_specs += [pl.BlockSpec((8, wl), lambda bb, p, t: (0, p))]
    args += [vecs]
    return pl.pallas_call(
        functools.partial(_rwkv_kernel, has_vres=has_vres, nb=nb, npl=npl, rg=rg),
        grid=(B // nb, n_blk, S // rg),
        in_specs=in_specs,
        out_specs=pl.BlockSpec((nb, rg, wl), lambda bb, p, t: (bb, t, p)),
        out_shape=jax.ShapeDtypeStruct((B, S, dr), F32),
        scratch_shapes=[pltpu.VMEM((nb * npl, PAIR, PAIR), F32)],
        compiler_params=_params(("arbitrary", "arbitrary", "arbitrary")),
        name="rwkv7",
    )(*args)


def _merge_kernel(x_ref, y_ref, pool_ref, gr_ref, gp_ref, pw_ref, ps_ref, wbr_ref, wbp_ref, wo_ref,
                  g1_ref, o_ref, hist_ref, *, tm, seq):
    i = pl.program_id(0)
    seq_off = (i * tm) % seq
    p = pool_ref[...].astype(F32)
    hist = jnp.where(seq_off == 0, 0.0, hist_ref[...])
    ext = jnp.concatenate([hist, p], axis=0)
    pos = (seq_off + 1 + lax.broadcasted_iota(jnp.int32, (tm, 1), 0)).astype(F32)
    gw = p.shape[1] // len(POOL_WINDOWS)
    zs = []
    for gi, win in enumerate(POOL_WINDOWS):
        e = ext[:, gi * gw:(gi + 1) * gw]
        s = e
        shift = 1
        while shift < win:
            n = s.shape[0]
            s = s[shift:, :] + s[:n - shift, :]
            shift *= 2
        lo = POOL_HIST + 1 - win
        wsum = s[lo:lo + tm, :]
        u = wsum / jnp.minimum(pos, float(win)) - p[:, gi * gw:(gi + 1) * gw]
        zs.append(_bdot(u, pw_ref[gi]))
    y_pool = jnp.concatenate(zs, axis=1) * ps_ref[...]
    hist_ref[...] = p[tm - POOL_HIST:, :]

    br = _bdot(y_ref[...], wbr_ref[...])
    bp = _bdot(y_pool, wbp_ref[...])
    merged = _sigmoid(gr_ref[...].astype(F32)) * br + _sigmoid(gp_ref[...].astype(F32)) * bp
    o_ref[...] = x_ref[...] + g1_ref[0] * _bdot(merged, wo_ref[...])


def _merge_call(x2, y_rwkv, proj, pool_w, pool_scale, w_br_rwkv, w_br_pool, w_out, g1, seq, tm=256):
    T, D = x2.shape
    DR = y_rwkv.shape[1]
    const = lambda shape: pl.BlockSpec(shape, lambda i: (0,) * len(shape),
                                       pipeline_mode=pl.Buffered(1))
    return pl.pallas_call(
        functools.partial(_merge_kernel, tm=tm, seq=seq),
        grid=(T // tm,),
        in_specs=[pl.BlockSpec((tm, D), lambda i: (i, 0)),
                  pl.BlockSpec((tm, DR), lambda i: (i, 0)),
                  pl.BlockSpec((tm, DR), lambda i: (i, 8)),
                  pl.BlockSpec((tm, D), lambda i: (i, 2)),
                  pl.BlockSpec((tm, D), lambda i: (i, 3)),
                  const(pool_w.shape),
                  const(pool_scale.shape),
                  const(w_br_rwkv.shape),
                  const(w_br_pool.shape),
                  const(w_out.shape),
                  pl.BlockSpec((1, 1, D), lambda i: ((i * tm) // seq, 0, 0))],
        out_specs=pl.BlockSpec((tm, D), lambda i: (i, 0)),
        out_shape=jax.ShapeDtypeStruct((T, D), F32),
        scratch_shapes=[pltpu.VMEM((POOL_HIST, DR), F32)],
        compiler_params=_params(("arbitrary",)),
        name="merge",
    )(x2, y_rwkv, proj, proj, proj, pool_w, pool_scale, w_br_rwkv, w_br_pool, w_out, g1)


def _first_argmax(cur, idx, axis):
    m = jnp.max(cur, axis=axis, keepdims=True)
    first = jnp.min(jnp.where(cur == m, idx, 1 << 20), axis=axis, keepdims=True)
    return m, first


def _router_kernel(x_ref, g_ref, sc_ref, sh_ref, wr_ref, bias_ref, h_ref, gt_ref, idx_ref, *, tm):
    h = _rms_mod(x_ref[...], g_ref[...], sc_ref[0], sh_ref[0])
    h_ref[...] = _pack_halves(h)
    scores = _sigmoid(_dot3(wr_ref[...], h, _NT))
    choice = scores + bias_ref[...]
    per = N_EXPERTS // N_GROUPS
    ch3 = choice.reshape(N_GROUPS, per, tm)
    within = lax.broadcasted_iota(jnp.int32, ch3.shape, 1)
    m1, i1 = _first_argmax(ch3, within, 1)
    m2 = jnp.max(jnp.where(within == i1, NEG_INF, ch3), axis=1, keepdims=True)
    grp = (m1 + m2).reshape(N_GROUPS, tm)
    gidx = lax.broadcasted_iota(jnp.int32, grp.shape, 0)
    gsel = jnp.zeros(grp.shape, F32)
    for _ in range(TOPK_GROUPS):
        _, gi = _first_argmax(grp, gidx, 0)
        pick = gidx == gi
        gsel = jnp.where(pick, 1.0, gsel)
        grp = jnp.where(pick, NEG_INF, grp)
    gsel3 = jnp.broadcast_to(gsel.reshape(N_GROUPS, 1, tm), ch3.shape)
    cur = jnp.where(gsel3 > 0.5, ch3, NEG_INF).reshape(N_EXPERTS, tm)
    eidx = lax.broadcasted_iota(jnp.int32, cur.shape, 0)
    sel = jnp.zeros(cur.shape, F32)
    picks = []
    for _ in range(TOP_K):
        _, ei = _first_argmax(cur, eidx, 0)
        pick = eidx == ei
        sel = jnp.where(pick, 1.0, sel)
        cur = jnp.where(pick, NEG_INF, cur)
        picks.append(ei)
    w = jnp.where(sel > 0.5, scores, 0.0)
    gate = w / jnp.sum(w, axis=0, keepdims=True) * ROUTED_SCALE
    gt_ref[...] = jnp.concatenate(
        [jnp.sum(jnp.where(eidx == ei, gate, 0.0), axis=0, keepdims=True) for ei in picks], axis=0)
    idx_ref[...] = jnp.concatenate(picks, axis=0)


def _router_call(x2, g, sc, sh, w_router_t, bias_col, seq, tm=256):
    T, D = x2.shape
    bmap = lambda i: ((i * tm) // seq, 0, 0)
    return pl.pallas_call(
        functools.partial(_router_kernel, tm=tm),
        grid=(T // tm,),
        in_specs=[pl.BlockSpec((tm, D), lambda i: (i, 0)),
                  pl.BlockSpec((1, D), lambda i: (0, 0)),
                  pl.BlockSpec((1, 1, D), bmap),
                  pl.BlockSpec((1, 1, D), bmap),
                  pl.BlockSpec((N_EXPERTS, D), lambda i: (0, 0)),
                  pl.BlockSpec((N_EXPERTS, 1), lambda i: (0, 0))],
        out_specs=[pl.BlockSpec((tm, D // 2), lambda i: (i, 0)),
                   pl.BlockSpec((TOP_K, tm), lambda i: (0, i)),
                   pl.BlockSpec((TOP_K, tm), lambda i: (0, i))],
        out_shape=[jax.ShapeDtypeStruct((T, D // 2), jnp.uint32),
                   jax.ShapeDtypeStruct((TOP_K, T), F32),
                   jax.ShapeDtypeStruct((TOP_K, T), jnp.int32)],
        compiler_params=_params(("arbitrary",)),
        name="router",
    )(x2, g, sc, sh, w_router_t, bias_col)


def _row_copy(src, src_row, dst, dst_row, sem):
    return pltpu.make_async_copy(src.at[pl.ds(src_row, 1), :], dst.at[pl.ds(dst_row, 1), :], sem)


def _dispatch_kernel(dest_ref, h_ref, xs_ref, sem, *, tm):
    def issue(t, carry):
        for j in range(TOP_K):
            _row_copy(h_ref, t, xs_ref, dest_ref[t * TOP_K + j], sem).start(priority=j % 2)
        return carry

    lax.fori_loop(0, tm, issue, 0)
    for _ in range(TOP_K):
        pltpu.make_async_copy(h_ref, xs_ref.at[pl.ds(0, tm), :], sem).wait()


def _dispatch_call(dest_flat, h2, tm=256):
    T, D = h2.shape
    return pl.pallas_call(
        functools.partial(_dispatch_kernel, tm=tm),
        grid=(T // tm,),
        in_specs=[pl.BlockSpec((tm * TOP_K,), lambda i: (i,), memory_space=pltpu.SMEM),
                  pl.BlockSpec((tm, D), lambda i: (i, 0))],
        out_specs=pl.BlockSpec(memory_space=pl.ANY),
        out_shape=jax.ShapeDtypeStruct((T * TOP_K, D), h2.dtype),
        scratch_shapes=[pltpu.SemaphoreType.DMA(())],
        compiler_params=_params(("arbitrary",)),
        name="dispatch",
    )(dest_flat, h2)


def _expert_kernel(tile_ref, exp_ref, nxt_ref, slot_ref, lo_ref, hi_ref, n_ref, x_ref, wg_hbm, wu_hbm, wd_hbm,
                   o_ref, wg_st, wu_st, wd_st, wg_bf, wu_bf, wd_bf, sem, *, te, layer):
    s = pl.program_id(0)

    def weight_copies(e, sl):
        return [pltpu.make_async_copy(src.at[layer, e], dst.at[sl], sem.at[k, sl])
                for k, (src, dst) in enumerate(((wg_hbm, wg_st), (wu_hbm, wu_st), (wd_hbm, wd_st)))]

    @pl.when(s < n_ref[0])
    def _():
        tile = tile_ref[s]
        prev = jnp.maximum(s - 1, 0)
        e = exp_ref[s]
        sl = slot_ref[s]

        @pl.when(s == 0)
        def _():
            for cp in weight_copies(e, sl):
                cp.start()

        @pl.when((s == 0) | (exp_ref[prev] != e))
        def _():
            for cp in weight_copies(e, sl):
                cp.wait()
            wg_bf[...] = wg_st[sl].astype(BF16)
            wu_bf[...] = wu_st[sl].astype(BF16)
            wd_bf[...] = wd_st[sl].astype(BF16)

            @pl.when(nxt_ref[s] >= 0)
            def _():
                for cp in weight_copies(nxt_ref[s], 1 - sl):
                    cp.start()

        xl, xh = (t.astype(BF16) for t in _unpack_halves(x_ref[...]))
        half = xl.shape[1]
        gate = _dot(xl, wg_bf[0:half, :]) + _dot(xh, wg_bf[half:, :])
        up = _dot(xl, wu_bf[0:half, :]) + _dot(xh, wu_bf[half:, :])
        y = _pack_halves(_dot((_silu(gate) * up).astype(BF16), wd_bf[...]))
        row0 = tile * te
        whole = (lo_ref[s] <= row0) & (hi_ref[s] >= row0 + te)
        first_visit = (s == 0) | (tile_ref[prev] != tile)
        rows = row0 + lax.broadcasted_iota(jnp.int32, (te, 1), 0)
        mine = (rows >= lo_ref[s]) & (rows < hi_ref[s])

        @pl.when(whole)
        def _():
            o_ref[...] = y

        @pl.when(jnp.logical_not(whole) & first_visit)
        def _():
            o_ref[...] = jnp.where(mine, y, jnp.uint32(0))

        @pl.when(jnp.logical_not(whole) & jnp.logical_not(first_visit))
        def _():
            o_ref[...] = jnp.where(mine, y, o_ref[...])


def _expert_call(meta, x_sorted, w_gate, w_up, w_down, layer):
    A, DH = x_sorted.shape
    _, E, D, DE = w_gate.shape
    te = EXPERT_TILE
    n_steps = A // te + E - 1
    hbm = pl.BlockSpec(memory_space=pl.ANY)
    grid_spec = pltpu.PrefetchScalarGridSpec(
        num_scalar_prefetch=7,
        grid=(n_steps,),
        in_specs=[pl.BlockSpec((te, DH), lambda s, tile, *_: (tile[s], 0)), hbm, hbm, hbm],
        out_specs=pl.BlockSpec((te, DH), lambda s, tile, *_: (tile[s], 0)),
        scratch_shapes=[pltpu.VMEM((2, D, DE), F32), pltpu.VMEM((2, D, DE), F32), pltpu.VMEM((2, DE, D), F32),
                        pltpu.VMEM((D, DE), BF16), pltpu.VMEM((D, DE), BF16), pltpu.VMEM((DE, D), BF16),
                        pltpu.SemaphoreType.DMA((3, 2))],
    )
    return pl.pallas_call(
        functools.partial(_expert_kernel, te=te, layer=layer),
        grid_spec=grid_spec,
        out_shape=jax.ShapeDtypeStruct((A, DH), jnp.uint32),
        compiler_params=_params(("arbitrary",)),
        name="experts",
    )(*meta, x_sorted, w_gate, w_up, w_down)


def _expert_steps(counts, n_rows):
    te = EXPERT_TILE
    E = counts.shape[0]
    n_steps = n_rows // te + E - 1
    ends = jnp.cumsum(counts)
    starts = ends - counts
    first_tile = starts // te
    last_tile = jnp.maximum(ends - 1, 0) // te
    n_tiles = jnp.where(counts > 0, last_tile - first_tile + 1, 0)
    step_end = jnp.cumsum(n_tiles)
    step_start = step_end - n_tiles
    total = step_end[-1]
    s = jnp.arange(n_steps, dtype=jnp.int32)
    s_eff = jnp.minimum(s, total - 1)
    exp = jnp.sum(step_end[None, :] <= s_eff[:, None], axis=1, dtype=jnp.int32)
    ids = jnp.arange(E, dtype=jnp.int32)
    used = counts > 0
    at_or_after = jnp.flip(lax.cummin(jnp.flip(jnp.where(used, ids, E))))
    nxt = jnp.concatenate([at_or_after[1:], jnp.full((1,), E, jnp.int32)])
    nxt = jnp.where(nxt >= E, -1, nxt)
    ordinal = jnp.cumsum(used.astype(jnp.int32)) - 1
    is_exp = exp[:, None] == ids[None, :]
    of_step = lambda v: jnp.sum(jnp.where(is_exp, v[None, :], 0), axis=1, dtype=jnp.int32)
    tile = of_step(first_tile) + s_eff - of_step(step_start)
    return (tile, exp, of_step(nxt), of_step(ordinal) % 2, of_step(starts), of_step(ends),
            total.reshape(1).astype(jnp.int32))


def _combine_kernel(dest_ref, dest_next_ref, x_ref, h_ref, w_ref, ys_ref, wsg_ref, wsu_ref, wsd_ref,
                    g2_ref, fg_ref, o_ref, buf, sem, *, tm, n_tiles, final_norm):
    i = pl.program_id(0)
    slot = i % 2

    def issue(dref, sl):
        def body(t, carry):
            for j in range(TOP_K):
                _row_copy(ys_ref, dref[t * TOP_K + j], buf.at[sl, j], t, sem.at[sl]).start(priority=j % 2)
            return carry

        lax.fori_loop(0, tm, body, 0)

    @pl.when(i == 0)
    def _():
        issue(dest_ref, 0)

    @pl.when(i + 1 < n_tiles)
    def _():
        issue(dest_next_ref, 1 - slot)

    hl, hh = (t.astype(BF16) for t in _unpack_halves(h_ref[...]))
    half = hl.shape[1]
    gate = _dot(hl, wsg_ref[0:half, :]) + _dot(hh, wsg_ref[half:, :])
    up = _dot(hl, wsu_ref[0:half, :]) + _dot(hh, wsu_ref[half:, :])
    shared = _dot((_silu(gate) * up).astype(BF16), wsd_ref[...])
    for j in range(TOP_K):
        pltpu.make_async_copy(ys_ref.at[pl.ds(0, tm), :], buf.at[slot, j], sem.at[slot]).wait()
    w = w_ref[...]
    acc_lo = shared[:, :half]
    acc_hi = shared[:, half:]
    for j in range(TOP_K):
        lo, hi = _unpack_halves(buf[slot, j])
        acc_lo = acc_lo + lo * w[:, j:j + 1]
        acc_hi = acc_hi + hi * w[:, j:j + 1]
    acc = jnp.concatenate([acc_lo, acc_hi], axis=1)
    out = x_ref[...] + g2_ref[0] * acc
    if final_norm:
        ms = jnp.mean(out * out, axis=-1, keepdims=True)
        out = out * lax.rsqrt(ms + RMS_EPS) * fg_ref[...]
    o_ref[...] = out


def _combine_call(dest_flat, x2, h2, w_tok, y_sorted, w_sh_gate, w_sh_up, w_sh_down, g2, final_g,
                  seq, final_norm, tm=256):
    T, D = x2.shape
    n_tiles = T // tm
    const = lambda shape: pl.BlockSpec(shape, lambda i: (0,) * len(shape),
                                       pipeline_mode=pl.Buffered(1))
    return pl.pallas_call(
        functools.partial(_combine_kernel, tm=tm, n_tiles=n_tiles, final_norm=final_norm),
        grid=(n_tiles,),
        in_specs=[pl.BlockSpec((tm * TOP_K,), lambda i: (i,), memory_space=pltpu.SMEM),
                  pl.BlockSpec((tm * TOP_K,), lambda i: (jnp.minimum(i + 1, n_tiles - 1),),
                               memory_space=pltpu.SMEM),
                  pl.BlockSpec((tm, D), lambda i: (i, 0)),
                  pl.BlockSpec((tm, D // 2), lambda i: (i, 0)),
                  pl.BlockSpec((tm, TOP_K), lambda i: (i, 0)),
                  pl.BlockSpec(memory_space=pl.ANY),
                  const(w_sh_gate.shape),
                  const(w_sh_up.shape),
                  const(w_sh_down.shape),
                  pl.BlockSpec((1, 1, D), lambda i: ((i * tm) // seq, 0, 0)),
                  const(final_g.shape)],
        out_specs=pl.BlockSpec((tm, D), lambda i: (i, 0)),
        out_shape=jax.ShapeDtypeStruct((T, D), F32),
        scratch_shapes=[pltpu.VMEM((2, TOP_K, tm, D // 2), jnp.uint32), pltpu.SemaphoreType.DMA((2,))],
        compiler_params=_params(("arbitrary",)),
        name="combine",
    )(dest_flat, dest_flat, x2, h2, w_tok, y_sorted, w_sh_gate, w_sh_up, w_sh_down, g2, final_g)


def _pad_cols(w, n):
    return jnp.pad(w, ((0, 0), (0, n - w.shape[1])))


def _pad_rows(w, n):
    return jnp.pad(w, ((0, n - w.shape[0]), (0, 0)))


def _pack_cols(parts):
    return jnp.concatenate([_pad_cols(a, n) for a, n in parts], axis=1)


def kernel(x, c, ada_w, ada_b, norm1_g, norm2_g, w_in, mu_shift, w_decay_up, decay_base, w_a_up, a_base, w_g_up, k_k, k_a, r_k, lnx_g, lnx_b, w_v_down, mu_v, w_v_up, v_base, pool_w, pool_scale, w_br_rwkv, w_br_pool, w_out, w_router, router_bias, w_exp_gate, w_exp_up, w_exp_down, w_sh_gate, w_sh_up, w_sh_down, final_g):
    B, S, D = x.shape
    T = B * S
    L = ada_w.shape[0]
    DR = w_decay_up.shape[2]
    n_dec, n_a, n_g, n_v = w_decay_up.shape[1], w_a_up.shape[1], w_g_up.shape[1], w_v_down.shape[2]
    o_k, o_v, o_wd = DR, 2 * DR, 3 * DR
    o_ad = o_wd + n_dec
    o_gd = o_ad + n_a
    n_shift = o_gd + n_g
    o_gate = n_shift + DR

    c_pad = jnp.pad(c, ((0, 8 - B), (0, 0)))
    mod = _ada_call(c_pad, ada_w, ada_b)[:, :B]
    mod = mod.reshape(L, B, 6, 1, D)

    x2 = x.reshape(T, D)
    proj_first = None
    for l in range(L):
        sh1, sc1, g1, sh2, sc2, g2 = (mod[l, :, q] for q in range(6))
        has_vres = l > 0
        wl = w_in[l]
        zero_cols = jnp.zeros((D, 0), F32)
        vd_w = w_v_down[l - 1] if has_vres else zero_cols
        w_packed = _pack_cols([
            (wl[:, :o_wd], 3 * DR),
            (wl[:, o_wd:o_ad], PAIR), (wl[:, o_ad:o_gd], PAIR), (wl[:, o_gd:n_shift], 2 * PAIR),
            (vd_w, PAIR), (zero_cols, PROJ_TN - 5 * PAIR),
            (wl[:, o_gate:], 2 * D),
            (wl[:, n_shift:o_gate], DR)]).astype(BF16)
        ml = mu_shift[l][None, :]
        zero_mu = jnp.zeros((1, 0), F32)
        vd_mu = mu_v[l - 1][None, :] if has_vres else zero_mu
        mu_packed = _pack_cols([
            (ml[:, :o_wd], 3 * DR),
            (ml[:, o_wd:o_ad], PAIR), (ml[:, o_ad:o_gd], PAIR), (ml[:, o_gd:n_shift], 2 * PAIR),
            (vd_mu, PAIR), (zero_mu, PROJ_TN - 5 * PAIR), (zero_mu, 2 * D + DR)])
        proj = _proj_call(x2, norm1_g[l][None, :], sc1, sh1, w_packed, mu_packed, S)
        if l == 0:
            proj_first = proj

        lora_w = {"decay": _pad_rows(w_decay_up[l], PAIR), "a": _pad_rows(w_a_up[l], PAIR),
                  "g": w_g_up[l]}
        zeros_dr = jnp.zeros((DR,), F32)
        if has_vres:
            lora_w["v"] = _pad_rows(w_v_up[l - 1], PAIR)
        vecs = jnp.stack([decay_base[l], a_base[l], k_k[l], k_a[l], r_k[l].reshape(DR), lnx_g[l],
                          lnx_b[l], v_base[l - 1] if has_vres else zeros_dr])
        y_rwkv = _rwkv_call(proj.reshape(B, S, -1), proj_first.reshape(B, S, -1), lora_w, vecs,
                            has_vres).reshape(T, DR)

        x2 = _merge_call(x2, y_rwkv, proj, pool_w[l].astype(BF16), pool_scale[l][None, :],
                         w_br_rwkv[l].astype(BF16), w_br_pool[l].astype(BF16),
                         w_out[l].astype(BF16), g1, S)

        h2, w_t, idx_t = _router_call(x2, norm2_g[l][None, :], sc2, sh2, w_router[l].T,
                                      router_bias[l][:, None], S)
        experts = jnp.arange(N_EXPERTS, dtype=jnp.int32)[None, :, None]
        sel = jnp.any(idx_t[:, None, :] == experts, axis=0).astype(jnp.int32)
        incl = jnp.cumsum(sel, axis=1)
        counts = incl[:, -1]
        start = jnp.cumsum(counts) - counts
        slot = start[:, None] + incl - sel
        dest = jnp.take_along_axis(slot, idx_t, axis=0)
        dest_flat = dest.T.reshape(T * TOP_K).astype(jnp.int32)
        w_tok = w_t.T

        x_sorted = _dispatch_call(dest_flat, h2)
        meta = _expert_steps(counts, T * TOP_K)
        y_sorted = _expert_call(meta, x_sorted, w_exp_gate, w_exp_up, w_exp_down, l)
        x2 = _combine_call(dest_flat, x2, h2, w_tok, y_sorted, w_sh_gate[l].astype(BF16),
                           w_sh_up[l].astype(BF16), w_sh_down[l].astype(BF16), g2,
                           final_g[None, :], S, final_norm=(l == L - 1))
    return x2.reshape(B, S, D)
```

```python
import functools
import math

import jax
import jax.numpy as jnp
from jax import lax
from jax.experimental import pallas as pl
from jax.experimental.pallas import tpu as pltpu

F32 = jnp.float32
BF16 = jnp.bfloat16

HEAD = 64
LANES = 128
PAIR = 2 * HEAD
CHUNK = 64
POOL_WINDOWS = (2, 4, 8, 16)
POOL_HIST = 16
N_EXPERTS = 64
TOP_K = 8
N_GROUPS = 8
TOPK_GROUPS = 4
ROUTED_SCALE = 2.5
GN_EPS = 64e-5
RMS_EPS = 1e-6
NEG_INF = float("-inf")

PROJ_TN = 1024
N_SHIFT_TILES = 4
EXPERT_TILE = 256
VMEM_LIMIT = 56 * 1024 * 1024

_NN = (((1,), (0,)), ((), ()))
_NT = (((1,), (1,)), ((), ()))


def _dot(a, b, dims=_NN):
    return lax.dot_general(a, b, dims, preferred_element_type=F32)


def _bdot(a, b, dims=_NN):
    return _dot(a.astype(BF16), b.astype(BF16), dims)


def _split(a):
    hi = a.astype(BF16)
    lo = (a - hi.astype(F32)).astype(BF16)
    return hi, lo


def _dot3s(ap, bp, dims=_NN):
    (ah, al), (bh, bl) = ap, bp
    return _dot(ah, bh, dims) + (_dot(ah, bl, dims) + _dot(al, bh, dims))


def _dot3(a, b, dims=_NN):
    return _dot3s(_split(a), _split(b), dims)


def _sigmoid(x):
    return 1.0 / (1.0 + jnp.exp(-x))


def _silu(x):
    return x * _sigmoid(x)


def _rms_mod(x, g, sc, sh):
    ms = jnp.mean(x * x, axis=-1, keepdims=True)
    return (x * lax.rsqrt(ms + RMS_EPS) * g) * (1.0 + sc) + sh


HI16 = 0xFFFF0000


def _pack_halves(x):
    n = x.shape[1] // 2
    bits = lax.bitcast_convert_type(x.astype(BF16).astype(F32), jnp.uint32)
    return (bits[:, n:] & jnp.uint32(HI16)) | (bits[:, :n] >> 16)


def _unpack_halves(p):
    lo = lax.bitcast_convert_type(p << 16, F32)
    hi = lax.bitcast_convert_type(p & jnp.uint32(HI16), F32)
    return lo, hi


def _params(sem):
    return pltpu.CompilerParams(dimension_semantics=sem, vmem_limit_bytes=VMEM_LIMIT)


def _ada_kernel(c_ref, w_ref, b_ref, o_ref):
    cond = _silu(c_ref[...])
    o_ref[0] = _bdot(cond, w_ref[0]) + b_ref[0]


def _ada_call(c_pad, ada_w, ada_b):
    L, D, N6 = ada_w.shape
    tn = 1024
    return pl.pallas_call(
        _ada_kernel,
        grid=(L, N6 // tn),
        in_specs=[pl.BlockSpec(c_pad.shape, lambda l, j: (0, 0)),
                  pl.BlockSpec((1, D, tn), lambda l, j: (l, 0, j)),
                  pl.BlockSpec((1, 1, tn), lambda l, j: (l, 0, j))],
        out_specs=pl.BlockSpec((1, c_pad.shape[0], tn), lambda l, j: (l, 0, j)),
        out_shape=jax.ShapeDtypeStruct((L, c_pad.shape[0], N6), F32),
        compiler_params=_params(("arbitrary", "arbitrary")),
        name="adaln",
    )(c_pad, ada_w, ada_b.reshape(L, 1, N6))


def _proj_kernel(x_ref, g_ref, sc_ref, sh_ref, w_ref, mu_ref, o_ref, h_scr, prev_scr, *, tm, seq):
    i = pl.program_id(0)
    j = pl.program_id(1)

    @pl.when(j == 0)
    def _():
        h_scr[...] = _rms_mod(x_ref[...], g_ref[...], sc_ref[0], sh_ref[0]).astype(BF16)

    p = _dot(h_scr[...], w_ref[...])

    @pl.when(j < N_SHIFT_TILES)
    def _():
        seq_start = (i * tm) % seq == 0
        carry = jnp.where(seq_start, 0.0, prev_scr[j])
        row = lax.broadcasted_iota(jnp.int32, p.shape, 0)
        prev = jnp.where(row == 0, carry, pltpu.roll(p, 1, 0))
        o_ref[...] = (p + (prev - p) * mu_ref[...]).astype(o_ref.dtype)
        prev_scr[j] = p[tm - 1:tm, :]

    @pl.when(j >= N_SHIFT_TILES)
    def _():
        o_ref[...] = p.astype(o_ref.dtype)


def _proj_call(x2, g, sc, sh, w_packed, mu_packed, seq, tm=1024):
    T, D = x2.shape
    NP = w_packed.shape[1]
    nj = NP // PROJ_TN
    bmap = lambda i, j: ((i * tm) // seq, 0, 0)
    return pl.pallas_call(
        functools.partial(_proj_kernel, tm=tm, seq=seq),
        grid=(T // tm, nj),
        in_specs=[pl.BlockSpec((tm, D), lambda i, j: (i, 0)),
                  pl.BlockSpec((1, D), lambda i, j: (0, 0)),
                  pl.BlockSpec((1, 1, D), bmap),
                  pl.BlockSpec((1, 1, D), bmap),
                  pl.BlockSpec((D, PROJ_TN), lambda i, j: (0, j)),
                  pl.BlockSpec((1, PROJ_TN), lambda i, j: (0, j))],
        out_specs=pl.BlockSpec((tm, PROJ_TN), lambda i, j: (i, j)),
        out_shape=jax.ShapeDtypeStruct((T, NP), BF16),
        scratch_shapes=[pltpu.VMEM((tm, D), BF16),
                        pltpu.VMEM((N_SHIFT_TILES, 1, PROJ_TN), F32)],
        compiler_params=_params(("arbitrary", "arbitrary")),
        name="proj",
    )(x2, g, sc, sh, w_packed, mu_packed)


RWKV_PASSES = {"lora": 1, "gram": 1, "inv": 1, "apply": 1, "norm": 1}
RWKV_BATCH = 4
RWKV_PAIRS = 4
RWKV_ROWS = 512


def _splitp(a, passes):
    hi = a.astype(BF16)
    if passes == 1:
        return hi, None
    return hi, (a - hi.astype(F32)).astype(BF16)


def _mm(ap, bp, passes, dims=_NN):
    (ah, al), (bh, bl) = ap, bp
    out = _dot(ah, bh, dims)
    if passes == 1:
        return out
    lo = None
    if bl is not None:
        lo = _dot(ah, bl, dims)
    if al is not None:
        t = _dot(al, bh, dims)
        lo = t if lo is None else lo + t
    return out if lo is None else out + lo


def _stack(z, lane_lo):
    return jnp.concatenate([jnp.where(lane_lo, z, 0.0), jnp.where(lane_lo, 0.0, z)], axis=0)


def _rwkv_kernel(*refs, has_vres, nb, npl, rg):
    if has_vres:
        (r_ref, k_ref, v_ref, wd_ref, ad_ref, gd_ref, vd_ref, vf_ref,
         wdu_ref, wau_ref, wgu_ref, wvu_ref, vec_ref, y_ref, st_ref) = refs
    else:
        (r_ref, k_ref, v_ref, wd_ref, ad_ref, gd_ref,
         wdu_ref, wau_ref, wgu_ref, vec_ref, y_ref, st_ref) = refs
    p_lora, p_gram, p_inv, p_app, p_norm = (RWKV_PASSES[s] for s in ("lora", "gram", "inv", "apply", "norm"))
    p_in = max(p_gram, p_app)
    C = CHUNK
    C2 = 2 * C

    @pl.when(pl.program_id(2) == 0)
    def _():
        st_ref[...] = jnp.zeros_like(st_ref)

    lane_lo = lax.broadcasted_iota(jnp.int32, (C, PAIR), 1) < HEAD
    ri = lax.broadcasted_iota(jnp.int32, (C2, C2), 0)
    ci = lax.broadcasted_iota(jnp.int32, (C2, C2), 1)
    strict = ri > ci
    incl = ri >= ci
    eye = jnp.where(ri == ci, 1.0, 0.0)
    wl = npl * PAIR
    ones_pair = jnp.where((ri < HEAD) == (ci < HEAD), 1.0, 0.0).astype(BF16)

    def head_sum(x):
        return jnp.concatenate(
            [_mm(_splitp(x[:, q * PAIR:(q + 1) * PAIR], p_norm), (ones_pair, None), p_norm)
             for q in range(npl)], axis=1)

    rb = lax.broadcasted_iota(jnp.int32, (nb * C, nb * C), 0)
    cb = lax.broadcasted_iota(jnp.int32, (nb * C, nb * C), 1)
    tri_blk = jnp.where(((rb // C) == (cb // C)) & (rb >= cb), 1.0, 0.0).astype(BF16)

    vec = vec_ref[...]
    decay_base, a_base, k_k, k_a = vec[0:1], vec[1:2], vec[2:3], vec[3:4]
    r_k, lnx_g, lnx_b, v_base = vec[4:5], vec[5:6], vec[6:7], vec[7:8]
    wdu = _splitp(wdu_ref[...], p_lora)
    wau = _splitp(wau_ref[...], p_lora)
    wgu = _splitp(wgu_ref[...], p_lora)
    if has_vres:
        wvu = _splitp(wvu_ref[...], p_lora)

    chains = [(b, q) for b in range(nb) for q in range(npl)]
    seqs = range(len(chains))

    def chunk_body(c, carry):
        rows = pl.ds(pl.multiple_of(c * C, C), C)
        cat = lambda ref: jnp.concatenate([ref[b, rows, :].astype(F32) for b in range(nb)], axis=0)

        def part(x, i):
            b, q = chains[i]
            return x[b * C:(b + 1) * C, q * PAIR:(q + 1) * PAIR]

        r, k, v = cat(r_ref), cat(k_ref), cat(v_ref)
        dl = decay_base + _mm(_splitp(jnp.tanh(cat(wd_ref)), p_lora), wdu, p_lora)
        z = -dl
        softplus = jnp.maximum(z, 0.0) + jnp.log(1.0 + jnp.exp(-jnp.abs(z)))
        lw = -jnp.exp(-softplus - 0.5)
        a = _sigmoid(a_base + _mm(_splitp(cat(ad_ref), p_lora), wau, p_lora))
        g = _mm(_splitp(_sigmoid(cat(gd_ref)), p_lora), wgu, p_lora)
        if has_vres:
            mix = _sigmoid(v_base + _mm(_splitp(cat(vd_ref), p_lora), wvu, p_lora))
            v = v + (cat(vf_ref) - v) * mix
        kkr = k * k_k
        kk = kkr / jnp.maximum(jnp.sqrt(head_sum(kkr * kkr)), 1e-12)
        k2 = k * (1.0 + (a - 1.0) * k_a)
        avec = -kk
        bvec = kk * a

        lwh, lwl = _split(lw)
        cl = _dot(tri_blk, lwh) + _dot(tri_blk, lwl)
        tot = jnp.concatenate([jnp.broadcast_to(cl[(b + 1) * C - 1:(b + 1) * C], (C, wl))
                               for b in range(nb)], axis=0)
        w_inv = jnp.exp(-cl)
        w_tail = jnp.exp(tot - cl)
        w_tot = jnp.exp(tot)
        a_t = avec * jnp.exp(cl - lw)
        r_t = r * jnp.exp(cl)
        b_t = bvec * w_inv
        k_t = k2 * w_inv
        b_h = bvec * w_tail
        k_h = k2 * w_tail

        pair2 = lambda x, y, b: jnp.concatenate([_stack(part(x, b), lane_lo),
                                                 _stack(part(y, b), lane_lo)], axis=0)
        lhs = [_splitp(pair2(a_t, r_t, b), p_in) for b in seqs]
        rhs = [_splitp(pair2(b_t, k_t, b), p_gram) for b in seqs]
        gram = [_mm(lhs[b], rhs[b], p_gram, _NT) for b in seqs]
        a_ab = [jnp.where(strict, gm[:C2, :C2], 0.0) for gm in gram]
        a_ak = [jnp.where(strict, gm[:C2, C2:], 0.0) for gm in gram]
        a_r = [jnp.concatenate([jnp.where(incl, gm[C2:, :C2], 0.0),
                                jnp.where(incl, gm[C2:, C2:], 0.0)], axis=1) for gm in gram]

        pw = a_ab
        inv = [eye + m for m in a_ab]
        for _ in range(int(math.log2(C)) - 1):
            pws = [_splitp(m, p_inv) for m in pw]
            pw = [_mm(m, m, p_inv) for m in pws]
            inv = [inv[b] + _mm(_splitp(inv[b], p_inv), _splitp(pw[b], p_inv), p_inv) for b in seqs]

        s_prev = [st_ref[b] for b in seqs]
        sx = [_mm(lhs[b], _splitp(s_prev[b], p_app), p_app, _NT) for b in seqs]
        v_st = [_stack(part(v, b), lane_lo) for b in seqs]
        x_st = [sx[b][:C2] + _mm(_splitp(a_ak[b], p_app), _splitp(v_st[b], p_app), p_app) for b in seqs]
        u_st = [_mm(_splitp(inv[b], p_app), _splitp(x_st[b], p_app), p_app) for b in seqs]
        uv = [jnp.concatenate([u_st[b], v_st[b]], axis=0) for b in seqs]
        y_st = [sx[b][C2:] + _mm(_splitp(a_r[b], p_app), _splitp(uv[b], p_app), p_app) for b in seqs]
        for b in seqs:
            bk = _splitp(pair2(b_h, k_h, b), p_app)
            st_ref[b] = (s_prev[b] * part(w_tot, b)[0:1]
                         + _mm(_splitp(uv[b].T, p_app), bk, p_app))
        y_ch = [ys[:C] + ys[C:] for ys in y_st]
        y = jnp.concatenate([jnp.concatenate(y_ch[b * npl:(b + 1) * npl], axis=1)
                             for b in range(nb)], axis=0)

        inv_n = 1.0 / HEAD
        mu = head_sum(y) * inv_n
        yc = y - mu
        var = head_sum(yc * yc) * inv_n
        yn = yc * lax.rsqrt(var + GN_EPS) * lnx_g + lnx_b
        bonus = head_sum(r * k2 * r_k) * v
        out = (yn + bonus) * g
        for b in range(nb):
            y_ref[b, rows, :] = out[b * C:(b + 1) * C]
        return carry

    lax.fori_loop(0, rg // C, chunk_body, 0)


def _rwkv_call(proj, proj_first, lora_w, vecs, has_vres):
    B, S, _ = proj.shape
    nb, rg, npl = min(RWKV_BATCH, B), min(RWKV_ROWS, S), RWKV_PAIRS
    dr = vecs.shape[1]
    wl = npl * PAIR
    n_blk = dr // wl
    col = lambda off: (lambda bb, p, t: (bb, t, off * n_blk + p))
    fixed = lambda blk: (lambda bb, p, t: (bb, t, blk))
    blk = lambda w, imap: pl.BlockSpec((nb, rg, w), imap)
    lora0 = 3 * dr // PAIR
    in_specs = [blk(wl, col(0)),
                blk(wl, col(1)),
                blk(wl, col(2)),
                blk(PAIR, fixed(lora0)),
                blk(PAIR, fixed(lora0 + 1)),
                blk(2 * PAIR, fixed(lora0 // 2 + 1))]
    args = [proj] * 6
    if has_vres:
        in_specs += [blk(PAIR, fixed(lora0 + 4)),
                     blk(wl, col(2))]
        args += [proj, proj_first]
    wspec = lambda rows: pl.BlockSpec((rows, wl), lambda bb, p, t: (0, p))
    in_specs += [wspec(PAIR), wspec(PAIR), wspec(2 * PAIR)]
    args += [lora_w["decay"], lora_w["a"], lora_w["g"]]
    if has_vres:
        in_specs += [wspec(PAIR)]
        args += [lora_w["v"]]
    in_specs += [pl.BlockSpec((8, wl), lambda bb, p, t: (0, p))]
    args += [vecs]
    return pl.pallas_call(
        functools.partial(_rwkv_kernel, has_vres=has_vres, nb=nb, npl=npl, rg=rg),
        grid=(B // nb, n_blk, S // rg),
        in_specs=in_specs,
        out_specs=pl.BlockSpec((nb, rg, wl), lambda bb, p, t: (bb, t, p)),
        out_shape=jax.ShapeDtypeStruct((B, S, dr), F32),
        scratch_shapes=[pltpu.VMEM((nb * npl, PAIR, PAIR), F32)],
        compiler_params=_params(("arbitrary", "arbitrary", "arbitrary")),
        name="rwkv7",
    )(*args)


def _merge_kernel(x_ref, y_ref, pool_ref, gr_ref, gp_ref, pw_ref, ps_ref, wbr_ref, wbp_ref, wo_ref,
                  g1_ref, o_ref, hist_ref, *, tm, seq):
    i = pl.program_id(0)
    seq_off = (i * tm) % seq
    p = pool_ref[...].astype(F32)
    hist = jnp.where(seq_off == 0, 0.0, hist_ref[...])
    ext = jnp.concatenate([hist, p], axis=0)
    pos = (seq_off + 1 + lax.broadcasted_iota(jnp.int32, (tm, 1), 0)).astype(F32)
    gw = p.shape[1] // len(POOL_WINDOWS)
    zs = []
    for gi, win in enumerate(POOL_WINDOWS):
        e = ext[:, gi * gw:(gi + 1) * gw]
        s = e
        shift = 1
        while shift < win:
            n = s.shape[0]
            s = s[shift:, :] + s[:n - shift, :]
            shift *= 2
        lo = POOL_HIST + 1 - win
        wsum = s[lo:lo + tm, :]
        u = wsum / jnp.minimum(pos, float(win)) - p[:, gi * gw:(gi + 1) * gw]
        zs.append(_bdot(u, pw_ref[gi]))
    y_pool = jnp.concatenate(zs, axis=1) * ps_ref[...]
    hist_ref[...] = p[tm - POOL_HIST:, :]

    br = _bdot(y_ref[...], wbr_ref[...])
    bp = _bdot(y_pool, wbp_ref[...])
    merged = _sigmoid(gr_ref[...].astype(F32)) * br + _sigmoid(gp_ref[...].astype(F32)) * bp
    o_ref[...] = x_ref[...] + g1_ref[0] * _bdot(merged, wo_ref[...])


def _merge_call(x2, y_rwkv, proj, pool_w, pool_scale, w_br_rwkv, w_br_pool, w_out, g1, seq, tm=256):
    T, D = x2.shape
    DR = y_rwkv.shape[1]
    const = lambda shape: pl.BlockSpec(shape, lambda i: (0,) * len(shape),
                                       pipeline_mode=pl.Buffered(1))
    return pl.pallas_call(
        functools.partial(_merge_kernel, tm=tm, seq=seq),
        grid=(T // tm,),
        in_specs=[pl.BlockSpec((tm, D), lambda i: (i, 0)),
                  pl.BlockSpec((tm, DR), lambda i: (i, 0)),
                  pl.BlockSpec((tm, DR), lambda i: (i, 8)),
                  pl.BlockSpec((tm, D), lambda i: (i, 2)),
                  pl.BlockSpec((tm, D), lambda i: (i, 3)),
                  const(pool_w.shape),
                  const(pool_scale.shape),
                  const(w_br_rwkv.shape),
                  const(w_br_pool.shape),
                  const(w_out.shape),
                  pl.BlockSpec((1, 1, D), lambda i: ((i * tm) // seq, 0, 0))],
        out_specs=pl.BlockSpec((tm, D), lambda i: (i, 0)),
        out_shape=jax.ShapeDtypeStruct((T, D), F32),
        scratch_shapes=[pltpu.VMEM((POOL_HIST, DR), F32)],
        compiler_params=_params(("arbitrary",)),
        name="merge",
    )(x2, y_rwkv, proj, proj, proj, pool_w, pool_scale, w_br_rwkv, w_br_pool, w_out, g1)


def _first_argmax(cur, idx, axis):
    m = jnp.max(cur, axis=axis, keepdims=True)
    first = jnp.min(jnp.where(cur == m, idx, 1 << 20), axis=axis, keepdims=True)
    return m, first


def _router_kernel(x_ref, g_ref, sc_ref, sh_ref, wr_ref, bias_ref, h_ref, gt_ref, idx_ref, pos_ref, cnt_ref,
                   *, tm):
    h = _rms_mod(x_ref[...], g_ref[...], sc_ref[0], sh_ref[0])
    h_ref[...] = _pack_halves(h)
    scores = _sigmoid(_dot3(wr_ref[...], h, _NT))
    choice = scores + bias_ref[...]
    per = N_EXPERTS // N_GROUPS
    ch3 = choice.reshape(N_GROUPS, per, tm)
    within = lax.broadcasted_iota(jnp.int32, ch3.shape, 1)
    m1, i1 = _first_argmax(ch3, within, 1)
    m2 = jnp.max(jnp.where(within == i1, NEG_INF, ch3), axis=1, keepdims=True)
    grp = (m1 + m2).reshape(N_GROUPS, tm)
    gidx = lax.broadcasted_iota(jnp.int32, grp.shape, 0)
    gsel = jnp.zeros(grp.shape, F32)
    for _ in range(TOPK_GROUPS):
        _, gi = _first_argmax(grp, gidx, 0)
        pick = gidx == gi
        gsel = jnp.where(pick, 1.0, gsel)
        grp = jnp.where(pick, NEG_INF, grp)
    gsel3 = jnp.broadcast_to(gsel.reshape(N_GROUPS, 1, tm), ch3.shape)
    cur = jnp.where(gsel3 > 0.5, ch3, NEG_INF).reshape(N_EXPERTS, tm)
    eidx = lax.broadcasted_iota(jnp.int32, cur.shape, 0)
    sel = jnp.zeros(cur.shape, F32)
    picks = []
    for _ in range(TOP_K):
        _, ei = _first_argmax(cur, eidx, 0)
        pick = eidx == ei
        sel = jnp.where(pick, 1.0, sel)
        cur = jnp.where(pick, NEG_INF, cur)
        picks.append(ei)
    w = jnp.where(sel > 0.5, scores, 0.0)
    gate = w / jnp.sum(w, axis=0, keepdims=True) * ROUTED_SCALE
    of_pick = lambda v: jnp.concatenate(
        [jnp.sum(jnp.where(eidx == ei, v, 0.0), axis=0, keepdims=True) for ei in picks], axis=0)
    gt_ref[...] = of_pick(gate)
    idx_ref[...] = jnp.concatenate(picks, axis=0)

    @pl.when(pl.program_id(0) == 0)
    def _():
        cnt_ref[...] = jnp.zeros_like(cnt_ref)

    upper = jnp.where(lax.broadcasted_iota(jnp.int32, (tm, tm), 0)
                      <= lax.broadcasted_iota(jnp.int32, (tm, tm), 1), 1.0, 0.0).astype(BF16)
    seen = cnt_ref[:, 0:1] + _dot(sel.astype(BF16), upper)
    pos_ref[...] = (of_pick(seen) - 1.0).astype(jnp.int32)
    cnt_ref[...] = jnp.broadcast_to(seen[:, tm - 1:tm], cnt_ref.shape)


def _router_call(x2, g, sc, sh, w_router_t, bias_col, seq, tm=256):
    T, D = x2.shape
    bmap = lambda i: ((i * tm) // seq, 0, 0)
    return pl.pallas_call(
        functools.partial(_router_kernel, tm=tm),
        grid=(T // tm,),
        in_specs=[pl.BlockSpec((tm, D), lambda i: (i, 0)),
                  pl.BlockSpec((1, D), lambda i: (0, 0)),
                  pl.BlockSpec((1, 1, D), bmap),
                  pl.BlockSpec((1, 1, D), bmap),
                  pl.BlockSpec((N_EXPERTS, D), lambda i: (0, 0)),
                  pl.BlockSpec((N_EXPERTS, 1), lambda i: (0, 0))],
        out_specs=[pl.BlockSpec((tm, D // 2), lambda i: (i, 0)),
                   pl.BlockSpec((TOP_K, tm), lambda i: (0, i)),
                   pl.BlockSpec((TOP_K, tm), lambda i: (0, i)),
                   pl.BlockSpec((TOP_K, tm), lambda i: (0, i)),
                   pl.BlockSpec((N_EXPERTS, LANES), lambda i: (0, 0))],
        out_shape=[jax.ShapeDtypeStruct((T, D // 2), jnp.uint32),
                   jax.ShapeDtypeStruct((TOP_K, T), F32),
                   jax.ShapeDtypeStruct((TOP_K, T), jnp.int32),
                   jax.ShapeDtypeStruct((TOP_K, T), jnp.int32),
                   jax.ShapeDtypeStruct((N_EXPERTS, LANES), F32)],
        compiler_params=_params(("arbitrary",)),
        name="router",
    )(x2, g, sc, sh, w_router_t, bias_col)


def _row_copy(src, src_row, dst, dst_row, sem):
    return pltpu.make_async_copy(src.at[pl.ds(src_row, 1), :], dst.at[pl.ds(dst_row, 1), :], sem)


def _dispatch_kernel(dest_ref, h_ref, xs_ref, sem, *, tm):
    def issue(t, carry):
        for j in range(TOP_K):
            _row_copy(h_ref, t, xs_ref, dest_ref[t * TOP_K + j], sem).start(priority=j % 2)
        return carry

    lax.fori_loop(0, tm, issue, 0)
    for _ in range(TOP_K):
        pltpu.make_async_copy(h_ref, xs_ref.at[pl.ds(0, tm), :], sem).wait()


def _dispatch_call(dest_flat, h2, tm=256):
    T, D = h2.shape
    return pl.pallas_call(
        functools.partial(_dispatch_kernel, tm=tm),
        grid=(T // tm,),
        in_specs=[pl.BlockSpec((tm * TOP_K,), lambda i: (i,), memory_space=pltpu.SMEM),
                  pl.BlockSpec((tm, D), lambda i: (i, 0))],
        out_specs=pl.BlockSpec(memory_space=pl.ANY),
        out_shape=jax.ShapeDtypeStruct((T * TOP_K, D), h2.dtype),
        scratch_shapes=[pltpu.SemaphoreType.DMA(())],
        compiler_params=_params(("arbitrary",)),
        name="dispatch",
    )(dest_flat, h2)


def _expert_kernel(tile_ref, exp_ref, nxt_ref, slot_ref, lo_ref, hi_ref, n_ref, x_ref, wg_hbm, wu_hbm, wd_hbm,
                   o_ref, wg_st, wu_st, wd_st, wg_bf, wu_bf, wd_bf, sem, *, te, layer):
    s = pl.program_id(0)

    def weight_copies(e, sl):
        return [pltpu.make_async_copy(src.at[layer, e], dst.at[sl], sem.at[k, sl])
                for k, (src, dst) in enumerate(((wg_hbm, wg_st), (wu_hbm, wu_st), (wd_hbm, wd_st)))]

    @pl.when(s < n_ref[0])
    def _():
        tile = tile_ref[s]
        prev = jnp.maximum(s - 1, 0)
        e = exp_ref[s]
        sl = slot_ref[s]

        @pl.when(s == 0)
        def _():
            for cp in weight_copies(e, sl):
                cp.start()

        @pl.when((s == 0) | (exp_ref[prev] != e))
        def _():
            for cp in weight_copies(e, sl):
                cp.wait()
            wg_bf[...] = wg_st[sl].astype(BF16)
            wu_bf[...] = wu_st[sl].astype(BF16)
            wd_bf[...] = wd_st[sl].astype(BF16)

            @pl.when(nxt_ref[s] >= 0)
            def _():
                for cp in weight_copies(nxt_ref[s], 1 - sl):
                    cp.start()

        xl, xh = (t.astype(BF16) for t in _unpack_halves(x_ref[...]))
        half = xl.shape[1]
        gate = _dot(xl, wg_bf[0:half, :]) + _dot(xh, wg_bf[half:, :])
        up = _dot(xl, wu_bf[0:half, :]) + _dot(xh, wu_bf[half:, :])
        y = _pack_halves(_dot((_silu(gate) * up).astype(BF16), wd_bf[...]))
        row0 = tile * te
        whole = (lo_ref[s] <= row0) & (hi_ref[s] >= row0 + te)
        first_visit = (s == 0) | (tile_ref[prev] != tile)
        rows = row0 + lax.broadcasted_iota(jnp.int32, (te, 1), 0)
        mine = (rows >= lo_ref[s]) & (rows < hi_ref[s])

        @pl.when(whole)
        def _():
            o_ref[...] = y

        @pl.when(jnp.logical_not(whole) & first_visit)
        def _():
            o_ref[...] = jnp.where(mine, y, jnp.uint32(0))

        @pl.when(jnp.logical_not(whole) & jnp.logical_not(first_visit))
        def _():
            o_ref[...] = jnp.where(mine, y, o_ref[...])


def _expert_call(meta, x_sorted, w_gate, w_up, w_down, layer):
    A, DH = x_sorted.shape
    _, E, D, DE = w_gate.shape
    te = EXPERT_TILE
    n_steps = A // te + E - 1
    hbm = pl.BlockSpec(memory_space=pl.ANY)
    grid_spec = pltpu.PrefetchScalarGridSpec(
        num_scalar_prefetch=7,
        grid=(n_steps,),
        in_specs=[pl.BlockSpec((te, DH), lambda s, tile, *_: (tile[s], 0)), hbm, hbm, hbm],
        out_specs=pl.BlockSpec((te, DH), lambda s, tile, *_: (tile[s], 0)),
        scratch_shapes=[pltpu.VMEM((2, D, DE), F32), pltpu.VMEM((2, D, DE), F32), pltpu.VMEM((2, DE, D), F32),
                        pltpu.VMEM((D, DE), BF16), pltpu.VMEM((D, DE), BF16), pltpu.VMEM((DE, D), BF16),
                        pltpu.SemaphoreType.DMA((3, 2))],
    )
    return pl.pallas_call(
        functools.partial(_expert_kernel, te=te, layer=layer),
        grid_spec=grid_spec,
        out_shape=jax.ShapeDtypeStruct((A, DH), jnp.uint32),
        compiler_params=_params(("arbitrary",)),
        name="experts",
    )(*meta, x_sorted, w_gate, w_up, w_down)


def _expert_steps(counts, n_rows):
    te = EXPERT_TILE
    E = counts.shape[0]
    n_steps = n_rows // te + E - 1
    ends = jnp.cumsum(counts)
    starts = ends - counts
    first_tile = starts // te
    last_tile = jnp.maximum(ends - 1, 0) // te
    n_tiles = jnp.where(counts > 0, last_tile - first_tile + 1, 0)
    step_end = jnp.cumsum(n_tiles)
    step_start = step_end - n_tiles
    total = step_end[-1]
    s = jnp.arange(n_steps, dtype=jnp.int32)
    s_eff = jnp.minimum(s, total - 1)
    exp = jnp.sum(step_end[None, :] <= s_eff[:, None], axis=1, dtype=jnp.int32)
    ids = jnp.arange(E, dtype=jnp.int32)
    used = counts > 0
    at_or_after = jnp.flip(lax.cummin(jnp.flip(jnp.where(used, ids, E))))
    nxt = jnp.concatenate([at_or_after[1:], jnp.full((1,), E, jnp.int32)])
    nxt = jnp.where(nxt >= E, -1, nxt)
    ordinal = jnp.cumsum(used.astype(jnp.int32)) - 1
    is_exp = exp[:, None] == ids[None, :]
    of_step = lambda v: jnp.sum(jnp.where(is_exp, v[None, :], 0), axis=1, dtype=jnp.int32)
    tile = of_step(first_tile) + s_eff - of_step(step_start)
    return (tile, exp, of_step(nxt), of_step(ordinal) % 2, of_step(starts), of_step(ends),
            total.reshape(1).astype(jnp.int32))


def _combine_kernel(dest_ref, dest_next_ref, x_ref, h_ref, w_ref, ys_ref, wsg_ref, wsu_ref, wsd_ref,
                    g2_ref, fg_ref, o_ref, buf, sem, *, tm, n_tiles, final_norm):
    i = pl.program_id(0)
    slot = i % 2

    def issue(dref, sl):
        def body(t, carry):
            for j in range(TOP_K):
                _row_copy(ys_ref, dref[t * TOP_K + j], buf.at[sl, j], t, sem.at[sl]).start(priority=j % 2)
            return carry

        lax.fori_loop(0, tm, body, 0)

    @pl.when(i == 0)
    def _():
        issue(dest_ref, 0)

    @pl.when(i + 1 < n_tiles)
    def _():
        issue(dest_next_ref, 1 - slot)

    hl, hh = (t.astype(BF16) for t in _unpack_halves(h_ref[...]))
    half = hl.shape[1]
    gate = _dot(hl, wsg_ref[0:half, :]) + _dot(hh, wsg_ref[half:, :])
    up = _dot(hl, wsu_ref[0:half, :]) + _dot(hh, wsu_ref[half:, :])
    shared = _dot((_silu(gate) * up).astype(BF16), wsd_ref[...])
    for j in range(TOP_K):
        pltpu.make_async_copy(ys_ref.at[pl.ds(0, tm), :], buf.at[slot, j], sem.at[slot]).wait()
    w = w_ref[...]
    acc_lo = shared[:, :half]
    acc_hi = shared[:, half:]
    for j in range(TOP_K):
        lo, hi = _unpack_halves(buf[slot, j])
        acc_lo = acc_lo + lo * w[:, j:j + 1]
        acc_hi = acc_hi + hi * w[:, j:j + 1]
    acc = jnp.concatenate([acc_lo, acc_hi], axis=1)
    out = x_ref[...] + g2_ref[0] * acc
    if final_norm:
        ms = jnp.mean(out * out, axis=-1, keepdims=True)
        out = out * lax.rsqrt(ms + RMS_EPS) * fg_ref[...]
    o_ref[...] = out


def _combine_call(dest_flat, x2, h2, w_tok, y_sorted, w_sh_gate, w_sh_up, w_sh_down, g2, final_g,
                  seq, final_norm, tm=256):
    T, D = x2.shape
    n_tiles = T // tm
    const = lambda shape: pl.BlockSpec(shape, lambda i: (0,) * len(shape),
                                       pipeline_mode=pl.Buffered(1))
    return pl.pallas_call(
        functools.partial(_combine_kernel, tm=tm, n_tiles=n_tiles, final_norm=final_norm),
        grid=(n_tiles,),
        in_specs=[pl.BlockSpec((tm * TOP_K,), lambda i: (i,), memory_space=pltpu.SMEM),
                  pl.BlockSpec((tm * TOP_K,), lambda i: (jnp.minimum(i + 1, n_tiles - 1),),
                               memory_space=pltpu.SMEM),
                  pl.BlockSpec((tm, D), lambda i: (i, 0)),
                  pl.BlockSpec((tm, D // 2), lambda i: (i, 0)),
                  pl.BlockSpec((tm, TOP_K), lambda i: (i, 0)),
                  pl.BlockSpec(memory_space=pl.ANY),
                  const(w_sh_gate.shape),
                  const(w_sh_up.shape),
                  const(w_sh_down.shape),
                  pl.BlockSpec((1, 1, D), lambda i: ((i * tm) // seq, 0, 0)),
                  const(final_g.shape)],
        out_specs=pl.BlockSpec((tm, D), lambda i: (i, 0)),
        out_shape=jax.ShapeDtypeStruct((T, D), F32),
        scratch_shapes=[pltpu.VMEM((2, TOP_K, tm, D // 2), jnp.uint32), pltpu.SemaphoreType.DMA((2,))],
        compiler_params=_params(("arbitrary",)),
        name="combine",
    )(dest_flat, dest_flat, x2, h2, w_tok, y_sorted, w_sh_gate, w_sh_up, w_sh_down, g2, final_g)


def _pad_cols(w, n):
    return jnp.pad(w, ((0, 0), (0, n - w.shape[1])))


def _pad_rows(w, n):
    return jnp.pad(w, ((0, n - w.shape[0]), (0, 0)))


def _pack_cols(parts):
    return jnp.concatenate([_pad_cols(a, n) for a, n in parts], axis=1)


def kernel(x, c, ada_w, ada_b, norm1_g, norm2_g, w_in, mu_shift, w_decay_up, decay_base, w_a_up, a_base, w_g_up, k_k, k_a, r_k, lnx_g, lnx_b, w_v_down, mu_v, w_v_up, v_base, pool_w, pool_scale, w_br_rwkv, w_br_pool, w_out, w_router, router_bias, w_exp_gate, w_exp_up, w_exp_down, w_sh_gate, w_sh_up, w_sh_down, final_g):
    B, S, D = x.shape
    T = B * S
    L = ada_w.shape[0]
    DR = w_decay_up.shape[2]
    n_dec, n_a, n_g, n_v = w_decay_up.shape[1], w_a_up.shape[1], w_g_up.shape[1], w_v_down.shape[2]
    o_k, o_v, o_wd = DR, 2 * DR, 3 * DR
    o_ad = o_wd + n_dec
    o_gd = o_ad + n_a
    n_shift = o_gd + n_g
    o_gate = n_shift + DR

    c_pad = jnp.pad(c, ((0, 8 - B), (0, 0)))
    mod = _ada_call(c_pad, ada_w, ada_b)[:, :B]
    mod = mod.reshape(L, B, 6, 1, D)

    x2 = x.reshape(T, D)
    proj_first = None
    for l in range(L):
        sh1, sc1, g1, sh2, sc2, g2 = (mod[l, :, q] for q in range(6))
        has_vres = l > 0
        wl = w_in[l]
        zero_cols = jnp.zeros((D, 0), F32)
        vd_w = w_v_down[l - 1] if has_vres else zero_cols
        w_packed = _pack_cols([
            (wl[:, :o_wd], 3 * DR),
            (wl[:, o_wd:o_ad], PAIR), (wl[:, o_ad:o_gd], PAIR), (wl[:, o_gd:n_shift], 2 * PAIR),
            (vd_w, PAIR), (zero_cols, PROJ_TN - 5 * PAIR),
            (wl[:, o_gate:], 2 * D),
            (wl[:, n_shift:o_gate], DR)]).astype(BF16)
        ml = mu_shift[l][None, :]
        zero_mu = jnp.zeros((1, 0), F32)
        vd_mu = mu_v[l - 1][None, :] if has_vres else zero_mu
        mu_packed = _pack_cols([
            (ml[:, :o_wd], 3 * DR),
            (ml[:, o_wd:o_ad], PAIR), (ml[:, o_ad:o_gd], PAIR), (ml[:, o_gd:n_shift], 2 * PAIR),
            (vd_mu, PAIR), (zero_mu, PROJ_TN - 5 * PAIR), (zero_mu, 2 * D + DR)])
        proj = _proj_call(x2, norm1_g[l][None, :], sc1, sh1, w_packed, mu_packed, S)
        if l == 0:
            proj_first = proj

        lora_w = {"decay": _pad_rows(w_decay_up[l], PAIR), "a": _pad_rows(w_a_up[l], PAIR),
                  "g": w_g_up[l]}
        zeros_dr = jnp.zeros((DR,), F32)
        if has_vres:
            lora_w["v"] = _pad_rows(w_v_up[l - 1], PAIR)
        vecs = jnp.stack([decay_base[l], a_base[l], k_k[l], k_a[l], r_k[l].reshape(DR), lnx_g[l],
                          lnx_b[l], v_base[l - 1] if has_vres else zeros_dr])
        y_rwkv = _rwkv_call(proj.reshape(B, S, -1), proj_first.reshape(B, S, -1), lora_w, vecs,
                            has_vres).reshape(T, DR)

        x2 = _merge_call(x2, y_rwkv, proj, pool_w[l].astype(BF16), pool_scale[l][None, :],
                         w_br_rwkv[l].astype(BF16), w_br_pool[l].astype(BF16),
                         w_out[l].astype(BF16), g1, S)

        h2, w_t, idx_t, pos_t, cnt = _router_call(x2, norm2_g[l][None, :], sc2, sh2, w_router[l].T,
                                                  router_bias[l][:, None], S)
        counts = cnt[:, 0].astype(jnp.int32)
        start = jnp.cumsum(counts) - counts
        experts = jnp.arange(N_EXPERTS, dtype=jnp.int32)[None, :, None]
        dest = pos_t + jnp.sum(jnp.where(idx_t[:, None, :] == experts, start[None, :, None], 0), axis=1)
        dest_flat = dest.T.reshape(T * TOP_K).astype(jnp.int32)
        w_tok = w_t.T

        x_sorted = _dispatch_call(dest_flat, h2)
        meta = _expert_steps(counts, T * TOP_K)
        y_sorted = _expert_call(meta, x_sorted, w_exp_gate, w_exp_up, w_exp_down, l)
        x2 = _combine_call(dest_flat, x2, h2, w_tok, y_sorted, w_sh_gate[l].astype(BF16),
                           w_sh_up[l].astype(BF16), w_sh_down[l].astype(BF16), g2,
                           final_g[None, :], S, final_norm=(l == L - 1))
    return x2.reshape(B, S, D)
```

```python
import functools
import math

import jax
import jax.numpy as jnp
from jax import lax
from jax.experimental import pallas as pl
from jax.experimental.pallas import tpu as pltpu

F32 = jnp.float32
BF16 = jnp.bfloat16

HEAD = 64
LANES = 128
PAIR = 2 * HEAD
CHUNK = 64
POOL_WINDOWS = (2, 4, 8, 16)
POOL_HIST = 16
N_EXPERTS = 64
TOP_K = 8
N_GROUPS = 8
TOPK_GROUPS = 4
ROUTED_SCALE = 2.5
GN_EPS = 64e-5
RMS_EPS = 1e-6
NEG_INF = float("-inf")

PROJ_TN = 1024
N_SHIFT_TILES = 4
EXPERT_TILE = 256
VMEM_LIMIT = 56 * 1024 * 1024

_NN = (((1,), (0,)), ((), ()))
_NT = (((1,), (1,)), ((), ()))


def _dot(a, b, dims=_NN):
    return lax.dot_general(a, b, dims, preferred_element_type=F32)


def _bdot(a, b, dims=_NN):
    return _dot(a.astype(BF16), b.astype(BF16), dims)


def _split(a):
    hi = a.astype(BF16)
    lo = (a - hi.astype(F32)).astype(BF16)
    return hi, lo


def _dot3s(ap, bp, dims=_NN):
    (ah, al), (bh, bl) = ap, bp
    return _dot(ah, bh, dims) + (_dot(ah, bl, dims) + _dot(al, bh, dims))


def _dot3(a, b, dims=_NN):
    return _dot3s(_split(a), _split(b), dims)


def _sigmoid(x):
    return 1.0 / (1.0 + jnp.exp(-x))


def _silu(x):
    return x * _sigmoid(x)


def _rms_mod(x, g, sc, sh):
    ms = jnp.mean(x * x, axis=-1, keepdims=True)
    return (x * lax.rsqrt(ms + RMS_EPS) * g) * (1.0 + sc) + sh


HI16 = 0xFFFF0000


def _pack_halves(x):
    n = x.shape[1] // 2
    bits = lax.bitcast_convert_type(x.astype(BF16).astype(F32), jnp.uint32)
    return (bits[:, n:] & jnp.uint32(HI16)) | (bits[:, :n] >> 16)


def _unpack_halves(p):
    lo = lax.bitcast_convert_type(p << 16, F32)
    hi = lax.bitcast_convert_type(p & jnp.uint32(HI16), F32)
    return lo, hi


def _params(sem):
    return pltpu.CompilerParams(dimension_semantics=sem, vmem_limit_bytes=VMEM_LIMIT)


def _ada_kernel(c_ref, w_ref, b_ref, o_ref):
    cond = _silu(c_ref[...])
    o_ref[0] = _bdot(cond, w_ref[0]) + b_ref[0]


def _ada_call(c_pad, ada_w, ada_b):
    L, D, N6 = ada_w.shape
    tn = 1024
    return pl.pallas_call(
        _ada_kernel,
        grid=(L, N6 // tn),
        in_specs=[pl.BlockSpec(c_pad.shape, lambda l, j: (0, 0)),
                  pl.BlockSpec((1, D, tn), lambda l, j: (l, 0, j)),
                  pl.BlockSpec((1, 1, tn), lambda l, j: (l, 0, j))],
        out_specs=pl.BlockSpec((1, c_pad.shape[0], tn), lambda l, j: (l, 0, j)),
        out_shape=jax.ShapeDtypeStruct((L, c_pad.shape[0], N6), F32),
        compiler_params=_params(("arbitrary", "arbitrary")),
        name="adaln",
    )(c_pad, ada_w, ada_b.reshape(L, 1, N6))


def _proj_kernel(x_ref, g_ref, sc_ref, sh_ref, w_ref, mu_ref, o_ref, h_scr, prev_scr, *, tm, seq):
    i = pl.program_id(0)
    j = pl.program_id(1)

    @pl.when(j == 0)
    def _():
        h_scr[...] = _rms_mod(x_ref[...], g_ref[...], sc_ref[0], sh_ref[0]).astype(BF16)

    p = _dot(h_scr[...], w_ref[...])

    @pl.when(j < N_SHIFT_TILES)
    def _():
        seq_start = (i * tm) % seq == 0
        carry = jnp.where(seq_start, 0.0, prev_scr[j])
        row = lax.broadcasted_iota(jnp.int32, p.shape, 0)
        prev = jnp.where(row == 0, carry, pltpu.roll(p, 1, 0))
        o_ref[...] = (p + (prev - p) * mu_ref[...]).astype(o_ref.dtype)
        prev_scr[j] = p[tm - 1:tm, :]

    @pl.when(j >= N_SHIFT_TILES)
    def _():
        o_ref[...] = p.astype(o_ref.dtype)


def _proj_call(x2, g, sc, sh, w_packed, mu_packed, seq, tm=1024):
    T, D = x2.shape
    NP = w_packed.shape[1]
    nj = NP // PROJ_TN
    bmap = lambda i, j: ((i * tm) // seq, 0, 0)
    return pl.pallas_call(
        functools.partial(_proj_kernel, tm=tm, seq=seq),
        grid=(T // tm, nj),
        in_specs=[pl.BlockSpec((tm, D), lambda i, j: (i, 0)),
                  pl.BlockSpec((1, D), lambda i, j: (0, 0)),
                  pl.BlockSpec((1, 1, D), bmap),
                  pl.BlockSpec((1, 1, D), bmap),
                  pl.BlockSpec((D, PROJ_TN), lambda i, j: (0, j)),
                  pl.BlockSpec((1, PROJ_TN), lambda i, j: (0, j))],
        out_specs=pl.BlockSpec((tm, PROJ_TN), lambda i, j: (i, j)),
        out_shape=jax.ShapeDtypeStruct((T, NP), BF16),
        scratch_shapes=[pltpu.VMEM((tm, D), BF16),
                        pltpu.VMEM((N_SHIFT_TILES, 1, PROJ_TN), F32)],
        compiler_params=_params(("arbitrary", "arbitrary")),
        name="proj",
    )(x2, g, sc, sh, w_packed, mu_packed)


RWKV_PASSES = {"lora": 1, "gram": 1, "inv": 1, "apply": 1, "norm": 1}
RWKV_BATCH = 4
RWKV_PAIRS = 4
RWKV_ROWS = 512


def _splitp(a, passes):
    hi = a.astype(BF16)
    if passes == 1:
        return hi, None
    return hi, (a - hi.astype(F32)).astype(BF16)


def _mm(ap, bp, passes, dims=_NN):
    (ah, al), (bh, bl) = ap, bp
    out = _dot(ah, bh, dims)
    if passes == 1:
        return out
    lo = None
    if bl is not None:
        lo = _dot(ah, bl, dims)
    if al is not None:
        t = _dot(al, bh, dims)
        lo = t if lo is None else lo + t
    return out if lo is None else out + lo


def _stack(z, lane_lo):
    return jnp.concatenate([jnp.where(lane_lo, z, 0.0), jnp.where(lane_lo, 0.0, z)], axis=0)


def _rwkv_kernel(*refs, has_vres, nb, npl, rg):
    if has_vres:
        (r_ref, k_ref, v_ref, wd_ref, ad_ref, gd_ref, vd_ref, vf_ref,
         wdu_ref, wau_ref, wgu_ref, wvu_ref, vec_ref, y_ref, st_ref) = refs
    else:
        (r_ref, k_ref, v_ref, wd_ref, ad_ref, gd_ref,
         wdu_ref, wau_ref, wgu_ref, vec_ref, y_ref, st_ref) = refs
    p_lora, p_gram, p_inv, p_app, p_norm = (RWKV_PASSES[s] for s in ("lora", "gram", "inv", "apply", "norm"))
    p_in = max(p_gram, p_app)
    C = CHUNK
    C2 = 2 * C

    @pl.when(pl.program_id(2) == 0)
    def _():
        st_ref[...] = jnp.zeros_like(st_ref)

    lane_lo = lax.broadcasted_iota(jnp.int32, (C, PAIR), 1) < HEAD
    ri = lax.broadcasted_iota(jnp.int32, (C2, C2), 0)
    ci = lax.broadcasted_iota(jnp.int32, (C2, C2), 1)
    strict = ri > ci
    incl = ri >= ci
    eye = jnp.where(ri == ci, 1.0, 0.0)
    wl = npl * PAIR
    ones_pair = jnp.where((ri < HEAD) == (ci < HEAD), 1.0, 0.0).astype(BF16)

    def head_sum(x):
        return jnp.concatenate(
            [_mm(_splitp(x[:, q * PAIR:(q + 1) * PAIR], p_norm), (ones_pair, None), p_norm)
             for q in range(npl)], axis=1)

    rb = lax.broadcasted_iota(jnp.int32, (nb * C, nb * C), 0)
    cb = lax.broadcasted_iota(jnp.int32, (nb * C, nb * C), 1)
    tri_blk = jnp.where(((rb // C) == (cb // C)) & (rb >= cb), 1.0, 0.0).astype(BF16)

    vec = vec_ref[...]
    decay_base, a_base, k_k, k_a = vec[0:1], vec[1:2], vec[2:3], vec[3:4]
    r_k, lnx_g, lnx_b, v_base = vec[4:5], vec[5:6], vec[6:7], vec[7:8]
    wdu = _splitp(wdu_ref[...], p_lora)
    wau = _splitp(wau_ref[...], p_lora)
    wgu = _splitp(wgu_ref[...], p_lora)
    if has_vres:
        wvu = _splitp(wvu_ref[...], p_lora)

    chains = [(b, q) for b in range(nb) for q in range(npl)]
    seqs = range(len(chains))

    def chunk_body(c, carry):
        rows = pl.ds(pl.multiple_of(c * C, C), C)
        cat = lambda ref: jnp.concatenate([ref[b, rows, :].astype(F32) for b in range(nb)], axis=0)

        def part(x, i):
            b, q = chains[i]
            return x[b * C:(b + 1) * C, q * PAIR:(q + 1) * PAIR]

        r, k, v = cat(r_ref), cat(k_ref), cat(v_ref)
        dl = decay_base + _mm(_splitp(jnp.tanh(cat(wd_ref)), p_lora), wdu, p_lora)
        z = -dl
        softplus = jnp.maximum(z, 0.0) + jnp.log(1.0 + jnp.exp(-jnp.abs(z)))
        lw = -jnp.exp(-softplus - 0.5)
        a = _sigmoid(a_base + _mm(_splitp(cat(ad_ref), p_lora), wau, p_lora))
        g = _mm(_splitp(_sigmoid(cat(gd_ref)), p_lora), wgu, p_lora)
        if has_vres:
            mix = _sigmoid(v_base + _mm(_splitp(cat(vd_ref), p_lora), wvu, p_lora))
            v = v + (cat(vf_ref) - v) * mix
        kkr = k * k_k
        kk = kkr / jnp.maximum(jnp.sqrt(head_sum(kkr * kkr)), 1e-12)
        k2 = k * (1.0 + (a - 1.0) * k_a)
        avec = -kk
        bvec = kk * a

        lwh, lwl = _split(lw)
        cl = _dot(tri_blk, lwh) + _dot(tri_blk, lwl)
        tot = jnp.concatenate([jnp.broadcast_to(cl[(b + 1) * C - 1:(b + 1) * C], (C, wl))
                               for b in range(nb)], axis=0)
        w_inv = jnp.exp(-cl)
        w_tail = jnp.exp(tot - cl)
        w_tot = jnp.exp(tot)
        a_t = avec * jnp.exp(cl - lw)
        r_t = r * jnp.exp(cl)
        b_t = bvec * w_inv
        k_t = k2 * w_inv
        b_h = bvec * w_tail
        k_h = k2 * w_tail

        pair2 = lambda x, y, b: jnp.concatenate([_stack(part(x, b), lane_lo),
                                                 _stack(part(y, b), lane_lo)], axis=0)
        lhs = [_splitp(pair2(a_t, r_t, b), p_in) for b in seqs]
        rhs = [_splitp(pair2(b_t, k_t, b), p_gram) for b in seqs]
        gram = [_mm(lhs[b], rhs[b], p_gram, _NT) for b in seqs]
        a_ab = [jnp.where(strict, gm[:C2, :C2], 0.0) for gm in gram]
        a_ak = [jnp.where(strict, gm[:C2, C2:], 0.0) for gm in gram]
        a_r = [jnp.concatenate([jnp.where(incl, gm[C2:, :C2], 0.0),
                                jnp.where(incl, gm[C2:, C2:], 0.0)], axis=1) for gm in gram]

        pw = a_ab
        inv = [eye + m for m in a_ab]
        for _ in range(int(math.log2(C)) - 1):
            pws = [_splitp(m, p_inv) for m in pw]
            pw = [_mm(m, m, p_inv) for m in pws]
            inv = [inv[b] + _mm(_splitp(inv[b], p_inv), _splitp(pw[b], p_inv), p_inv) for b in seqs]

        s_prev = [st_ref[b] for b in seqs]
        sx = [_mm(lhs[b], _splitp(s_prev[b], p_app), p_app, _NT) for b in seqs]
        v_st = [_stack(part(v, b), lane_lo) for b in seqs]
        x_st = [sx[b][:C2] + _mm(_splitp(a_ak[b], p_app), _splitp(v_st[b], p_app), p_app) for b in seqs]
        u_st = [_mm(_splitp(inv[b], p_app), _splitp(x_st[b], p_app), p_app) for b in seqs]
        uv = [jnp.concatenate([u_st[b], v_st[b]], axis=0) for b in seqs]
        y_st = [sx[b][C2:] + _mm(_splitp(a_r[b], p_app), _splitp(uv[b], p_app), p_app) for b in seqs]
        for b in seqs:
            bk = _splitp(pair2(b_h, k_h, b), p_app)
            st_ref[b] = (s_prev[b] * part(w_tot, b)[0:1]
                         + _mm(_splitp(uv[b].T, p_app), bk, p_app))
        y_ch = [ys[:C] + ys[C:] for ys in y_st]
        y = jnp.concatenate([jnp.concatenate(y_ch[b * npl:(b + 1) * npl], axis=1)
                             for b in range(nb)], axis=0)

        inv_n = 1.0 / HEAD
        mu = head_sum(y) * inv_n
        yc = y - mu
        var = head_sum(yc * yc) * inv_n
        yn = yc * lax.rsqrt(var + GN_EPS) * lnx_g + lnx_b
        bonus = head_sum(r * k2 * r_k) * v
        out = (yn + bonus) * g
        for b in range(nb):
            y_ref[b, rows, :] = out[b * C:(b + 1) * C]
        return carry

    lax.fori_loop(0, rg // C, chunk_body, 0)


def _rwkv_call(proj, proj_first, lora_w, vecs, has_vres):
    B, S, _ = proj.shape
    nb, rg, npl = min(RWKV_BATCH, B), min(RWKV_ROWS, S), RWKV_PAIRS
    dr = vecs.shape[1]
    wl = npl * PAIR
    n_blk = dr // wl
    col = lambda off: (lambda bb, p, t: (bb, t, off * n_blk + p))
    fixed = lambda blk: (lambda bb, p, t: (bb, t, blk))
    blk = lambda w, imap: pl.BlockSpec((nb, rg, w), imap)
    lora0 = 3 * dr // PAIR
    in_specs = [blk(wl, col(0)),
                blk(wl, col(1)),
                blk(wl, col(2)),
                blk(PAIR, fixed(lora0)),
                blk(PAIR, fixed(lora0 + 1)),
                blk(2 * PAIR, fixed(lora0 // 2 + 1))]
    args = [proj] * 6
    if has_vres:
        in_specs += [blk(PAIR, fixed(lora0 + 4)),
                     blk(wl, col(2))]
        args += [proj, proj_first]
    wspec = lambda rows: pl.BlockSpec((rows, wl), lambda bb, p, t: (0, p))
    in_specs += [wspec(PAIR), wspec(PAIR), wspec(2 * PAIR)]
    args += [lora_w["decay"], lora_w["a"], lora_w["g"]]
    if has_vres:
        in_specs += [wspec(PAIR)]
        args += [lora_w["v"]]
    in_specs += [pl.BlockSpec((8, wl), lambda bb, p, t: (0, p))]
    args += [vecs]
    return pl.pallas_call(
        functools.partial(_rwkv_kernel, has_vres=has_vres, nb=nb, npl=npl, rg=rg),
        grid=(B // nb, n_blk, S // rg),
        in_specs=in_specs,
        out_specs=pl.BlockSpec((nb, rg, wl), lambda bb, p, t: (bb, t, p)),
        out_shape=jax.ShapeDtypeStruct((B, S, dr), F32),
        scratch_shapes=[pltpu.VMEM((nb * npl, PAIR, PAIR), F32)],
        compiler_params=_params(("arbitrary", "arbitrary", "arbitrary")),
        name="rwkv7",
    )(*args)


def _merge_kernel(x_ref, y_ref, pool_ref, gr_ref, gp_ref, pw_ref, ps_ref, wbr_ref, wbp_ref, wo_ref,
                  g1_ref, o_ref, hist_ref, *, tm, seq):
    i = pl.program_id(0)
    seq_off = (i * tm) % seq
    p = pool_ref[...].astype(F32)
    hist = jnp.where(seq_off == 0, 0.0, hist_ref[...])
    ext = jnp.concatenate([hist, p], axis=0)
    pos = (seq_off + 1 + lax.broadcasted_iota(jnp.int32, (tm, 1), 0)).astype(F32)
    gw = p.shape[1] // len(POOL_WINDOWS)
    zs = []
    for gi, win in enumerate(POOL_WINDOWS):
        e = ext[:, gi * gw:(gi + 1) * gw]
        s = e
        shift = 1
        while shift < win:
            n = s.shape[0]
            s = s[shift:, :] + s[:n - shift, :]
            shift *= 2
        lo = POOL_HIST + 1 - win
        wsum = s[lo:lo + tm, :]
        u = wsum / jnp.minimum(pos, float(win)) - p[:, gi * gw:(gi + 1) * gw]
        zs.append(_bdot(u, pw_ref[gi]))
    y_pool = jnp.concatenate(zs, axis=1) * ps_ref[...]
    hist_ref[...] = p[tm - POOL_HIST:, :]

    br = _bdot(y_ref[...], wbr_ref[...])
    bp = _bdot(y_pool, wbp_ref[...])
    merged = _sigmoid(gr_ref[...].astype(F32)) * br + _sigmoid(gp_ref[...].astype(F32)) * bp
    o_ref[...] = x_ref[...] + g1_ref[0] * _bdot(merged, wo_ref[...])


def _merge_call(x2, y_rwkv, proj, pool_w, pool_scale, w_br_rwkv, w_br_pool, w_out, g1, seq, tm=256):
    T, D = x2.shape
    DR = y_rwkv.shape[1]
    const = lambda shape: pl.BlockSpec(shape, lambda i: (0,) * len(shape),
                                       pipeline_mode=pl.Buffered(1))
    return pl.pallas_call(
        functools.partial(_merge_kernel, tm=tm, seq=seq),
        grid=(T // tm,),
        in_specs=[pl.BlockSpec((tm, D), lambda i: (i, 0)),
                  pl.BlockSpec((tm, DR), lambda i: (i, 0)),
                  pl.BlockSpec((tm, DR), lambda i: (i, 8)),
                  pl.BlockSpec((tm, D), lambda i: (i, 2)),
                  pl.BlockSpec((tm, D), lambda i: (i, 3)),
                  const(pool_w.shape),
                  const(pool_scale.shape),
                  const(w_br_rwkv.shape),
                  const(w_br_pool.shape),
                  const(w_out.shape),
                  pl.BlockSpec((1, 1, D), lambda i: ((i * tm) // seq, 0, 0))],
        out_specs=pl.BlockSpec((tm, D), lambda i: (i, 0)),
        out_shape=jax.ShapeDtypeStruct((T, D), F32),
        scratch_shapes=[pltpu.VMEM((POOL_HIST, DR), F32)],
        compiler_params=_params(("arbitrary",)),
        name="merge",
    )(x2, y_rwkv, proj, proj, proj, pool_w, pool_scale, w_br_rwkv, w_br_pool, w_out, g1)


def _first_argmax(cur, idx, axis):
    m = jnp.max(cur, axis=axis, keepdims=True)
    first = jnp.min(jnp.where(cur == m, idx, 1 << 20), axis=axis, keepdims=True)
    return m, first


def _router_kernel(x_ref, g_ref, sc_ref, sh_ref, wr_ref, bias_ref, h_ref, gt_ref, idx_ref, pos_ref, cnt_ref,
                   *, tm):
    h = _rms_mod(x_ref[...], g_ref[...], sc_ref[0], sh_ref[0])
    h_ref[...] = _pack_halves(h)
    scores = _sigmoid(_dot3(wr_ref[...], h, _NT))
    choice = scores + bias_ref[...]
    per = N_EXPERTS // N_GROUPS
    ch3 = choice.reshape(N_GROUPS, per, tm)
    within = lax.broadcasted_iota(jnp.int32, ch3.shape, 1)
    m1, i1 = _first_argmax(ch3, within, 1)
    m2 = jnp.max(jnp.where(within == i1, NEG_INF, ch3), axis=1, keepdims=True)
    grp = (m1 + m2).reshape(N_GROUPS, tm)
    gidx = lax.broadcasted_iota(jnp.int32, grp.shape, 0)
    gsel = jnp.zeros(grp.shape, F32)
    for _ in range(TOPK_GROUPS):
        _, gi = _first_argmax(grp, gidx, 0)
        pick = gidx == gi
        gsel = jnp.where(pick, 1.0, gsel)
        grp = jnp.where(pick, NEG_INF, grp)
    gsel3 = jnp.broadcast_to(gsel.reshape(N_GROUPS, 1, tm), ch3.shape)
    cur = jnp.where(gsel3 > 0.5, ch3, NEG_INF).reshape(N_EXPERTS, tm)
    eidx = lax.broadcasted_iota(jnp.int32, cur.shape, 0)
    sel = jnp.zeros(cur.shape, F32)
    picks = []
    for _ in range(TOP_K):
        _, ei = _first_argmax(cur, eidx, 0)
        pick = eidx == ei
        sel = jnp.where(pick, 1.0, sel)
        cur = jnp.where(pick, NEG_INF, cur)
        picks.append(ei)
    w = jnp.where(sel > 0.5, scores, 0.0)
    gate = w / jnp.sum(w, axis=0, keepdims=True) * ROUTED_SCALE
    of_pick = lambda v: jnp.concatenate(
        [jnp.sum(jnp.where(eidx == ei, v, 0.0), axis=0, keepdims=True) for ei in picks], axis=0)
    gt_ref[...] = of_pick(gate)
    idx_ref[...] = jnp.concatenate(picks, axis=0)

    @pl.when(pl.program_id(0) == 0)
    def _():
        cnt_ref[...] = jnp.zeros_like(cnt_ref)

    upper = jnp.where(lax.broadcasted_iota(jnp.int32, (tm, tm), 0)
                      <= lax.broadcasted_iota(jnp.int32, (tm, tm), 1), 1.0, 0.0).astype(BF16)
    seen = cnt_ref[:, 0:1] + _dot(sel.astype(BF16), upper)
    pos_ref[...] = (of_pick(seen) - 1.0).astype(jnp.int32)
    cnt_ref[...] = jnp.broadcast_to(seen[:, tm - 1:tm], cnt_ref.shape)


def _router_call(x2, g, sc, sh, w_router_t, bias_col, seq, tm=256):
    T, D = x2.shape
    bmap = lambda i: ((i * tm) // seq, 0, 0)
    return pl.pallas_call(
        functools.partial(_router_kernel, tm=tm),
        grid=(T // tm,),
        in_specs=[pl.BlockSpec((tm, D), lambda i: (i, 0)),
                  pl.BlockSpec((1, D), lambda i: (0, 0)),
                  pl.BlockSpec((1, 1, D), bmap),
                  pl.BlockSpec((1, 1, D), bmap),
                  pl.BlockSpec((N_EXPERTS, D), lambda i: (0, 0)),
                  pl.BlockSpec((N_EXPERTS, 1), lambda i: (0, 0))],
        out_specs=[pl.BlockSpec((tm, D // 2), lambda i: (i, 0)),
                   pl.BlockSpec((TOP_K, tm), lambda i: (0, i)),
                   pl.BlockSpec((TOP_K, tm), lambda i: (0, i)),
                   pl.BlockSpec((TOP_K, tm), lambda i: (0, i)),
                   pl.BlockSpec((N_EXPERTS, LANES), lambda i: (0, 0))],
        out_shape=[jax.ShapeDtypeStruct((T, D // 2), jnp.uint32),
                   jax.ShapeDtypeStruct((TOP_K, T), F32),
                   jax.ShapeDtypeStruct((TOP_K, T), jnp.int32),
                   jax.ShapeDtypeStruct((TOP_K, T), jnp.int32),
                   jax.ShapeDtypeStruct((N_EXPERTS, LANES), F32)],
        compiler_params=_params(("arbitrary",)),
        name="router",
    )(x2, g, sc, sh, w_router_t, bias_col)


def _row_copy(src, src_row, dst, dst_row, sem):
    return pltpu.make_async_copy(src.at[pl.ds(src_row, 1), :], dst.at[pl.ds(dst_row, 1), :], sem)


def _dispatch_kernel(dest_ref, h_ref, xs_ref, sem, *, tm):
    def issue(t, carry):
        for j in range(TOP_K):
            _row_copy(h_ref, t, xs_ref, dest_ref[t * TOP_K + j], sem).start(priority=j % 2)
        return carry

    lax.fori_loop(0, tm, issue, 0)
    for _ in range(TOP_K):
        pltpu.make_async_copy(h_ref, xs_ref.at[pl.ds(0, tm), :], sem).wait()


def _dispatch_call(dest_flat, h2, tm=256):
    T, D = h2.shape
    return pl.pallas_call(
        functools.partial(_dispatch_kernel, tm=tm),
        grid=(T // tm,),
        in_specs=[pl.BlockSpec((tm * TOP_K,), lambda i: (i,), memory_space=pltpu.SMEM),
                  pl.BlockSpec((tm, D), lambda i: (i, 0))],
        out_specs=pl.BlockSpec(memory_space=pl.ANY),
        out_shape=jax.ShapeDtypeStruct((T * TOP_K, D), h2.dtype),
        scratch_shapes=[pltpu.SemaphoreType.DMA(())],
        compiler_params=_params(("arbitrary",)),
        name="dispatch",
    )(dest_flat, h2)


def _expert_kernel(tile_ref, exp_ref, nxt_ref, slot_ref, lo_ref, hi_ref, n_ref, x_ref, wg_hbm, wu_hbm, wd_hbm,
                   o_ref, wg_st, wu_st, wd_st, wg_bf, wu_bf, wd_bf, sem, *, te, layer):
    s = pl.program_id(0)

    def weight_copies(e, sl):
        return [pltpu.make_async_copy(src.at[layer, e], dst.at[sl], sem.at[k, sl])
                for k, (src, dst) in enumerate(((wg_hbm, wg_st), (wu_hbm, wu_st), (wd_hbm, wd_st)))]

    @pl.when(s < n_ref[0])
    def _():
        tile = tile_ref[s]
        prev = jnp.maximum(s - 1, 0)
        e = exp_ref[s]
        sl = slot_ref[s]

        @pl.when(s == 0)
        def _():
            for cp in weight_copies(e, sl):
                cp.start()

        @pl.when((s == 0) | (exp_ref[prev] != e))
        def _():
            for cp in weight_copies(e, sl):
                cp.wait()
            wg_bf[...] = wg_st[sl].astype(BF16)
            wu_bf[...] = wu_st[sl].astype(BF16)
            wd_bf[...] = wd_st[sl].astype(BF16)

            @pl.when(nxt_ref[s] >= 0)
            def _():
                for cp in weight_copies(nxt_ref[s], 1 - sl):
                    cp.start()

        xl, xh = (t.astype(BF16) for t in _unpack_halves(x_ref[...]))
        half = xl.shape[1]
        gate = _dot(xl, wg_bf[0:half, :]) + _dot(xh, wg_bf[half:, :])
        up = _dot(xl, wu_bf[0:half, :]) + _dot(xh, wu_bf[half:, :])
        y = _pack_halves(_dot((_silu(gate) * up).astype(BF16), wd_bf[...]))
        row0 = tile * te
        whole = (lo_ref[s] <= row0) & (hi_ref[s] >= row0 + te)
        first_visit = (s == 0) | (tile_ref[prev] != tile)
        rows = row0 + lax.broadcasted_iota(jnp.int32, (te, 1), 0)
        mine = (rows >= lo_ref[s]) & (rows < hi_ref[s])

        @pl.when(whole)
        def _():
            o_ref[...] = y

        @pl.when(jnp.logical_not(whole) & first_visit)
        def _():
            o_ref[...] = jnp.where(mine, y, jnp.uint32(0))

        @pl.when(jnp.logical_not(whole) & jnp.logical_not(first_visit))
        def _():
            o_ref[...] = jnp.where(mine, y, o_ref[...])


def _expert_call(meta, x_sorted, w_gate, w_up, w_down, layer):
    A, DH = x_sorted.shape
    _, E, D, DE = w_gate.shape
    te = EXPERT_TILE
    n_steps = A // te + E - 1
    hbm = pl.BlockSpec(memory_space=pl.ANY)
    grid_spec = pltpu.PrefetchScalarGridSpec(
        num_scalar_prefetch=7,
        grid=(n_steps,),
        in_specs=[pl.BlockSpec((te, DH), lambda s, tile, *_: (tile[s], 0)), hbm, hbm, hbm],
        out_specs=pl.BlockSpec((te, DH), lambda s, tile, *_: (tile[s], 0)),
        scratch_shapes=[pltpu.VMEM((2, D, DE), F32), pltpu.VMEM((2, D, DE), F32), pltpu.VMEM((2, DE, D), F32),
                        pltpu.VMEM((D, DE), BF16), pltpu.VMEM((D, DE), BF16), pltpu.VMEM((DE, D), BF16),
                        pltpu.SemaphoreType.DMA((3, 2))],
    )
    return pl.pallas_call(
        functools.partial(_expert_kernel, te=te, layer=layer),
        grid_spec=grid_spec,
        out_shape=jax.ShapeDtypeStruct((A, DH), jnp.uint32),
        compiler_params=_params(("arbitrary",)),
        name="experts",
    )(*meta, x_sorted, w_gate, w_up, w_down)


def _expert_steps(counts, n_rows):
    te = EXPERT_TILE
    E = counts.shape[0]
    n_steps = n_rows // te + E - 1
    ends = jnp.cumsum(counts)
    starts = ends - counts
    first_tile = starts // te
    last_tile = jnp.maximum(ends - 1, 0) // te
    n_tiles = jnp.where(counts > 0, last_tile - first_tile + 1, 0)
    step_end = jnp.cumsum(n_tiles)
    step_start = step_end - n_tiles
    total = step_end[-1]
    s = jnp.arange(n_steps, dtype=jnp.int32)
    s_eff = jnp.minimum(s, total - 1)
    exp = jnp.sum(step_end[None, :] <= s_eff[:, None], axis=1, dtype=jnp.int32)
    ids = jnp.arange(E, dtype=jnp.int32)
    used = counts > 0
    at_or_after = jnp.flip(lax.cummin(jnp.flip(jnp.where(used, ids, E))))
    nxt = jnp.concatenate([at_or_after[1:], jnp.full((1,), E, jnp.int32)])
    nxt = jnp.where(nxt >= E, -1, nxt)
    ordinal = jnp.cumsum(used.astype(jnp.int32)) - 1
    is_exp = exp[:, None] == ids[None, :]
    of_step = lambda v: jnp.sum(jnp.where(is_exp, v[None, :], 0), axis=1, dtype=jnp.int32)
    tile = of_step(first_tile) + s_eff - of_step(step_start)
    return (tile, exp, of_step(nxt), of_step(ordinal) % 2, of_step(starts), of_step(ends),
            total.reshape(1).astype(jnp.int32))


def _combine_kernel(dest_ref, dest_next_ref, x_ref, h_ref, w_ref, ys_ref, wsg_ref, wsu_ref, wsd_ref,
                    g2_ref, fg_ref, o_ref, buf, sem, *, tm, n_tiles, final_norm):
    i = pl.program_id(0)
    slot = i % 2

    def copy(dref, sl, t, j):
        return _row_copy(ys_ref, dref[t * TOP_K + j], buf.at[sl, j], t, sem.at[sl])

    def wait_rows(sl):
        for j in range(TOP_K):
            pltpu.make_async_copy(ys_ref.at[pl.ds(0, tm), :], buf.at[sl, j], sem.at[sl]).wait()

    @pl.when(i == 0)
    def _():
        def body(t, carry):
            for j in range(TOP_K):
                copy(dest_ref, 0, t, j).start(priority=j % 2)
            return carry

        lax.fori_loop(0, tm, body, 0)

    wait_rows(slot)
    for t in range(tm):
        for j in range(TOP_K):
            copy(dest_next_ref, 1 - slot, t, j).start(priority=j % 2)

    hl, hh = (t.astype(BF16) for t in _unpack_halves(h_ref[...]))
    half = hl.shape[1]
    gate = _dot(hl, wsg_ref[0:half, :]) + _dot(hh, wsg_ref[half:, :])
    up = _dot(hl, wsu_ref[0:half, :]) + _dot(hh, wsu_ref[half:, :])
    shared = _dot((_silu(gate) * up).astype(BF16), wsd_ref[...])
    w = w_ref[...]
    acc_lo = shared[:, :half]
    acc_hi = shared[:, half:]
    for j in range(TOP_K):
        lo, hi = _unpack_halves(buf[slot, j])
        acc_lo = acc_lo + lo * w[:, j:j + 1]
        acc_hi = acc_hi + hi * w[:, j:j + 1]
    acc = jnp.concatenate([acc_lo, acc_hi], axis=1)
    out = x_ref[...] + g2_ref[0] * acc
    if final_norm:
        ms = jnp.mean(out * out, axis=-1, keepdims=True)
        out = out * lax.rsqrt(ms + RMS_EPS) * fg_ref[...]
    o_ref[...] = out

    @pl.when(i == n_tiles - 1)
    def _():
        wait_rows(1 - slot)


def _combine_call(dest_flat, x2, h2, w_tok, y_sorted, w_sh_gate, w_sh_up, w_sh_down, g2, final_g,
                  seq, final_norm, tm=256):
    T, D = x2.shape
    n_tiles = T // tm
    const = lambda shape: pl.BlockSpec(shape, lambda i: (0,) * len(shape),
                                       pipeline_mode=pl.Buffered(1))
    return pl.pallas_call(
        functools.partial(_combine_kernel, tm=tm, n_tiles=n_tiles, final_norm=final_norm),
        grid=(n_tiles,),
        in_specs=[pl.BlockSpec((tm * TOP_K,), lambda i: (i,), memory_space=pltpu.SMEM),
                  pl.BlockSpec((tm * TOP_K,), lambda i: (jnp.minimum(i + 1, n_tiles - 1),),
                               memory_space=pltpu.SMEM),
                  pl.BlockSpec((tm, D), lambda i: (i, 0)),
                  pl.BlockSpec((tm, D // 2), lambda i: (i, 0)),
                  pl.BlockSpec((tm, TOP_K), lambda i: (i, 0)),
                  pl.BlockSpec(memory_space=pl.ANY),
                  const(w_sh_gate.shape),
                  const(w_sh_up.shape),
                  const(w_sh_down.shape),
                  pl.BlockSpec((1, 1, D), lambda i: ((i * tm) // seq, 0, 0)),
                  const(final_g.shape)],
        out_specs=pl.BlockSpec((tm, D), lambda i: (i, 0)),
        out_shape=jax.ShapeDtypeStruct((T, D), F32),
        scratch_shapes=[pltpu.VMEM((2, TOP_K, tm, D // 2), jnp.uint32), pltpu.SemaphoreType.DMA((2,))],
        compiler_params=_params(("arbitrary",)),
        name="combine",
    )(dest_flat, dest_flat, x2, h2, w_tok, y_sorted, w_sh_gate, w_sh_up, w_sh_down, g2, final_g)


def _pad_cols(w, n):
    return jnp.pad(w, ((0, 0), (0, n - w.shape[1])))


def _pad_rows(w, n):
    return jnp.pad(w, ((0, n - w.shape[0]), (0, 0)))


def _pack_cols(parts):
    return jnp.concatenate([_pad_cols(a, n) for a, n in parts], axis=1)


def kernel(x, c, ada_w, ada_b, norm1_g, norm2_g, w_in, mu_shift, w_decay_up, decay_base, w_a_up, a_base, w_g_up, k_k, k_a, r_k, lnx_g, lnx_b, w_v_down, mu_v, w_v_up, v_base, pool_w, pool_scale, w_br_rwkv, w_br_pool, w_out, w_router, router_bias, w_exp_gate, w_exp_up, w_exp_down, w_sh_gate, w_sh_up, w_sh_down, final_g):
    B, S, D = x.shape
    T = B * S
    L = ada_w.shape[0]
    DR = w_decay_up.shape[2]
    n_dec, n_a, n_g, n_v = w_decay_up.shape[1], w_a_up.shape[1], w_g_up.shape[1], w_v_down.shape[2]
    o_k, o_v, o_wd = DR, 2 * DR, 3 * DR
    o_ad = o_wd + n_dec
    o_gd = o_ad + n_a
    n_shift = o_gd + n_g
    o_gate = n_shift + DR

    c_pad = jnp.pad(c, ((0, 8 - B), (0, 0)))
    mod = _ada_call(c_pad, ada_w, ada_b)[:, :B]
    mod = mod.reshape(L, B, 6, 1, D)

    x2 = x.reshape(T, D)
    proj_first = None
    for l in range(L):
        sh1, sc1, g1, sh2, sc2, g2 = (mod[l, :, q] for q in range(6))
        has_vres = l > 0
        wl = w_in[l]
        zero_cols = jnp.zeros((D, 0), F32)
        vd_w = w_v_down[l - 1] if has_vres else zero_cols
        w_packed = _pack_cols([
            (wl[:, :o_wd], 3 * DR),
            (wl[:, o_wd:o_ad], PAIR), (wl[:, o_ad:o_gd], PAIR), (wl[:, o_gd:n_shift], 2 * PAIR),
            (vd_w, PAIR), (zero_cols, PROJ_TN - 5 * PAIR),
            (wl[:, o_gate:], 2 * D),
            (wl[:, n_shift:o_gate], DR)]).astype(BF16)
        ml = mu_shift[l][None, :]
        zero_mu = jnp.zeros((1, 0), F32)
        vd_mu = mu_v[l - 1][None, :] if has_vres else zero_mu
        mu_packed = _pack_cols([
            (ml[:, :o_wd], 3 * DR),
            (ml[:, o_wd:o_ad], PAIR), (ml[:, o_ad:o_gd], PAIR), (ml[:, o_gd:n_shift], 2 * PAIR),
            (vd_mu, PAIR), (zero_mu, PROJ_TN - 5 * PAIR), (zero_mu, 2 * D + DR)])
        proj = _proj_call(x2, norm1_g[l][None, :], sc1, sh1, w_packed, mu_packed, S)
        if l == 0:
            proj_first = proj

        lora_w = {"decay": _pad_rows(w_decay_up[l], PAIR), "a": _pad_rows(w_a_up[l], PAIR),
                  "g": w_g_up[l]}
        zeros_dr = jnp.zeros((DR,), F32)
        if has_vres:
            lora_w["v"] = _pad_rows(w_v_up[l - 1], PAIR)
        vecs = jnp.stack([decay_base[l], a_base[l], k_k[l], k_a[l], r_k[l].reshape(DR), lnx_g[l],
                          lnx_b[l], v_base[l - 1] if has_vres else zeros_dr])
        y_rwkv = _rwkv_call(proj.reshape(B, S, -1), proj_first.reshape(B, S, -1), lora_w, vecs,
                            has_vres).reshape(T, DR)

        x2 = _merge_call(x2, y_rwkv, proj, pool_w[l].astype(BF16), pool_scale[l][None, :],
                         w_br_rwkv[l].astype(BF16), w_br_pool[l].astype(BF16),
                         w_out[l].astype(BF16), g1, S)

        h2, w_t, idx_t, pos_t, cnt = _router_call(x2, norm2_g[l][None, :], sc2, sh2, w_router[l].T,
                                                  router_bias[l][:, None], S)
        counts = cnt[:, 0].astype(jnp.int32)
        start = jnp.cumsum(counts) - counts
        experts = jnp.arange(N_EXPERTS, dtype=jnp.int32)[None, :, None]
        dest = pos_t + jnp.sum(jnp.where(idx_t[:, None, :] == experts, start[None, :, None], 0), axis=1)
        dest_flat = dest.T.reshape(T * TOP_K).astype(jnp.int32)
        w_tok = w_t.T

        x_sorted = _dispatch_call(dest_flat, h2)
        meta = _expert_steps(counts, T * TOP_K)
        y_sorted = _expert_call(meta, x_sorted, w_exp_gate, w_exp_up, w_exp_down, l)
        x2 = _combine_call(dest_flat, x2, h2, w_tok, y_sorted, w_sh_gate[l].astype(BF16),
                           w_sh_up[l].astype(BF16), w_sh_down[l].astype(BF16), g2,
                           final_g[None, :], S, final_norm=(l == L - 1))
    return x2.reshape(B, S, D)
```

```python
import functools
import math

import jax
import jax.numpy as jnp
from jax import lax
from jax.experimental import pallas as pl
from jax.experimental.pallas import tpu as pltpu

F32 = jnp.float32
BF16 = jnp.bfloat16

HEAD = 64
LANES = 128
PAIR = 2 * HEAD
CHUNK = 64
POOL_WINDOWS = (2, 4, 8, 16)
POOL_HIST = 16
N_EXPERTS = 64
TOP_K = 8
N_GROUPS = 8
TOPK_GROUPS = 4
ROUTED_SCALE = 2.5
GN_EPS = 64e-5
RMS_EPS = 1e-6
NEG_INF = float("-inf")

PROJ_TN = 1024
N_SHIFT_TILES = 4
PROJ_ROW_CHUNK = 512
EXPERT_TILE = 256
VMEM_LIMIT = 56 * 1024 * 1024

_NN = (((1,), (0,)), ((), ()))
_NT = (((1,), (1,)), ((), ()))


def _dot(a, b, dims=_NN):
    return lax.dot_general(a, b, dims, preferred_element_type=F32)


def _bdot(a, b, dims=_NN):
    return _dot(a.astype(BF16), b.astype(BF16), dims)


def _split(a):
    hi = a.astype(BF16)
    lo = (a - hi.astype(F32)).astype(BF16)
    return hi, lo


def _dot3s(ap, bp, dims=_NN):
    (ah, al), (bh, bl) = ap, bp
    return _dot(ah, bh, dims) + (_dot(ah, bl, dims) + _dot(al, bh, dims))


def _dot3(a, b, dims=_NN):
    return _dot3s(_split(a), _split(b), dims)


def _sigmoid(x):
    return 1.0 / (1.0 + jnp.exp(-x))


def _silu(x):
    return x * _sigmoid(x)


def _rms_mod(x, g, sc, sh):
    ms = jnp.mean(x * x, axis=-1, keepdims=True)
    return (x * lax.rsqrt(ms + RMS_EPS) * g) * (1.0 + sc) + sh


HI16 = 0xFFFF0000


def _pack_halves(x):
    n = x.shape[1] // 2
    bits = lax.bitcast_convert_type(x.astype(BF16).astype(F32), jnp.uint32)
    return (bits[:, n:] & jnp.uint32(HI16)) | (bits[:, :n] >> 16)


def _unpack_halves(p):
    lo = lax.bitcast_convert_type(p << 16, F32)
    hi = lax.bitcast_convert_type(p & jnp.uint32(HI16), F32)
    return lo, hi


def _params(sem):
    return pltpu.CompilerParams(dimension_semantics=sem, vmem_limit_bytes=VMEM_LIMIT)


def _ada_kernel(c_ref, w_ref, b_ref, o_ref):
    cond = _silu(c_ref[...])
    o_ref[0] = _bdot(cond, w_ref[0]) + b_ref[0]


def _ada_call(c_pad, ada_w, ada_b):
    L, D, N6 = ada_w.shape
    tn = 1024
    return pl.pallas_call(
        _ada_kernel,
        grid=(L, N6 // tn),
        in_specs=[pl.BlockSpec(c_pad.shape, lambda l, j: (0, 0)),
                  pl.BlockSpec((1, D, tn), lambda l, j: (l, 0, j)),
                  pl.BlockSpec((1, 1, tn), lambda l, j: (l, 0, j))],
        out_specs=pl.BlockSpec((1, c_pad.shape[0], tn), lambda l, j: (l, 0, j)),
        out_shape=jax.ShapeDtypeStruct((L, c_pad.shape[0], N6), F32),
        compiler_params=_params(("arbitrary", "arbitrary")),
        name="adaln",
    )(c_pad, ada_w, ada_b.reshape(L, 1, N6))


def _proj_kernel(x_ref, g_ref, sc_ref, sh_ref, w_ref, mu_ref, o_ref, h_scr, prev_scr, *, tm, seq):
    i = pl.program_id(0)
    j = pl.program_id(1)

    rc = PROJ_ROW_CHUNK
    chunks = [pl.ds(c * rc, rc) for c in range(tm // rc)]

    @pl.when(j == 0)
    def _():
        for rows in chunks:
            h_scr[rows, :] = _rms_mod(x_ref[rows, :], g_ref[...], sc_ref[0], sh_ref[0]).astype(BF16)

    @pl.when(j < N_SHIFT_TILES)
    def _():
        seq_start = (i * tm) % seq == 0
        last = jnp.where(seq_start, 0.0, prev_scr[j])
        row = lax.broadcasted_iota(jnp.int32, (rc, w_ref.shape[1]), 0)
        for rows in chunks:
            p = _dot(h_scr[rows, :], w_ref[...])
            prev = jnp.where(row == 0, last, pltpu.roll(p, 1, 0))
            o_ref[rows, :] = (p + (prev - p) * mu_ref[...]).astype(o_ref.dtype)
            last = p[rc - 1:rc, :]
        prev_scr[j] = last

    @pl.when(j >= N_SHIFT_TILES)
    def _():
        for rows in chunks:
            o_ref[rows, :] = _dot(h_scr[rows, :], w_ref[...]).astype(o_ref.dtype)


def _proj_call(x2, g, sc, sh, w_packed, mu_packed, seq, tm=1024):
    T, D = x2.shape
    NP = w_packed.shape[1]
    nj = NP // PROJ_TN
    bmap = lambda i, j: ((i * tm) // seq, 0, 0)
    return pl.pallas_call(
        functools.partial(_proj_kernel, tm=tm, seq=seq),
        grid=(T // tm, nj),
        in_specs=[pl.BlockSpec((tm, D), lambda i, j: (i, 0)),
                  pl.BlockSpec((1, D), lambda i, j: (0, 0)),
                  pl.BlockSpec((1, 1, D), bmap),
                  pl.BlockSpec((1, 1, D), bmap),
                  pl.BlockSpec((D, PROJ_TN), lambda i, j: (0, j)),
                  pl.BlockSpec((1, PROJ_TN), lambda i, j: (0, j))],
        out_specs=pl.BlockSpec((tm, PROJ_TN), lambda i, j: (i, j)),
        out_shape=jax.ShapeDtypeStruct((T, NP), BF16),
        scratch_shapes=[pltpu.VMEM((tm, D), BF16),
                        pltpu.VMEM((N_SHIFT_TILES, 1, PROJ_TN), F32)],
        compiler_params=_params(("arbitrary", "arbitrary")),
        name="proj",
    )(x2, g, sc, sh, w_packed, mu_packed)


RWKV_PASSES = {"lora": 1, "gram": 1, "inv": 1, "apply": 1, "norm": 1}
RWKV_BATCH = 4
RWKV_PAIRS = 4
RWKV_ROWS = 512


def _splitp(a, passes):
    hi = a.astype(BF16)
    if passes == 1:
        return hi, None
    return hi, (a - hi.astype(F32)).astype(BF16)


def _mm(ap, bp, passes, dims=_NN):
    (ah, al), (bh, bl) = ap, bp
    out = _dot(ah, bh, dims)
    if passes == 1:
        return out
    lo = None
    if bl is not None:
        lo = _dot(ah, bl, dims)
    if al is not None:
        t = _dot(al, bh, dims)
        lo = t if lo is None else lo + t
    return out if lo is None else out + lo


def _stack(z, lane_lo):
    return jnp.concatenate([jnp.where(lane_lo, z, 0.0), jnp.where(lane_lo, 0.0, z)], axis=0)


def _rwkv_kernel(*refs, has_vres, nb, npl, rg):
    if has_vres:
        (r_ref, k_ref, v_ref, wd_ref, ad_ref, gd_ref, vd_ref, vf_ref,
         wdu_ref, wau_ref, wgu_ref, wvu_ref, vec_ref, y_ref, st_ref) = refs
    else:
        (r_ref, k_ref, v_ref, wd_ref, ad_ref, gd_ref,
         wdu_ref, wau_ref, wgu_ref, vec_ref, y_ref, st_ref) = refs
    p_lora, p_gram, p_inv, p_app, p_norm = (RWKV_PASSES[s] for s in ("lora", "gram", "inv", "apply", "norm"))
    p_in = max(p_gram, p_app)
    C = CHUNK
    C2 = 2 * C

    @pl.when(pl.program_id(2) == 0)
    def _():
        st_ref[...] = jnp.zeros_like(st_ref)

    lane_lo = lax.broadcasted_iota(jnp.int32, (C, PAIR), 1) < HEAD
    ri = lax.broadcasted_iota(jnp.int32, (C2, C2), 0)
    ci = lax.broadcasted_iota(jnp.int32, (C2, C2), 1)
    strict = ri > ci
    incl = ri >= ci
    eye = jnp.where(ri == ci, 1.0, 0.0)
    wl = npl * PAIR
    ones_pair = jnp.where((ri < HEAD) == (ci < HEAD), 1.0, 0.0).astype(BF16)

    def head_sum(x):
        return jnp.concatenate(
            [_mm(_splitp(x[:, q * PAIR:(q + 1) * PAIR], p_norm), (ones_pair, None), p_norm)
             for q in range(npl)], axis=1)

    rb = lax.broadcasted_iota(jnp.int32, (nb * C, nb * C), 0)
    cb = lax.broadcasted_iota(jnp.int32, (nb * C, nb * C), 1)
    tri_blk = jnp.where(((rb // C) == (cb // C)) & (rb >= cb), 1.0, 0.0).astype(BF16)

    vec = vec_ref[...]
    decay_base, a_base, k_k, k_a = vec[0:1], vec[1:2], vec[2:3], vec[3:4]
    r_k, lnx_g, lnx_b, v_base = vec[4:5], vec[5:6], vec[6:7], vec[7:8]
    wdu = _splitp(wdu_ref[...], p_lora)
    wau = _splitp(wau_ref[...], p_lora)
    wgu = _splitp(wgu_ref[...], p_lora)
    if has_vres:
        wvu = _splitp(wvu_ref[...], p_lora)

    chains = [(b, q) for b in range(nb) for q in range(npl)]
    seqs = range(len(chains))

    def chunk_body(c, carry):
        rows = pl.ds(pl.multiple_of(c * C, C), C)
        cat = lambda ref: jnp.concatenate([ref[b, rows, :].astype(F32) for b in range(nb)], axis=0)

        def part(x, i):
            b, q = chains[i]
            return x[b * C:(b + 1) * C, q * PAIR:(q + 1) * PAIR]

        r, k, v = cat(r_ref), cat(k_ref), cat(v_ref)
        dl = decay_base + _mm(_splitp(jnp.tanh(cat(wd_ref)), p_lora), wdu, p_lora)
        z = -dl
        softplus = jnp.maximum(z, 0.0) + jnp.log(1.0 + jnp.exp(-jnp.abs(z)))
        lw = -jnp.exp(-softplus - 0.5)
        a = _sigmoid(a_base + _mm(_splitp(cat(ad_ref), p_lora), wau, p_lora))
        g = _mm(_splitp(_sigmoid(cat(gd_ref)), p_lora), wgu, p_lora)
        if has_vres:
            mix = _sigmoid(v_base + _mm(_splitp(cat(vd_ref), p_lora), wvu, p_lora))
            v = v + (cat(vf_ref) - v) * mix
        kkr = k * k_k
        kk = kkr / jnp.maximum(jnp.sqrt(head_sum(kkr * kkr)), 1e-12)
        k2 = k * (1.0 + (a - 1.0) * k_a)
        avec = -kk
        bvec = kk * a

        lwh, lwl = _split(lw)
        cl = _dot(tri_blk, lwh) + _dot(tri_blk, lwl)
        tot = jnp.concatenate([jnp.broadcast_to(cl[(b + 1) * C - 1:(b + 1) * C], (C, wl))
                               for b in range(nb)], axis=0)
        w_inv = jnp.exp(-cl)
        w_tail = jnp.exp(tot - cl)
        w_tot = jnp.exp(tot)
        a_t = avec * jnp.exp(cl - lw)
        r_t = r * jnp.exp(cl)
        b_t = bvec * w_inv
        k_t = k2 * w_inv
        b_h = bvec * w_tail
        k_h = k2 * w_tail

        pair2 = lambda x, y, b: jnp.concatenate([_stack(part(x, b), lane_lo),
                                                 _stack(part(y, b), lane_lo)], axis=0)
        lhs = [_splitp(pair2(a_t, r_t, b), p_in) for b in seqs]
        rhs = [_splitp(pair2(b_t, k_t, b), p_gram) for b in seqs]
        gram = [_mm(lhs[b], rhs[b], p_gram, _NT) for b in seqs]
        a_ab = [jnp.where(strict, gm[:C2, :C2], 0.0) for gm in gram]
        a_ak = [jnp.where(strict, gm[:C2, C2:], 0.0) for gm in gram]
        a_r = [jnp.concatenate([jnp.where(incl, gm[C2:, :C2], 0.0),
                                jnp.where(incl, gm[C2:, C2:], 0.0)], axis=1) for gm in gram]

        pw = a_ab
        inv = [eye + m for m in a_ab]
        for _ in range(int(math.log2(C)) - 1):
            pws = [_splitp(m, p_inv) for m in pw]
            pw = [_mm(m, m, p_inv) for m in pws]
            inv = [inv[b] + _mm(_splitp(inv[b], p_inv), _splitp(pw[b], p_inv), p_inv) for b in seqs]

        s_prev = [st_ref[b] for b in seqs]
        sx = [_mm(lhs[b], _splitp(s_prev[b], p_app), p_app, _NT) for b in seqs]
        v_st = [_stack(part(v, b), lane_lo) for b in seqs]
        x_st = [sx[b][:C2] + _mm(_splitp(a_ak[b], p_app), _splitp(v_st[b], p_app), p_app) for b in seqs]
        u_st = [_mm(_splitp(inv[b], p_app), _splitp(x_st[b], p_app), p_app) for b in seqs]
        uv = [jnp.concatenate([u_st[b], v_st[b]], axis=0) for b in seqs]
        y_st = [sx[b][C2:] + _mm(_splitp(a_r[b], p_app), _splitp(uv[b], p_app), p_app) for b in seqs]
        for b in seqs:
            bk = _splitp(pair2(b_h, k_h, b), p_app)
            st_ref[b] = (s_prev[b] * part(w_tot, b)[0:1]
                         + _mm(_splitp(uv[b].T, p_app), bk, p_app))
        y_ch = [ys[:C] + ys[C:] for ys in y_st]
        y = jnp.concatenate([jnp.concatenate(y_ch[b * npl:(b + 1) * npl], axis=1)
                             for b in range(nb)], axis=0)

        inv_n = 1.0 / HEAD
        mu = head_sum(y) * inv_n
        yc = y - mu
        var = head_sum(yc * yc) * inv_n
        yn = yc * lax.rsqrt(var + GN_EPS) * lnx_g + lnx_b
        bonus = head_sum(r * k2 * r_k) * v
        out = (yn + bonus) * g
        for b in range(nb):
            y_ref[b, rows, :] = out[b * C:(b + 1) * C]
        return carry

    lax.fori_loop(0, rg // C, chunk_body, 0)


def _rwkv_call(proj, proj_first, lora_w, vecs, has_vres):
    B, S, _ = proj.shape
    nb, rg, npl = min(RWKV_BATCH, B), min(RWKV_ROWS, S), RWKV_PAIRS
    dr = vecs.shape[1]
    wl = npl * PAIR
    n_blk = dr // wl
    col = lambda off: (lambda bb, p, t: (bb, t, off * n_blk + p))
    fixed = lambda blk: (lambda bb, p, t: (bb, t, blk))
    blk = lambda w, imap: pl.BlockSpec((nb, rg, w), imap)
    lora0 = 3 * dr // PAIR
    in_specs = [blk(wl, col(0)),
                blk(wl, col(1)),
                blk(wl, col(2)),
                blk(PAIR, fixed(lora0)),
                blk(PAIR, fixed(lora0 + 1)),
                blk(2 * PAIR, fixed(lora0 // 2 + 1))]
    args = [proj] * 6
    if has_vres:
        in_specs += [blk(PAIR, fixed(lora0 + 4)),
                     blk(wl, col(2))]
        args += [proj, proj_first]
    wspec = lambda rows: pl.BlockSpec((rows, wl), lambda bb, p, t: (0, p))
    in_specs += [wspec(PAIR), wspec(PAIR), wspec(2 * PAIR)]
    args += [lora_w["decay"], lora_w["a"], lora_w["g"]]
    if has_vres:
        in_specs += [wspec(PAIR)]
        args += [lora_w["v"]]
    in_specs += [pl.BlockSpec((8, wl), lambda bb, p, t: (0, p))]
    args += [vecs]
    return pl.pallas_call(
        functools.partial(_rwkv_kernel, has_vres=has_vres, nb=nb, npl=npl, rg=rg),
        grid=(B // nb, n_blk, S // rg),
        in_specs=in_specs,
        out_specs=pl.BlockSpec((nb, rg, wl), lambda bb, p, t: (bb, t, p)),
        out_shape=jax.ShapeDtypeStruct((B, S, dr), F32),
        scratch_shapes=[pltpu.VMEM((nb * npl, PAIR, PAIR), F32)],
        compiler_params=_params(("arbitrary", "arbitrary", "arbitrary")),
        name="rwkv7",
    )(*args)


def _merge_kernel(x_ref, y_ref, pool_ref, gr_ref, gp_ref, pw_ref, ps_ref, wbr_ref, wbp_ref, wo_ref,
                  g1_ref, o_ref, hist_ref, *, tm, seq):
    i = pl.program_id(0)
    seq_off = (i * tm) % seq
    p = pool_ref[...].astype(F32)
    hist = jnp.where(seq_off == 0, 0.0, hist_ref[...])
    ext = jnp.concatenate([hist, p], axis=0)
    pos = (seq_off + 1 + lax.broadcasted_iota(jnp.int32, (tm, 1), 0)).astype(F32)
    gw = p.shape[1] // len(POOL_WINDOWS)
    zs = []
    for gi, win in enumerate(POOL_WINDOWS):
        e = ext[:, gi * gw:(gi + 1) * gw]
        s = e
        shift = 1
        while shift < win:
            n = s.shape[0]
            s = s[shift:, :] + s[:n - shift, :]
            shift *= 2
        lo = POOL_HIST + 1 - win
        wsum = s[lo:lo + tm, :]
        u = wsum / jnp.minimum(pos, float(win)) - p[:, gi * gw:(gi + 1) * gw]
        zs.append(_bdot(u, pw_ref[gi]))
    y_pool = jnp.concatenate(zs, axis=1) * ps_ref[...]
    hist_ref[...] = p[tm - POOL_HIST:, :]

    br = _bdot(y_ref[...], wbr_ref[...])
    bp = _bdot(y_pool, wbp_ref[...])
    merged = _sigmoid(gr_ref[...].astype(F32)) * br + _sigmoid(gp_ref[...].astype(F32)) * bp
    o_ref[...] = x_ref[...] + g1_ref[0] * _bdot(merged, wo_ref[...])


def _merge_call(x2, y_rwkv, proj, pool_w, pool_scale, w_br_rwkv, w_br_pool, w_out, g1, seq, tm=256):
    T, D = x2.shape
    DR = y_rwkv.shape[1]
    const = lambda shape: pl.BlockSpec(shape, lambda i: (0,) * len(shape),
                                       pipeline_mode=pl.Buffered(1))
    return pl.pallas_call(
        functools.partial(_merge_kernel, tm=tm, seq=seq),
        grid=(T // tm,),
        in_specs=[pl.BlockSpec((tm, D), lambda i: (i, 0)),
                  pl.BlockSpec((tm, DR), lambda i: (i, 0)),
                  pl.BlockSpec((tm, DR), lambda i: (i, 8)),
                  pl.BlockSpec((tm, D), lambda i: (i, 2)),
                  pl.BlockSpec((tm, D), lambda i: (i, 3)),
                  const(pool_w.shape),
                  const(pool_scale.shape),
                  const(w_br_rwkv.shape),
                  const(w_br_pool.shape),
                  const(w_out.shape),
                  pl.BlockSpec((1, 1, D), lambda i: ((i * tm) // seq, 0, 0))],
        out_specs=pl.BlockSpec((tm, D), lambda i: (i, 0)),
        out_shape=jax.ShapeDtypeStruct((T, D), F32),
        scratch_shapes=[pltpu.VMEM((POOL_HIST, DR), F32)],
        compiler_params=_params(("arbitrary",)),
        name="merge",
    )(x2, y_rwkv, proj, proj, proj, pool_w, pool_scale, w_br_rwkv, w_br_pool, w_out, g1)


def _first_argmax(cur, idx, axis):
    m = jnp.max(cur, axis=axis, keepdims=True)
    first = jnp.min(jnp.where(cur == m, idx, 1 << 20), axis=axis, keepdims=True)
    return m, first


def _router_kernel(x_ref, g_ref, sc_ref, sh_ref, wr_ref, bias_ref, h_ref, gt_ref, idx_ref, pos_ref, cnt_ref,
                   *, tm):
    h = _rms_mod(x_ref[...], g_ref[...], sc_ref[0], sh_ref[0])
    h_ref[...] = _pack_halves(h)
    scores = _sigmoid(_dot3(wr_ref[...], h, _NT))
    choice = scores + bias_ref[...]
    per = N_EXPERTS // N_GROUPS
    ch3 = choice.reshape(N_GROUPS, per, tm)
    within = lax.broadcasted_iota(jnp.int32, ch3.shape, 1)
    m1, i1 = _first_argmax(ch3, within, 1)
    m2 = jnp.max(jnp.where(within == i1, NEG_INF, ch3), axis=1, keepdims=True)
    grp = (m1 + m2).reshape(N_GROUPS, tm)
    gidx = lax.broadcasted_iota(jnp.int32, grp.shape, 0)
    gsel = jnp.zeros(grp.shape, F32)
    for _ in range(TOPK_GROUPS):
        _, gi = _first_argmax(grp, gidx, 0)
        pick = gidx == gi
        gsel = jnp.where(pick, 1.0, gsel)
        grp = jnp.where(pick, NEG_INF, grp)
    gsel3 = jnp.broadcast_to(gsel.reshape(N_GROUPS, 1, tm), ch3.shape)
    cur = jnp.where(gsel3 > 0.5, ch3, NEG_INF).reshape(N_EXPERTS, tm)
    eidx = lax.broadcasted_iota(jnp.int32, cur.shape, 0)
    sel = jnp.zeros(cur.shape, F32)
    picks = []
    for _ in range(TOP_K):
        _, ei = _first_argmax(cur, eidx, 0)
        pick = eidx == ei
        sel = jnp.where(pick, 1.0, sel)
        cur = jnp.where(pick, NEG_INF, cur)
        picks.append(ei)
    w = jnp.where(sel > 0.5, scores, 0.0)
    gate = w / jnp.sum(w, axis=0, keepdims=True) * ROUTED_SCALE
    of_pick = lambda v: jnp.concatenate(
        [jnp.sum(jnp.where(eidx == ei, v, 0.0), axis=0, keepdims=True) for ei in picks], axis=0)
    gt_ref[...] = of_pick(gate)
    idx_ref[...] = jnp.concatenate(picks, axis=0)

    @pl.when(pl.program_id(0) == 0)
    def _():
        cnt_ref[...] = jnp.zeros_like(cnt_ref)

    upper = jnp.where(lax.broadcasted_iota(jnp.int32, (tm, tm), 0)
                      <= lax.broadcasted_iota(jnp.int32, (tm, tm), 1), 1.0, 0.0).astype(BF16)
    seen = cnt_ref[:, 0:1] + _dot(sel.astype(BF16), upper)
    pos_ref[...] = (of_pick(seen) - 1.0).astype(jnp.int32)
    cnt_ref[...] = jnp.broadcast_to(seen[:, tm - 1:tm], cnt_ref.shape)


def _router_call(x2, g, sc, sh, w_router_t, bias_col, seq, tm=256):
    T, D = x2.shape
    bmap = lambda i: ((i * tm) // seq, 0, 0)
    return pl.pallas_call(
        functools.partial(_router_kernel, tm=tm),
        grid=(T // tm,),
        in_specs=[pl.BlockSpec((tm, D), lambda i: (i, 0)),
                  pl.BlockSpec((1, D), lambda i: (0, 0)),
                  pl.BlockSpec((1, 1, D), bmap),
                  pl.BlockSpec((1, 1, D), bmap),
                  pl.BlockSpec((N_EXPERTS, D), lambda i: (0, 0)),
                  pl.BlockSpec((N_EXPERTS, 1), lambda i: (0, 0))],
        out_specs=[pl.BlockSpec((tm, D // 2), lambda i: (i, 0)),
                   pl.BlockSpec((TOP_K, tm), lambda i: (0, i)),
                   pl.BlockSpec((TOP_K, tm), lambda i: (0, i)),
                   pl.BlockSpec((TOP_K, tm), lambda i: (0, i)),
                   pl.BlockSpec((N_EXPERTS, LANES), lambda i: (0, 0))],
        out_shape=[jax.ShapeDtypeStruct((T, D // 2), jnp.uint32),
                   jax.ShapeDtypeStruct((TOP_K, T), F32),
                   jax.ShapeDtypeStruct((TOP_K, T), jnp.int32),
                   jax.ShapeDtypeStruct((TOP_K, T), jnp.int32),
                   jax.ShapeDtypeStruct((N_EXPERTS, LANES), F32)],
        compiler_params=_params(("arbitrary",)),
        name="router",
    )(x2, g, sc, sh, w_router_t, bias_col)


def _row_copy(src, src_row, dst, dst_row, sem):
    return pltpu.make_async_copy(src.at[pl.ds(src_row, 1), :], dst.at[pl.ds(dst_row, 1), :], sem)


def _dispatch_kernel(dest_ref, h_ref, xs_ref, sem, *, tm):
    def issue(t, carry):
        for j in range(TOP_K):
            _row_copy(h_ref, t, xs_ref, dest_ref[t * TOP_K + j], sem).start(priority=j % 2)
        return carry

    lax.fori_loop(0, tm, issue, 0)
    for _ in range(TOP_K):
        pltpu.make_async_copy(h_ref, xs_ref.at[pl.ds(0, tm), :], sem).wait()


def _dispatch_call(dest_flat, h2, tm=256):
    T, D = h2.shape
    return pl.pallas_call(
        functools.partial(_dispatch_kernel, tm=tm),
        grid=(T // tm,),
        in_specs=[pl.BlockSpec((tm * TOP_K,), lambda i: (i,), memory_space=pltpu.SMEM),
                  pl.BlockSpec((tm, D), lambda i: (i, 0))],
        out_specs=pl.BlockSpec(memory_space=pl.ANY),
        out_shape=jax.ShapeDtypeStruct((T * TOP_K, D), h2.dtype),
        scratch_shapes=[pltpu.SemaphoreType.DMA(())],
        compiler_params=_params(("arbitrary",)),
        name="dispatch",
    )(dest_flat, h2)


def _expert_kernel(tile_ref, exp_ref, nxt_ref, slot_ref, lo_ref, hi_ref, n_ref, x_ref, wg_hbm, wu_hbm, wd_hbm,
                   o_ref, wg_st, wu_st, wd_st, wg_bf, wu_bf, wd_bf, sem, *, te, layer):
    s = pl.program_id(0)

    def weight_copies(e, sl):
        return [pltpu.make_async_copy(src.at[layer, e], dst.at[sl], sem.at[k, sl])
                for k, (src, dst) in enumerate(((wg_hbm, wg_st), (wu_hbm, wu_st), (wd_hbm, wd_st)))]

    @pl.when(s < n_ref[0])
    def _():
        tile = tile_ref[s]
        prev = jnp.maximum(s - 1, 0)
        e = exp_ref[s]
        sl = slot_ref[s]

        @pl.when(s == 0)
        def _():
            for cp in weight_copies(e, sl):
                cp.start()

        @pl.when((s == 0) | (exp_ref[prev] != e))
        def _():
            for cp in weight_copies(e, sl):
                cp.wait()
            wg_bf[...] = wg_st[sl].astype(BF16)
            wu_bf[...] = wu_st[sl].astype(BF16)
            wd_bf[...] = wd_st[sl].astype(BF16)

            @pl.when(nxt_ref[s] >= 0)
            def _():
                for cp in weight_copies(nxt_ref[s], 1 - sl):
                    cp.start()

        xl, xh = (t.astype(BF16) for t in _unpack_halves(x_ref[...]))
        half = xl.shape[1]
        gate = _dot(xl, wg_bf[0:half, :]) + _dot(xh, wg_bf[half:, :])
        up = _dot(xl, wu_bf[0:half, :]) + _dot(xh, wu_bf[half:, :])
        y = _pack_halves(_dot((_silu(gate) * up).astype(BF16), wd_bf[...]))
        row0 = tile * te
        whole = (lo_ref[s] <= row0) & (hi_ref[s] >= row0 + te)
        first_visit = (s == 0) | (tile_ref[prev] != tile)
        rows = row0 + lax.broadcasted_iota(jnp.int32, (te, 1), 0)
        mine = (rows >= lo_ref[s]) & (rows < hi_ref[s])

        @pl.when(whole)
        def _():
            o_ref[...] = y

        @pl.when(jnp.logical_not(whole) & first_visit)
        def _():
            o_ref[...] = jnp.where(mine, y, jnp.uint32(0))

        @pl.when(jnp.logical_not(whole) & jnp.logical_not(first_visit))
        def _():
            o_ref[...] = jnp.where(mine, y, o_ref[...])


def _expert_call(meta, x_sorted, w_gate, w_up, w_down, layer):
    A, DH = x_sorted.shape
    _, E, D, DE = w_gate.shape
    te = EXPERT_TILE
    n_steps = A // te + E - 1
    hbm = pl.BlockSpec(memory_space=pl.ANY)
    grid_spec = pltpu.PrefetchScalarGridSpec(
        num_scalar_prefetch=7,
        grid=(n_steps,),
        in_specs=[pl.BlockSpec((te, DH), lambda s, tile, *_: (tile[s], 0)), hbm, hbm, hbm],
        out_specs=pl.BlockSpec((te, DH), lambda s, tile, *_: (tile[s], 0)),
        scratch_shapes=[pltpu.VMEM((2, D, DE), F32), pltpu.VMEM((2, D, DE), F32), pltpu.VMEM((2, DE, D), F32),
                        pltpu.VMEM((D, DE), BF16), pltpu.VMEM((D, DE), BF16), pltpu.VMEM((DE, D), BF16),
                        pltpu.SemaphoreType.DMA((3, 2))],
    )
    return pl.pallas_call(
        functools.partial(_expert_kernel, te=te, layer=layer),
        grid_spec=grid_spec,
        out_shape=jax.ShapeDtypeStruct((A, DH), jnp.uint32),
        compiler_params=_params(("arbitrary",)),
        name="experts",
    )(*meta, x_sorted, w_gate, w_up, w_down)


def _expert_steps(counts, n_rows):
    te = EXPERT_TILE
    E = counts.shape[0]
    n_steps = n_rows // te + E - 1
    ends = jnp.cumsum(counts)
    starts = ends - counts
    first_tile = starts // te
    last_tile = jnp.maximum(ends - 1, 0) // te
    n_tiles = jnp.where(counts > 0, last_tile - first_tile + 1, 0)
    step_end = jnp.cumsum(n_tiles)
    step_start = step_end - n_tiles
    total = step_end[-1]
    s = jnp.arange(n_steps, dtype=jnp.int32)
    s_eff = jnp.minimum(s, total - 1)
    exp = jnp.sum(step_end[None, :] <= s_eff[:, None], axis=1, dtype=jnp.int32)
    ids = jnp.arange(E, dtype=jnp.int32)
    used = counts > 0
    at_or_after = jnp.flip(lax.cummin(jnp.flip(jnp.where(used, ids, E))))
    nxt = jnp.concatenate([at_or_after[1:], jnp.full((1,), E, jnp.int32)])
    nxt = jnp.where(nxt >= E, -1, nxt)
    ordinal = jnp.cumsum(used.astype(jnp.int32)) - 1
    is_exp = exp[:, None] == ids[None, :]
    of_step = lambda v: jnp.sum(jnp.where(is_exp, v[None, :], 0), axis=1, dtype=jnp.int32)
    tile = of_step(first_tile) + s_eff - of_step(step_start)
    return (tile, exp, of_step(nxt), of_step(ordinal) % 2, of_step(starts), of_step(ends),
            total.reshape(1).astype(jnp.int32))


def _combine_kernel(dest_ref, dest_next_ref, x_ref, h_ref, w_ref, ys_ref, wsg_ref, wsu_ref, wsd_ref,
                    g2_ref, fg_ref, o_ref, buf, sem, *, tm, n_tiles, final_norm):
    i = pl.program_id(0)
    slot = i % 2

    def copy(dref, sl, t, j):
        return _row_copy(ys_ref, dref[t * TOP_K + j], buf.at[sl, j], t, sem.at[sl])

    def wait_rows(sl):
        for j in range(TOP_K):
            pltpu.make_async_copy(ys_ref.at[pl.ds(0, tm), :], buf.at[sl, j], sem.at[sl]).wait()

    @pl.when(i == 0)
    def _():
        def body(t, carry):
            for j in range(TOP_K):
                copy(dest_ref, 0, t, j).start(priority=j % 2)
            return carry

        lax.fori_loop(0, tm, body, 0)

    wait_rows(slot)
    for t in range(tm):
        for j in range(TOP_K):
            copy(dest_next_ref, 1 - slot, t, j).start(priority=j % 2)

    hl, hh = (t.astype(BF16) for t in _unpack_halves(h_ref[...]))
    half = hl.shape[1]
    gate = _dot(hl, wsg_ref[0:half, :]) + _dot(hh, wsg_ref[half:, :])
    up = _dot(hl, wsu_ref[0:half, :]) + _dot(hh, wsu_ref[half:, :])
    shared = _dot((_silu(gate) * up).astype(BF16), wsd_ref[...])
    w = w_ref[...]
    acc_lo = shared[:, :half]
    acc_hi = shared[:, half:]
    for j in range(TOP_K):
        lo, hi = _unpack_halves(buf[slot, j])
        acc_lo = acc_lo + lo * w[:, j:j + 1]
        acc_hi = acc_hi + hi * w[:, j:j + 1]
    acc = jnp.concatenate([acc_lo, acc_hi], axis=1)
    out = x_ref[...] + g2_ref[0] * acc
    if final_norm:
        ms = jnp.mean(out * out, axis=-1, keepdims=True)
        out = out * lax.rsqrt(ms + RMS_EPS) * fg_ref[...]
    o_ref[...] = out

    @pl.when(i == n_tiles - 1)
    def _():
        wait_rows(1 - slot)


def _combine_call(dest_flat, x2, h2, w_tok, y_sorted, w_sh_gate, w_sh_up, w_sh_down, g2, final_g,
                  seq, final_norm, tm=256):
    T, D = x2.shape
    n_tiles = T // tm
    const = lambda shape: pl.BlockSpec(shape, lambda i: (0,) * len(shape),
                                       pipeline_mode=pl.Buffered(1))
    return pl.pallas_call(
        functools.partial(_combine_kernel, tm=tm, n_tiles=n_tiles, final_norm=final_norm),
        grid=(n_tiles,),
        in_specs=[pl.BlockSpec((tm * TOP_K,), lambda i: (i,), memory_space=pltpu.SMEM),
                  pl.BlockSpec((tm * TOP_K,), lambda i: (jnp.minimum(i + 1, n_tiles - 1),),
                               memory_space=pltpu.SMEM),
                  pl.BlockSpec((tm, D), lambda i: (i, 0)),
                  pl.BlockSpec((tm, D // 2), lambda i: (i, 0)),
                  pl.BlockSpec((tm, TOP_K), lambda i: (i, 0)),
                  pl.BlockSpec(memory_space=pl.ANY),
                  const(w_sh_gate.shape),
                  const(w_sh_up.shape),
                  const(w_sh_down.shape),
                  pl.BlockSpec((1, 1, D), lambda i: ((i * tm) // seq, 0, 0)),
                  const(final_g.shape)],
        out_specs=pl.BlockSpec((tm, D), lambda i: (i, 0)),
        out_shape=jax.ShapeDtypeStruct((T, D), F32),
        scratch_shapes=[pltpu.VMEM((2, TOP_K, tm, D // 2), jnp.uint32), pltpu.SemaphoreType.DMA((2,))],
        compiler_params=_params(("arbitrary",)),
        name="combine",
    )(dest_flat, dest_flat, x2, h2, w_tok, y_sorted, w_sh_gate, w_sh_up, w_sh_down, g2, final_g)


def _pad_cols(w, n):
    return jnp.pad(w, ((0, 0), (0, n - w.shape[1])))


def _pad_rows(w, n):
    return jnp.pad(w, ((0, n - w.shape[0]), (0, 0)))


def _pack_cols(parts):
    return jnp.concatenate([_pad_cols(a, n) for a, n in parts], axis=1)


def kernel(x, c, ada_w, ada_b, norm1_g, norm2_g, w_in, mu_shift, w_decay_up, decay_base, w_a_up, a_base, w_g_up, k_k, k_a, r_k, lnx_g, lnx_b, w_v_down, mu_v, w_v_up, v_base, pool_w, pool_scale, w_br_rwkv, w_br_pool, w_out, w_router, router_bias, w_exp_gate, w_exp_up, w_exp_down, w_sh_gate, w_sh_up, w_sh_down, final_g):
    B, S, D = x.shape
    T = B * S
    L = ada_w.shape[0]
    DR = w_decay_up.shape[2]
    n_dec, n_a, n_g, n_v = w_decay_up.shape[1], w_a_up.shape[1], w_g_up.shape[1], w_v_down.shape[2]
    o_k, o_v, o_wd = DR, 2 * DR, 3 * DR
    o_ad = o_wd + n_dec
    o_gd = o_ad + n_a
    n_shift = o_gd + n_g
    o_gate = n_shift + DR

    c_pad = jnp.pad(c, ((0, 8 - B), (0, 0)))
    mod = _ada_call(c_pad, ada_w, ada_b)[:, :B]
    mod = mod.reshape(L, B, 6, 1, D)

    x2 = x.reshape(T, D)
    proj_first = None
    for l in range(L):
        sh1, sc1, g1, sh2, sc2, g2 = (mod[l, :, q] for q in range(6))
        has_vres = l > 0
        wl = w_in[l]
        zero_cols = jnp.zeros((D, 0), F32)
        vd_w = w_v_down[l - 1] if has_vres else zero_cols
        w_packed = _pack_cols([
            (wl[:, :o_wd], 3 * DR),
            (wl[:, o_wd:o_ad], PAIR), (wl[:, o_ad:o_gd], PAIR), (wl[:, o_gd:n_shift], 2 * PAIR),
            (vd_w, PAIR), (zero_cols, PROJ_TN - 5 * PAIR),
            (wl[:, o_gate:], 2 * D),
            (wl[:, n_shift:o_gate], DR)]).astype(BF16)
        ml = mu_shift[l][None, :]
        zero_mu = jnp.zeros((1, 0), F32)
        vd_mu = mu_v[l - 1][None, :] if has_vres else zero_mu
        mu_packed = _pack_cols([
            (ml[:, :o_wd], 3 * DR),
            (ml[:, o_wd:o_ad], PAIR), (ml[:, o_ad:o_gd], PAIR), (ml[:, o_gd:n_shift], 2 * PAIR),
            (vd_mu, PAIR), (zero_mu, PROJ_TN - 5 * PAIR), (zero_mu, 2 * D + DR)])
        proj = _proj_call(x2, norm1_g[l][None, :], sc1, sh1, w_packed, mu_packed, S)
        if l == 0:
            proj_first = proj

        lora_w = {"decay": _pad_rows(w_decay_up[l], PAIR), "a": _pad_rows(w_a_up[l], PAIR),
                  "g": w_g_up[l]}
        zeros_dr = jnp.zeros((DR,), F32)
        if has_vres:
            lora_w["v"] = _pad_rows(w_v_up[l - 1], PAIR)
        vecs = jnp.stack([decay_base[l], a_base[l], k_k[l], k_a[l], r_k[l].reshape(DR), lnx_g[l],
                          lnx_b[l], v_base[l - 1] if has_vres else zeros_dr])
        y_rwkv = _rwkv_call(proj.reshape(B, S, -1), proj_first.reshape(B, S, -1), lora_w, vecs,
                            has_vres).reshape(T, DR)

        x2 = _merge_call(x2, y_rwkv, proj, pool_w[l].astype(BF16), pool_scale[l][None, :],
                         w_br_rwkv[l].astype(BF16), w_br_pool[l].astype(BF16),
                         w_out[l].astype(BF16), g1, S)

        h2, w_t, idx_t, pos_t, cnt = _router_call(x2, norm2_g[l][None, :], sc2, sh2, w_router[l].T,
                                                  router_bias[l][:, None], S)
        counts = cnt[:, 0].astype(jnp.int32)
        start = jnp.cumsum(counts) - counts
        experts = jnp.arange(N_EXPERTS, dtype=jnp.int32)[None, :, None]
        dest = pos_t + jnp.sum(jnp.where(idx_t[:, None, :] == experts, start[None, :, None], 0), axis=1)
        dest_flat = dest.T.reshape(T * TOP_K).astype(jnp.int32)
        w_tok = w_t.T

        x_sorted = _dispatch_call(dest_flat, h2)
        meta = _expert_steps(counts, T * TOP_K)
        y_sorted = _expert_call(meta, x_sorted, w_exp_gate, w_exp_up, w_exp_down, l)
        x2 = _combine_call(dest_flat, x2, h2, w_tok, y_sorted, w_sh_gate[l].astype(BF16),
                           w_sh_up[l].astype(BF16), w_sh_down[l].astype(BF16), g2,
                           final_g[None, :], S, final_norm=(l == L - 1))
    return x2.reshape(B, S, D)
```

```python
import functools
import math

import jax
import jax.numpy as jnp
from jax import lax
from jax.experimental import pallas as pl
from jax.experimental.pallas import tpu as pltpu

F32 = jnp.float32
BF16 = jnp.bfloat16

HEAD = 64
LANES = 128
PAIR = 2 * HEAD
CHUNK = 64
POOL_WINDOWS = (2, 4, 8, 16)
POOL_HIST = 16
N_EXPERTS = 64
TOP_K = 8
N_GROUPS = 8
TOPK_GROUPS = 4
ROUTED_SCALE = 2.5
GN_EPS = 64e-5
RMS_EPS = 1e-6
NEG_INF = float("-inf")

PROJ_TN = 1024
N_SHIFT_TILES = 4
PROJ_ROW_CHUNK = 512
EXPERT_TILE = 256
VMEM_LIMIT = 56 * 1024 * 1024

_NN = (((1,), (0,)), ((), ()))
_NT = (((1,), (1,)), ((), ()))


def _dot(a, b, dims=_NN):
    return lax.dot_general(a, b, dims, preferred_element_type=F32)


def _bdot(a, b, dims=_NN):
    return _dot(a.astype(BF16), b.astype(BF16), dims)


def _split(a):
    hi = a.astype(BF16)
    lo = (a - hi.astype(F32)).astype(BF16)
    return hi, lo


def _dot3s(ap, bp, dims=_NN):
    (ah, al), (bh, bl) = ap, bp
    return _dot(ah, bh, dims) + (_dot(ah, bl, dims) + _dot(al, bh, dims))


def _dot3(a, b, dims=_NN):
    return _dot3s(_split(a), _split(b), dims)


def _sigmoid(x):
    return 1.0 / (1.0 + jnp.exp(-x))


def _silu(x):
    return x * _sigmoid(x)


def _rms_mod(x, g, sc, sh):
    ms = jnp.mean(x * x, axis=-1, keepdims=True)
    return (x * lax.rsqrt(ms + RMS_EPS) * g) * (1.0 + sc) + sh


HI16 = 0xFFFF0000


def _pack_halves(x):
    n = x.shape[1] // 2
    bits = lax.bitcast_convert_type(x.astype(BF16).astype(F32), jnp.uint32)
    return (bits[:, n:] & jnp.uint32(HI16)) | (bits[:, :n] >> 16)


def _unpack_halves(p):
    lo = lax.bitcast_convert_type(p << 16, F32)
    hi = lax.bitcast_convert_type(p & jnp.uint32(HI16), F32)
    return lo, hi


def _params(sem):
    return pltpu.CompilerParams(dimension_semantics=sem, vmem_limit_bytes=VMEM_LIMIT)


def _ada_kernel(c_ref, w_ref, b_ref, o_ref):
    cond = _silu(c_ref[...])
    o_ref[0] = _bdot(cond, w_ref[0]) + b_ref[0]


def _ada_call(c_pad, ada_w, ada_b):
    L, D, N6 = ada_w.shape
    tn = 2048
    return pl.pallas_call(
        _ada_kernel,
        grid=(L, N6 // tn),
        in_specs=[pl.BlockSpec(c_pad.shape, lambda l, j: (0, 0)),
                  pl.BlockSpec((1, D, tn), lambda l, j: (l, 0, j)),
                  pl.BlockSpec((1, 1, tn), lambda l, j: (l, 0, j))],
        out_specs=pl.BlockSpec((1, c_pad.shape[0], tn), lambda l, j: (l, 0, j)),
        out_shape=jax.ShapeDtypeStruct((L, c_pad.shape[0], N6), F32),
        compiler_params=_params(("arbitrary", "arbitrary")),
        name="adaln",
    )(c_pad, ada_w, ada_b.reshape(L, 1, N6))


def _proj_kernel(x_ref, g_ref, sc_ref, sh_ref, w_ref, mu_ref, o_ref, h_scr, prev_scr, *, tm, seq):
    i = pl.program_id(0)
    j = pl.program_id(1)

    rc = PROJ_ROW_CHUNK
    chunks = [pl.ds(c * rc, rc) for c in range(tm // rc)]

    @pl.when(j == 0)
    def _():
        for rows in chunks:
            h_scr[rows, :] = _rms_mod(x_ref[rows, :], g_ref[...], sc_ref[0], sh_ref[0]).astype(BF16)

    @pl.when(j < N_SHIFT_TILES)
    def _():
        seq_start = (i * tm) % seq == 0
        last = jnp.where(seq_start, 0.0, prev_scr[j])
        row = lax.broadcasted_iota(jnp.int32, (rc, w_ref.shape[1]), 0)
        for rows in chunks:
            p = _dot(h_scr[rows, :], w_ref[...])
            prev = jnp.where(row == 0, last, pltpu.roll(p, 1, 0))
            o_ref[rows, :] = (p + (prev - p) * mu_ref[...]).astype(o_ref.dtype)
            last = p[rc - 1:rc, :]
        prev_scr[j] = last

    @pl.when(j >= N_SHIFT_TILES)
    def _():
        for rows in chunks:
            o_ref[rows, :] = _dot(h_scr[rows, :], w_ref[...]).astype(o_ref.dtype)


def _proj_call(x2, g, sc, sh, w_packed, mu_packed, seq, tm=1024):
    T, D = x2.shape
    NP = w_packed.shape[1]
    nj = NP // PROJ_TN
    bmap = lambda i, j: ((i * tm) // seq, 0, 0)
    return pl.pallas_call(
        functools.partial(_proj_kernel, tm=tm, seq=seq),
        grid=(T // tm, nj),
        in_specs=[pl.BlockSpec((tm, D), lambda i, j: (i, 0)),
                  pl.BlockSpec((1, D), lambda i, j: (0, 0)),
                  pl.BlockSpec((1, 1, D), bmap),
                  pl.BlockSpec((1, 1, D), bmap),
                  pl.BlockSpec((D, PROJ_TN), lambda i, j: (0, j)),
                  pl.BlockSpec((1, PROJ_TN), lambda i, j: (0, j))],
        out_specs=pl.BlockSpec((tm, PROJ_TN), lambda i, j: (i, j)),
        out_shape=jax.ShapeDtypeStruct((T, NP), BF16),
        scratch_shapes=[pltpu.VMEM((tm, D), BF16),
                        pltpu.VMEM((N_SHIFT_TILES, 1, PROJ_TN), F32)],
        compiler_params=_params(("arbitrary", "arbitrary")),
        name="proj",
    )(x2, g, sc, sh, w_packed, mu_packed)


RWKV_PASSES = {"lora": 1, "gram": 1, "inv": 1, "apply": 1, "norm": 1}
RWKV_BATCH = 4
RWKV_PAIRS = 4
RWKV_ROWS = 512


def _splitp(a, passes):
    hi = a.astype(BF16)
    if passes == 1:
        return hi, None
    return hi, (a - hi.astype(F32)).astype(BF16)


def _mm(ap, bp, passes, dims=_NN):
    (ah, al), (bh, bl) = ap, bp
    out = _dot(ah, bh, dims)
    if passes == 1:
        return out
    lo = None
    if bl is not None:
        lo = _dot(ah, bl, dims)
    if al is not None:
        t = _dot(al, bh, dims)
        lo = t if lo is None else lo + t
    return out if lo is None else out + lo


def _stack(z, lane_lo):
    return jnp.concatenate([jnp.where(lane_lo, z, 0.0), jnp.where(lane_lo, 0.0, z)], axis=0)


def _rwkv_kernel(*refs, has_vres, nb, npl, rg):
    if has_vres:
        (r_ref, k_ref, v_ref, wd_ref, ad_ref, gd_ref, vd_ref, vf_ref,
         wdu_ref, wau_ref, wgu_ref, wvu_ref, vec_ref, y_ref, st_ref) = refs
    else:
        (r_ref, k_ref, v_ref, wd_ref, ad_ref, gd_ref,
         wdu_ref, wau_ref, wgu_ref, vec_ref, y_ref, st_ref) = refs
    p_lora, p_gram, p_inv, p_app, p_norm = (RWKV_PASSES[s] for s in ("lora", "gram", "inv", "apply", "norm"))
    p_in = max(p_gram, p_app)
    C = CHUNK
    C2 = 2 * C

    @pl.when(pl.program_id(2) == 0)
    def _():
        st_ref[...] = jnp.zeros_like(st_ref)

    lane_lo = lax.broadcasted_iota(jnp.int32, (C, PAIR), 1) < HEAD
    ri = lax.broadcasted_iota(jnp.int32, (C2, C2), 0)
    ci = lax.broadcasted_iota(jnp.int32, (C2, C2), 1)
    strict = ri > ci
    incl = ri >= ci
    eye = jnp.where(ri == ci, 1.0, 0.0)
    wl = npl * PAIR
    ones_pair = jnp.where((ri < HEAD) == (ci < HEAD), 1.0, 0.0).astype(BF16)

    def head_sum(x):
        return jnp.concatenate(
            [_mm(_splitp(x[:, q * PAIR:(q + 1) * PAIR], p_norm), (ones_pair, None), p_norm)
             for q in range(npl)], axis=1)

    rb = lax.broadcasted_iota(jnp.int32, (nb * C, nb * C), 0)
    cb = lax.broadcasted_iota(jnp.int32, (nb * C, nb * C), 1)
    tri_blk = jnp.where(((rb // C) == (cb // C)) & (rb >= cb), 1.0, 0.0).astype(BF16)

    vec = vec_ref[...]
    decay_base, a_base, k_k, k_a = vec[0:1], vec[1:2], vec[2:3], vec[3:4]
    r_k, lnx_g, lnx_b, v_base = vec[4:5], vec[5:6], vec[6:7], vec[7:8]
    wdu = _splitp(wdu_ref[...], p_lora)
    wau = _splitp(wau_ref[...], p_lora)
    wgu = _splitp(wgu_ref[...], p_lora)
    if has_vres:
        wvu = _splitp(wvu_ref[...], p_lora)

    chains = [(b, q) for b in range(nb) for q in range(npl)]
    seqs = range(len(chains))

    def chunk_body(c, carry):
        rows = pl.ds(pl.multiple_of(c * C, C), C)
        cat = lambda ref: jnp.concatenate([ref[b, rows, :].astype(F32) for b in range(nb)], axis=0)

        def part(x, i):
            b, q = chains[i]
            return x[b * C:(b + 1) * C, q * PAIR:(q + 1) * PAIR]

        r, k, v = cat(r_ref), cat(k_ref), cat(v_ref)
        dl = decay_base + _mm(_splitp(jnp.tanh(cat(wd_ref)), p_lora), wdu, p_lora)
        z = -dl
        softplus = jnp.maximum(z, 0.0) + jnp.log(1.0 + jnp.exp(-jnp.abs(z)))
        lw = -jnp.exp(-softplus - 0.5)
        a = _sigmoid(a_base + _mm(_splitp(cat(ad_ref), p_lora), wau, p_lora))
        g = _mm(_splitp(_sigmoid(cat(gd_ref)), p_lora), wgu, p_lora)
        if has_vres:
            mix = _sigmoid(v_base + _mm(_splitp(cat(vd_ref), p_lora), wvu, p_lora))
            v = v + (cat(vf_ref) - v) * mix
        kkr = k * k_k
        kk = kkr / jnp.maximum(jnp.sqrt(head_sum(kkr * kkr)), 1e-12)
        k2 = k * (1.0 + (a - 1.0) * k_a)
        avec = -kk
        bvec = kk * a

        lwh, lwl = _split(lw)
        cl = _dot(tri_blk, lwh) + _dot(tri_blk, lwl)
        tot = jnp.concatenate([jnp.broadcast_to(cl[(b + 1) * C - 1:(b + 1) * C], (C, wl))
                               for b in range(nb)], axis=0)
        w_inv = jnp.exp(-cl)
        w_tail = jnp.exp(tot - cl)
        w_tot = jnp.exp(tot)
        a_t = avec * jnp.exp(cl - lw)
        r_t = r * jnp.exp(cl)
        b_t = bvec * w_inv
        k_t = k2 * w_inv
        b_h = bvec * w_tail
        k_h = k2 * w_tail

        pair2 = lambda x, y, b: jnp.concatenate([_stack(part(x, b), lane_lo),
                                                 _stack(part(y, b), lane_lo)], axis=0)
        lhs = [_splitp(pair2(a_t, r_t, b), p_in) for b in seqs]
        rhs = [_splitp(pair2(b_t, k_t, b), p_gram) for b in seqs]
        gram = [_mm(lhs[b], rhs[b], p_gram, _NT) for b in seqs]
        a_ab = [jnp.where(strict, gm[:C2, :C2], 0.0) for gm in gram]
        a_ak = [jnp.where(strict, gm[:C2, C2:], 0.0) for gm in gram]
        a_r = [jnp.concatenate([jnp.where(incl, gm[C2:, :C2], 0.0),
                                jnp.where(incl, gm[C2:, C2:], 0.0)], axis=1) for gm in gram]

        pw = a_ab
        inv = [eye + m for m in a_ab]
        for _ in range(int(math.log2(C)) - 1):
            pws = [_splitp(m, p_inv) for m in pw]
            pw = [_mm(m, m, p_inv) for m in pws]
            inv = [inv[b] + _mm(_splitp(inv[b], p_inv), _splitp(pw[b], p_inv), p_inv) for b in seqs]

        s_prev = [st_ref[b] for b in seqs]
        sx = [_mm(lhs[b], _splitp(s_prev[b], p_app), p_app, _NT) for b in seqs]
        v_st = [_stack(part(v, b), lane_lo) for b in seqs]
        x_st = [sx[b][:C2] + _mm(_splitp(a_ak[b], p_app), _splitp(v_st[b], p_app), p_app) for b in seqs]
        u_st = [_mm(_splitp(inv[b], p_app), _splitp(x_st[b], p_app), p_app) for b in seqs]
        uv = [jnp.concatenate([u_st[b], v_st[b]], axis=0) for b in seqs]
        y_st = [sx[b][C2:] + _mm(_splitp(a_r[b], p_app), _splitp(uv[b], p_app), p_app) for b in seqs]
        for b in seqs:
            bk = _splitp(pair2(b_h, k_h, b), p_app)
            st_ref[b] = (s_prev[b] * part(w_tot, b)[0:1]
                         + _mm(_splitp(uv[b].T, p_app), bk, p_app))
        y_ch = [ys[:C] + ys[C:] for ys in y_st]
        y = jnp.concatenate([jnp.concatenate(y_ch[b * npl:(b + 1) * npl], axis=1)
                             for b in range(nb)], axis=0)

        inv_n = 1.0 / HEAD
        mu = head_sum(y) * inv_n
        yc = y - mu
        var = head_sum(yc * yc) * inv_n
        yn = yc * lax.rsqrt(var + GN_EPS) * lnx_g + lnx_b
        bonus = head_sum(r * k2 * r_k) * v
        out = (yn + bonus) * g
        for b in range(nb):
            y_ref[b, rows, :] = out[b * C:(b + 1) * C]
        return carry

    lax.fori_loop(0, rg // C, chunk_body, 0)


def _rwkv_call(proj, proj_first, lora_w, vecs, has_vres):
    B, S, _ = proj.shape
    nb, rg, npl = min(RWKV_BATCH, B), min(RWKV_ROWS, S), RWKV_PAIRS
    dr = vecs.shape[1]
    wl = npl * PAIR
    n_blk = dr // wl
    col = lambda off: (lambda bb, p, t: (bb, t, off * n_blk + p))
    fixed = lambda blk: (lambda bb, p, t: (bb, t, blk))
    blk = lambda w, imap: pl.BlockSpec((nb, rg, w), imap)
    lora0 = 3 * dr // PAIR
    in_specs = [blk(wl, col(0)),
                blk(wl, col(1)),
                blk(wl, col(2)),
                blk(PAIR, fixed(lora0)),
                blk(PAIR, fixed(lora0 + 1)),
                blk(2 * PAIR, fixed(lora0 // 2 + 1))]
    args = [proj] * 6
    if has_vres:
        in_specs += [blk(PAIR, fixed(lora0 + 4)),
                     blk(wl, col(2))]
        args += [proj, proj_first]
    wspec = lambda rows: pl.BlockSpec((rows, wl), lambda bb, p, t: (0, p))
    in_specs += [wspec(PAIR), wspec(PAIR), wspec(2 * PAIR)]
    args += [lora_w["decay"], lora_w["a"], lora_w["g"]]
    if has_vres:
        in_specs += [wspec(PAIR)]
        args += [lora_w["v"]]
    in_specs += [pl.BlockSpec((8, wl), lambda bb, p, t: (0, p))]
    args += [vecs]
    return pl.pallas_call(
        functools.partial(_rwkv_kernel, has_vres=has_vres, nb=nb, npl=npl, rg=rg),
        grid=(B // nb, n_blk, S // rg),
        in_specs=in_specs,
        out_specs=pl.BlockSpec((nb, rg, wl), lambda bb, p, t: (bb, t, p)),
        out_shape=jax.ShapeDtypeStruct((B, S, dr), F32),
        scratch_shapes=[pltpu.VMEM((nb * npl, PAIR, PAIR), F32)],
        compiler_params=_params(("arbitrary", "arbitrary", "arbitrary")),
        name="rwkv7",
    )(*args)


def _merge_kernel(x_ref, y_ref, pool_ref, gr_ref, gp_ref, pw_ref, ps_ref, wbr_ref, wbp_ref, wo_ref,
                  g1_ref, o_ref, hist_ref, *, tm, seq):
    i = pl.program_id(0)
    seq_off = (i * tm) % seq
    p = pool_ref[...].astype(F32)
    hist = jnp.where(seq_off == 0, 0.0, hist_ref[...])
    ext = jnp.concatenate([hist, p], axis=0)
    pos = (seq_off + 1 + lax.broadcasted_iota(jnp.int32, (tm, 1), 0)).astype(F32)
    gw = p.shape[1] // len(POOL_WINDOWS)
    zs = []
    for gi, win in enumerate(POOL_WINDOWS):
        e = ext[:, gi * gw:(gi + 1) * gw]
        s = e
        shift = 1
        while shift < win:
            n = s.shape[0]
            s = s[shift:, :] + s[:n - shift, :]
            shift *= 2
        lo = POOL_HIST + 1 - win
        wsum = s[lo:lo + tm, :]
        u = wsum / jnp.minimum(pos, float(win)) - p[:, gi * gw:(gi + 1) * gw]
        zs.append(_bdot(u, pw_ref[gi]))
    y_pool = jnp.concatenate(zs, axis=1) * ps_ref[...]
    hist_ref[...] = p[tm - POOL_HIST:, :]

    br = _bdot(y_ref[...], wbr_ref[...])
    bp = _bdot(y_pool, wbp_ref[...])
    merged = _sigmoid(gr_ref[...].astype(F32)) * br + _sigmoid(gp_ref[...].astype(F32)) * bp
    o_ref[...] = x_ref[...] + g1_ref[0] * _bdot(merged, wo_ref[...])


def _merge_call(x2, y_rwkv, proj, pool_w, pool_scale, w_br_rwkv, w_br_pool, w_out, g1, seq, tm=256):
    T, D = x2.shape
    DR = y_rwkv.shape[1]
    const = lambda shape: pl.BlockSpec(shape, lambda i: (0,) * len(shape),
                                       pipeline_mode=pl.Buffered(1))
    return pl.pallas_call(
        functools.partial(_merge_kernel, tm=tm, seq=seq),
        grid=(T // tm,),
        in_specs=[pl.BlockSpec((tm, D), lambda i: (i, 0)),
                  pl.BlockSpec((tm, DR), lambda i: (i, 0)),
                  pl.BlockSpec((tm, DR), lambda i: (i, 8)),
                  pl.BlockSpec((tm, D), lambda i: (i, 2)),
                  pl.BlockSpec((tm, D), lambda i: (i, 3)),
                  const(pool_w.shape),
                  const(pool_scale.shape),
                  const(w_br_rwkv.shape),
                  const(w_br_pool.shape),
                  const(w_out.shape),
                  pl.BlockSpec((1, 1, D), lambda i: ((i * tm) // seq, 0, 0))],
        out_specs=pl.BlockSpec((tm, D), lambda i: (i, 0)),
        out_shape=jax.ShapeDtypeStruct((T, D), F32),
        scratch_shapes=[pltpu.VMEM((POOL_HIST, DR), F32)],
        compiler_params=_params(("arbitrary",)),
        name="merge",
    )(x2, y_rwkv, proj, proj, proj, pool_w, pool_scale, w_br_rwkv, w_br_pool, w_out, g1)


def _first_argmax(cur, idx, axis):
    m = jnp.max(cur, axis=axis, keepdims=True)
    first = jnp.min(jnp.where(cur == m, idx, 1 << 20), axis=axis, keepdims=True)
    return m, first


def _router_kernel(x_ref, g_ref, sc_ref, sh_ref, wr_ref, bias_ref, h_ref, gt_ref, idx_ref, pos_ref, cnt_ref,
                   *, tm):
    h = _rms_mod(x_ref[...], g_ref[...], sc_ref[0], sh_ref[0])
    h_ref[...] = _pack_halves(h)
    scores = _sigmoid(_dot3(wr_ref[...], h, _NT))
    choice = scores + bias_ref[...]
    per = N_EXPERTS // N_GROUPS
    ch3 = choice.reshape(N_GROUPS, per, tm)
    within = lax.broadcasted_iota(jnp.int32, ch3.shape, 1)
    m1, i1 = _first_argmax(ch3, within, 1)
    m2 = jnp.max(jnp.where(within == i1, NEG_INF, ch3), axis=1, keepdims=True)
    grp = (m1 + m2).reshape(N_GROUPS, tm)
    gidx = lax.broadcasted_iota(jnp.int32, grp.shape, 0)
    gsel = jnp.zeros(grp.shape, F32)
    for _ in range(TOPK_GROUPS):
        _, gi = _first_argmax(grp, gidx, 0)
        pick = gidx == gi
        gsel = jnp.where(pick, 1.0, gsel)
        grp = jnp.where(pick, NEG_INF, grp)
    gsel3 = jnp.broadcast_to(gsel.reshape(N_GROUPS, 1, tm), ch3.shape)
    cur = jnp.where(gsel3 > 0.5, ch3, NEG_INF).reshape(N_EXPERTS, tm)
    eidx = lax.broadcasted_iota(jnp.int32, cur.shape, 0)
    sel = jnp.zeros(cur.shape, F32)
    picks = []
    for _ in range(TOP_K):
        _, ei = _first_argmax(cur, eidx, 0)
        pick = eidx == ei
        sel = jnp.where(pick, 1.0, sel)
        cur = jnp.where(pick, NEG_INF, cur)
        picks.append(ei)
    w = jnp.where(sel > 0.5, scores, 0.0)
    gate = w / jnp.sum(w, axis=0, keepdims=True) * ROUTED_SCALE
    of_pick = lambda v: jnp.concatenate(
        [jnp.sum(jnp.where(eidx == ei, v, 0.0), axis=0, keepdims=True) for ei in picks], axis=0)
    gt_ref[...] = of_pick(gate)
    idx_ref[...] = jnp.concatenate(picks, axis=0)

    @pl.when(pl.program_id(0) == 0)
    def _():
        cnt_ref[...] = jnp.zeros_like(cnt_ref)

    upper = jnp.where(lax.broadcasted_iota(jnp.int32, (tm, tm), 0)
                      <= lax.broadcasted_iota(jnp.int32, (tm, tm), 1), 1.0, 0.0).astype(BF16)
    seen = cnt_ref[:, 0:1] + _dot(sel.astype(BF16), upper)
    pos_ref[...] = (of_pick(seen) - 1.0).astype(jnp.int32)
    cnt_ref[...] = jnp.broadcast_to(seen[:, tm - 1:tm], cnt_ref.shape)


def _router_call(x2, g, sc, sh, w_router_t, bias_col, seq, tm=256):
    T, D = x2.shape
    bmap = lambda i: ((i * tm) // seq, 0, 0)
    return pl.pallas_call(
        functools.partial(_router_kernel, tm=tm),
        grid=(T // tm,),
        in_specs=[pl.BlockSpec((tm, D), lambda i: (i, 0)),
                  pl.BlockSpec((1, D), lambda i: (0, 0)),
                  pl.BlockSpec((1, 1, D), bmap),
                  pl.BlockSpec((1, 1, D), bmap),
                  pl.BlockSpec((N_EXPERTS, D), lambda i: (0, 0)),
                  pl.BlockSpec((N_EXPERTS, 1), lambda i: (0, 0))],
        out_specs=[pl.BlockSpec((tm, D // 2), lambda i: (i, 0)),
                   pl.BlockSpec((TOP_K, tm), lambda i: (0, i)),
                   pl.BlockSpec((TOP_K, tm), lambda i: (0, i)),
                   pl.BlockSpec((TOP_K, tm), lambda i: (0, i)),
                   pl.BlockSpec((N_EXPERTS, LANES), lambda i: (0, 0))],
        out_shape=[jax.ShapeDtypeStruct((T, D // 2), jnp.uint32),
                   jax.ShapeDtypeStruct((TOP_K, T), F32),
                   jax.ShapeDtypeStruct((TOP_K, T), jnp.int32),
                   jax.ShapeDtypeStruct((TOP_K, T), jnp.int32),
                   jax.ShapeDtypeStruct((N_EXPERTS, LANES), F32)],
        compiler_params=_params(("arbitrary",)),
        name="router",
    )(x2, g, sc, sh, w_router_t, bias_col)


def _row_copy(src, src_row, dst, dst_row, sem):
    return pltpu.make_async_copy(src.at[pl.ds(src_row, 1), :], dst.at[pl.ds(dst_row, 1), :], sem)


def _dispatch_kernel(dest_ref, h_ref, xs_ref, sem, *, tm):
    def issue(t, carry):
        for j in range(TOP_K):
            _row_copy(h_ref, t, xs_ref, dest_ref[t * TOP_K + j], sem).start(priority=j % 2)
        return carry

    lax.fori_loop(0, tm, issue, 0)
    for _ in range(TOP_K):
        pltpu.make_async_copy(h_ref, xs_ref.at[pl.ds(0, tm), :], sem).wait()


def _dispatch_call(dest_flat, h2, tm=256):
    T, D = h2.shape
    return pl.pallas_call(
        functools.partial(_dispatch_kernel, tm=tm),
        grid=(T // tm,),
        in_specs=[pl.BlockSpec((tm * TOP_K,), lambda i: (i,), memory_space=pltpu.SMEM),
                  pl.BlockSpec((tm, D), lambda i: (i, 0))],
        out_specs=pl.BlockSpec(memory_space=pl.ANY),
        out_shape=jax.ShapeDtypeStruct((T * TOP_K, D), h2.dtype),
        scratch_shapes=[pltpu.SemaphoreType.DMA(())],
        compiler_params=_params(("arbitrary",)),
        name="dispatch",
    )(dest_flat, h2)


def _expert_kernel(tile_ref, exp_ref, nxt_ref, slot_ref, lo_ref, hi_ref, n_ref, x_ref, wg_hbm, wu_hbm, wd_hbm,
                   o_ref, wg_st, wu_st, wd_st, wg_bf, wu_bf, wd_bf, sem, *, te, layer):
    s = pl.program_id(0)

    def weight_copies(e, sl):
        return [pltpu.make_async_copy(src.at[layer, e], dst.at[sl], sem.at[k, sl])
                for k, (src, dst) in enumerate(((wg_hbm, wg_st), (wu_hbm, wu_st), (wd_hbm, wd_st)))]

    @pl.when(s < n_ref[0])
    def _():
        tile = tile_ref[s]
        prev = jnp.maximum(s - 1, 0)
        e = exp_ref[s]
        sl = slot_ref[s]

        @pl.when(s == 0)
        def _():
            for cp in weight_copies(e, sl):
                cp.start()

        @pl.when((s == 0) | (exp_ref[prev] != e))
        def _():
            for cp in weight_copies(e, sl):
                cp.wait()
            wg_bf[...] = wg_st[sl].astype(BF16)
            wu_bf[...] = wu_st[sl].astype(BF16)
            wd_bf[...] = wd_st[sl].astype(BF16)

            @pl.when(nxt_ref[s] >= 0)
            def _():
                for cp in weight_copies(nxt_ref[s], 1 - sl):
                    cp.start()

        xl, xh = (t.astype(BF16) for t in _unpack_halves(x_ref[...]))
        half = xl.shape[1]
        gate = _dot(xl, wg_bf[0:half, :]) + _dot(xh, wg_bf[half:, :])
        up = _dot(xl, wu_bf[0:half, :]) + _dot(xh, wu_bf[half:, :])
        y = _pack_halves(_dot((_silu(gate) * up).astype(BF16), wd_bf[...]))
        row0 = tile * te
        whole = (lo_ref[s] <= row0) & (hi_ref[s] >= row0 + te)
        first_visit = (s == 0) | (tile_ref[prev] != tile)
        rows = row0 + lax.broadcasted_iota(jnp.int32, (te, 1), 0)
        mine = (rows >= lo_ref[s]) & (rows < hi_ref[s])

        @pl.when(whole)
        def _():
            o_ref[...] = y

        @pl.when(jnp.logical_not(whole) & first_visit)
        def _():
            o_ref[...] = jnp.where(mine, y, jnp.uint32(0))

        @pl.when(jnp.logical_not(whole) & jnp.logical_not(first_visit))
        def _():
            o_ref[...] = jnp.where(mine, y, o_ref[...])


def _expert_call(meta, x_sorted, w_gate, w_up, w_down, layer):
    A, DH = x_sorted.shape
    _, E, D, DE = w_gate.shape
    te = EXPERT_TILE
    n_steps = A // te + E - 1
    hbm = pl.BlockSpec(memory_space=pl.ANY)
    grid_spec = pltpu.PrefetchScalarGridSpec(
        num_scalar_prefetch=7,
        grid=(n_steps,),
        in_specs=[pl.BlockSpec((te, DH), lambda s, tile, *_: (tile[s], 0)), hbm, hbm, hbm],
        out_specs=pl.BlockSpec((te, DH), lambda s, tile, *_: (tile[s], 0)),
        scratch_shapes=[pltpu.VMEM((2, D, DE), F32), pltpu.VMEM((2, D, DE), F32), pltpu.VMEM((2, DE, D), F32),
                        pltpu.VMEM((D, DE), BF16), pltpu.VMEM((D, DE), BF16), pltpu.VMEM((DE, D), BF16),
                        pltpu.SemaphoreType.DMA((3, 2))],
    )
    return pl.pallas_call(
        functools.partial(_expert_kernel, te=te, layer=layer),
        grid_spec=grid_spec,
        out_shape=jax.ShapeDtypeStruct((A, DH), jnp.uint32),
        compiler_params=_params(("arbitrary",)),
        name="experts",
    )(*meta, x_sorted, w_gate, w_up, w_down)


def _expert_steps(counts, n_rows):
    te = EXPERT_TILE
    E = counts.shape[0]
    n_steps = n_rows // te + E - 1
    ends = jnp.cumsum(counts)
    starts = ends - counts
    first_tile = starts // te
    last_tile = jnp.maximum(ends - 1, 0) // te
    n_tiles = jnp.where(counts > 0, last_tile - first_tile + 1, 0)
    step_end = jnp.cumsum(n_tiles)
    step_start = step_end - n_tiles
    total = step_end[-1]
    s = jnp.arange(n_steps, dtype=jnp.int32)
    s_eff = jnp.minimum(s, total - 1)
    exp = jnp.sum(step_end[None, :] <= s_eff[:, None], axis=1, dtype=jnp.int32)
    ids = jnp.arange(E, dtype=jnp.int32)
    used = counts > 0
    at_or_after = jnp.flip(lax.cummin(jnp.flip(jnp.where(used, ids, E))))
    nxt = jnp.concatenate([at_or_after[1:], jnp.full((1,), E, jnp.int32)])
    nxt = jnp.where(nxt >= E, -1, nxt)
    ordinal = jnp.cumsum(used.astype(jnp.int32)) - 1
    is_exp = exp[:, None] == ids[None, :]
    of_step = lambda v: jnp.sum(jnp.where(is_exp, v[None, :], 0), axis=1, dtype=jnp.int32)
    tile = of_step(first_tile) + s_eff - of_step(step_start)
    return (tile, exp, of_step(nxt), of_step(ordinal) % 2, of_step(starts), of_step(ends),
            total.reshape(1).astype(jnp.int32))


def _combine_kernel(dest_ref, dest_next_ref, x_ref, h_ref, w_ref, ys_ref, wsg_ref, wsu_ref, wsd_ref,
                    g2_ref, fg_ref, o_ref, buf, sem, *, tm, n_tiles, final_norm):
    i = pl.program_id(0)
    slot = i % 2

    def copy(dref, sl, t, j):
        return _row_copy(ys_ref, dref[t * TOP_K + j], buf.at[sl, j], t, sem.at[sl])

    def wait_rows(sl):
        for j in range(TOP_K):
            pltpu.make_async_copy(ys_ref.at[pl.ds(0, tm), :], buf.at[sl, j], sem.at[sl]).wait()

    @pl.when(i == 0)
    def _():
        def body(t, carry):
            for j in range(TOP_K):
                copy(dest_ref, 0, t, j).start(priority=j % 2)
            return carry

        lax.fori_loop(0, tm, body, 0)

    wait_rows(slot)
    for t in range(tm):
        for j in range(TOP_K):
            copy(dest_next_ref, 1 - slot, t, j).start(priority=j % 2)

    hl, hh = (t.astype(BF16) for t in _unpack_halves(h_ref[...]))
    half = hl.shape[1]
    gate = _dot(hl, wsg_ref[0:half, :]) + _dot(hh, wsg_ref[half:, :])
    up = _dot(hl, wsu_ref[0:half, :]) + _dot(hh, wsu_ref[half:, :])
    shared = _dot((_silu(gate) * up).astype(BF16), wsd_ref[...])
    w = w_ref[...]
    acc_lo = shared[:, :half]
    acc_hi = shared[:, half:]
    for j in range(TOP_K):
        lo, hi = _unpack_halves(buf[slot, j])
        acc_lo = acc_lo + lo * w[:, j:j + 1]
        acc_hi = acc_hi + hi * w[:, j:j + 1]
    acc = jnp.concatenate([acc_lo, acc_hi], axis=1)
    out = x_ref[...] + g2_ref[0] * acc
    if final_norm:
        ms = jnp.mean(out * out, axis=-1, keepdims=True)
        out = out * lax.rsqrt(ms + RMS_EPS) * fg_ref[...]
    o_ref[...] = out

    @pl.when(i == n_tiles - 1)
    def _():
        wait_rows(1 - slot)


def _combine_call(dest_flat, x2, h2, w_tok, y_sorted, w_sh_gate, w_sh_up, w_sh_down, g2, final_g,
                  seq, final_norm, tm=256):
    T, D = x2.shape
    n_tiles = T // tm
    const = lambda shape: pl.BlockSpec(shape, lambda i: (0,) * len(shape),
                                       pipeline_mode=pl.Buffered(1))
    return pl.pallas_call(
        functools.partial(_combine_kernel, tm=tm, n_tiles=n_tiles, final_norm=final_norm),
        grid=(n_tiles,),
        in_specs=[pl.BlockSpec((tm * TOP_K,), lambda i: (i,), memory_space=pltpu.SMEM),
                  pl.BlockSpec((tm * TOP_K,), lambda i: (jnp.minimum(i + 1, n_tiles - 1),),
                               memory_space=pltpu.SMEM),
                  pl.BlockSpec((tm, D), lambda i: (i, 0)),
                  pl.BlockSpec((tm, D // 2), lambda i: (i, 0)),
                  pl.BlockSpec((tm, TOP_K), lambda i: (i, 0)),
                  pl.BlockSpec(memory_space=pl.ANY),
                  const(w_sh_gate.shape),
                  const(w_sh_up.shape),
                  const(w_sh_down.shape),
                  pl.BlockSpec((1, 1, D), lambda i: ((i * tm) // seq, 0, 0)),
                  const(final_g.shape)],
        out_specs=pl.BlockSpec((tm, D), lambda i: (i, 0)),
        out_shape=jax.ShapeDtypeStruct((T, D), F32),
        scratch_shapes=[pltpu.VMEM((2, TOP_K, tm, D // 2), jnp.uint32), pltpu.SemaphoreType.DMA((2,))],
        compiler_params=_params(("arbitrary",)),
        name="combine",
    )(dest_flat, dest_flat, x2, h2, w_tok, y_sorted, w_sh_gate, w_sh_up, w_sh_down, g2, final_g)


def _pad_cols(w, n):
    return jnp.pad(w, ((0, 0), (0, n - w.shape[1])))


def _pad_rows(w, n):
    return jnp.pad(w, ((0, n - w.shape[0]), (0, 0)))


def _pack_cols(parts):
    return jnp.concatenate([_pad_cols(a, n) for a, n in parts], axis=1)


def kernel(x, c, ada_w, ada_b, norm1_g, norm2_g, w_in, mu_shift, w_decay_up, decay_base, w_a_up, a_base, w_g_up, k_k, k_a, r_k, lnx_g, lnx_b, w_v_down, mu_v, w_v_up, v_base, pool_w, pool_scale, w_br_rwkv, w_br_pool, w_out, w_router, router_bias, w_exp_gate, w_exp_up, w_exp_down, w_sh_gate, w_sh_up, w_sh_down, final_g):
    B, S, D = x.shape
    T = B * S
    L = ada_w.shape[0]
    DR = w_decay_up.shape[2]
    n_dec, n_a, n_g, n_v = w_decay_up.shape[1], w_a_up.shape[1], w_g_up.shape[1], w_v_down.shape[2]
    o_k, o_v, o_wd = DR, 2 * DR, 3 * DR
    o_ad = o_wd + n_dec
    o_gd = o_ad + n_a
    n_shift = o_gd + n_g
    o_gate = n_shift + DR

    c_pad = jnp.pad(c, ((0, 8 - B), (0, 0)))
    mod = _ada_call(c_pad, ada_w, ada_b)[:, :B]
    mod = mod.reshape(L, B, 6, 1, D)

    x2 = x.reshape(T, D)
    proj_first = None
    for l in range(L):
        sh1, sc1, g1, sh2, sc2, g2 = (mod[l, :, q] for q in range(6))
        has_vres = l > 0
        wl = w_in[l]
        zero_cols = jnp.zeros((D, 0), F32)
        vd_w = w_v_down[l - 1] if has_vres else zero_cols
        w_packed = _pack_cols([
            (wl[:, :o_wd], 3 * DR),
            (wl[:, o_wd:o_ad], PAIR), (wl[:, o_ad:o_gd], PAIR), (wl[:, o_gd:n_shift], 2 * PAIR),
            (vd_w, PAIR), (zero_cols, PROJ_TN - 5 * PAIR),
            (wl[:, o_gate:], 2 * D),
            (wl[:, n_shift:o_gate], DR)]).astype(BF16)
        ml = mu_shift[l][None, :]
        zero_mu = jnp.zeros((1, 0), F32)
        vd_mu = mu_v[l - 1][None, :] if has_vres else zero_mu
        mu_packed = _pack_cols([
            (ml[:, :o_wd], 3 * DR),
            (ml[:, o_wd:o_ad], PAIR), (ml[:, o_ad:o_gd], PAIR), (ml[:, o_gd:n_shift], 2 * PAIR),
            (vd_mu, PAIR), (zero_mu, PROJ_TN - 5 * PAIR), (zero_mu, 2 * D + DR)])
        proj = _proj_call(x2, norm1_g[l][None, :], sc1, sh1, w_packed, mu_packed, S)
        if l == 0:
            proj_first = proj

        lora_w = {"decay": _pad_rows(w_decay_up[l], PAIR), "a": _pad_rows(w_a_up[l], PAIR),
                  "g": w_g_up[l]}
        zeros_dr = jnp.zeros((DR,), F32)
        if has_vres:
            lora_w["v"] = _pad_rows(w_v_up[l - 1], PAIR)
        vecs = jnp.stack([decay_base[l], a_base[l], k_k[l], k_a[l], r_k[l].reshape(DR), lnx_g[l],
                          lnx_b[l], v_base[l - 1] if has_vres else zeros_dr])
        y_rwkv = _rwkv_call(proj.reshape(B, S, -1), proj_first.reshape(B, S, -1), lora_w, vecs,
                            has_vres).reshape(T, DR)

        x2 = _merge_call(x2, y_rwkv, proj, pool_w[l].astype(BF16), pool_scale[l][None, :],
                         w_br_rwkv[l].astype(BF16), w_br_pool[l].astype(BF16),
                         w_out[l].astype(BF16), g1, S)

        h2, w_t, idx_t, pos_t, cnt = _router_call(x2, norm2_g[l][None, :], sc2, sh2, w_router[l].T,
                                                  router_bias[l][:, None], S)
        counts = cnt[:, 0].astype(jnp.int32)
        start = jnp.cumsum(counts) - counts
        experts = jnp.arange(N_EXPERTS, dtype=jnp.int32)[None, :, None]
        dest = pos_t + jnp.sum(jnp.where(idx_t[:, None, :] == experts, start[None, :, None], 0), axis=1)
        dest_flat = dest.T.reshape(T * TOP_K).astype(jnp.int32)
        w_tok = w_t.T

        x_sorted = _dispatch_call(dest_flat, h2)
        meta = _expert_steps(counts, T * TOP_K)
        y_sorted = _expert_call(meta, x_sorted, w_exp_gate, w_exp_up, w_exp_down, l)
        x2 = _combine_call(dest_flat, x2, h2, w_tok, y_sorted, w_sh_gate[l].astype(BF16),
                           w_sh_up[l].astype(BF16), w_sh_down[l].astype(BF16), g2,
                           final_g[None, :], S, final_norm=(l == L - 1))
    return x2.reshape(B, S, D)
```
